```python
import math
import jax, jax.numpy as jnp
from jax import lax
import numpy as np

D_MODEL = 2048
BATCH = 4
SEQ = 4096
DEPTH = 4

D_MIX = D_MODEL
NORM_EPS = 1e-6
ML_HEADS = 4
ML_HEAD_DIM = 128
ML_WIDTH = ML_HEADS * ML_HEAD_DIM
ML_CONV = 4
ML_CHUNK = 64
MLA_HEADS = 8
MLA_NOPE = 128
MLA_ROPE = 64
MLA_V = 128
MLA_WIDTH = MLA_HEADS * MLA_V
MLA_Q_RANK = 512
MLA_KV_RANK = 256
ROPE_BASE = 10000.0
Q_BLOCK = 128
RW_HEAD_DIM = 64
RW_WIDTH = D_MIX - ML_WIDTH - MLA_WIDTH
RW_HEADS = RW_WIDTH // RW_HEAD_DIM
RW_DECAY_RANK = 64
RW_AAA_RANK = 64
RW_MV_RANK = 32
RW_GN_EPS = 64e-5
RW_MIX_WIDTH = 3 * RW_WIDTH + RW_DECAY_RANK + RW_AAA_RANK

IN_SIZES = (2 * ML_WIDTH, ML_WIDTH, ML_HEADS, ML_HEADS, ML_WIDTH, ML_WIDTH,
            MLA_Q_RANK, MLA_KV_RANK, MLA_ROPE, MLA_WIDTH,
            RW_MIX_WIDTH, RW_WIDTH)
D_IN = (5 * ML_WIDTH + 2 * ML_HEADS + MLA_Q_RANK + MLA_KV_RANK + MLA_ROPE + MLA_WIDTH
        + RW_MIX_WIDTH + RW_WIDTH)

kernel_name = 'hymba_style_mlstm_mla_rwkv7_trunk'


def split_offsets():
    return [int(o) for o in np.cumsum(np.array(IN_SIZES))[:-1]]


def rms_norm(x, g, eps=NORM_EPS):
    xf = x.astype(jnp.float32)
    y = xf * lax.rsqrt(jnp.mean(xf * xf, axis=-1, keepdims=True) + eps)
    return (y * g.astype(jnp.float32)).astype(x.dtype)


def head_layer_norm(x, g, b, eps):
    xf = x.astype(jnp.float32)
    mu = jnp.mean(xf, axis=-1, keepdims=True)
    var = jnp.mean(jnp.square(xf - mu), axis=-1, keepdims=True)
    y = (xf - mu) * lax.rsqrt(var + eps) * g.astype(jnp.float32)
    if b is not None:
        y = y + b.astype(jnp.float32)
    return y


def token_shift(x):
    return jnp.pad(x, ((0, 0), (1, 0), (0, 0)))[:, :-1]


def causal_conv(x, w, b):
    T = x.shape[1]
    K = w.shape[0]
    xp = jnp.pad(x, ((0, 0), (K - 1, 0), (0, 0)))
    y = b
    for j in range(K):
        y = y + w[j] * xp[:, j:j + T]
    return y


def mlstm_chunkwise(q, k, v, log_i, log_f):
    B, H, T, d = q.shape
    L = ML_CHUNK
    nc = T // L
    q = q * (d ** -0.5)

    def to_chunks(a):
        return jnp.moveaxis(a.reshape((B, H, nc, L) + a.shape[3:]), 2, 0)

    causal = jnp.tril(jnp.ones((L, L), dtype=bool))

    def step(carry, inp):
        C, n, m = carry
        qc, kc, vc, li, lf = inp
        b = jnp.cumsum(lf, axis=-1)
        m_inter = b + m[..., None]
        D = jnp.where(causal, b[..., :, None] - b[..., None, :] + li[..., None, :], -jnp.inf)
        m_t = jnp.maximum(m_inter, jnp.max(D, axis=-1))
        inter = jnp.exp(m_inter - m_t)
        S = jnp.einsum('bhld,bhsd->bhls', qc, kc) * jnp.exp(D - m_t[..., None])
        num = inter[..., None] * jnp.einsum('bhvk,bhlk->bhlv', C, qc) + jnp.einsum('bhls,bhsv->bhlv', S, vc)
        den = inter * jnp.einsum('bhk,bhlk->bhl', n, qc) + jnp.sum(S, axis=-1)
        h = num / jnp.maximum(jnp.abs(den), jnp.exp(-m_t))[..., None]
        b_last = b[..., -1]
        g_s = b_last[..., None] - b + li
        m_new = jnp.maximum(b_last + m, jnp.max(g_s, axis=-1))
        decay = jnp.exp(b_last + m - m_new)
        w_s = jnp.exp(g_s - m_new[..., None])
        C = decay[..., None, None] * C + jnp.einsum('bhs,bhsv,bhsk->bhvk', w_s, vc, kc)
        n = decay[..., None] * n + jnp.einsum('bhs,bhsk->bhk', w_s, kc)
        return (C, n, m_new), h

    init = (jnp.zeros((B, H, d, d), jnp.float32), jnp.zeros((B, H, d), jnp.float32),
            jnp.zeros((B, H), jnp.float32))
    _, h = lax.scan(step, init, (to_chunks(q), to_chunks(k), to_chunks(v),
                                 to_chunks(log_i), to_chunks(log_f)))
    return jnp.moveaxis(h, 0, 2).reshape(B, H, T, d)


def mlstm_branch(ml_qk, ml_v, ml_i, ml_f, ml_o, conv_w, conv_b, i_bias, f_bias, norm_g):
    B, T, _ = ml_qk.shape
    qk = jax.nn.silu(causal_conv(ml_qk, conv_w, conv_b))
    q, k = jnp.split(qk, 2, axis=-1)

    def to_heads(t):
        return t.astype(jnp.float32).reshape(B, T, ML_HEADS, ML_HEAD_DIM).transpose(0, 2, 1, 3)

    log_i = (ml_i + i_bias).astype(jnp.float32).transpose(0, 2, 1)
    log_f = jax.nn.log_sigmoid((ml_f + f_bias).astype(jnp.float32)).transpose(0, 2, 1)
    h = mlstm_chunkwise(to_heads(q), to_heads(k), to_heads(ml_v), log_i, log_f)
    h = head_layer_norm(h.transpose(0, 2, 1, 3), norm_g.reshape(ML_HEADS, ML_HEAD_DIM), None, NORM_EPS)
    return jax.nn.sigmoid(ml_o.astype(jnp.float32)) * h.reshape(B, T, ML_WIDTH)


def rope_cos_sin(positions):
    inv_freq = jnp.power(ROPE_BASE, -jnp.arange(0, MLA_ROPE, 2, dtype=jnp.float32) / MLA_ROPE)
    ang = positions.astype(jnp.float32)[..., None] * inv_freq
    return jnp.cos(ang)[:, :, None, :], jnp.sin(ang)[:, :, None, :]


def apply_rope(x, cos, sin):
    half = x.shape[-1] // 2
    x1, x2 = x[..., :half], x[..., half:]
    cos = cos.astype(x.dtype)
    sin = sin.astype(x.dtype)
    return jnp.concatenate([x1 * cos - x2 * sin, x2 * cos + x1 * sin], axis=-1)


def mla_branch(c_q, c_kv, k_rope, cos, sin, q_norm_g, w_uq, kv_norm_g, w_ukv):
    B, T, _ = c_q.shape
    q = (rms_norm(c_q, q_norm_g) @ w_uq).reshape(B, T, MLA_HEADS, MLA_NOPE + MLA_ROPE)
    q_nope, q_rope = q[..., :MLA_NOPE], apply_rope(q[..., MLA_NOPE:], cos, sin)
    kv = (rms_norm(c_kv, kv_norm_g) @ w_ukv).reshape(B, T, MLA_HEADS, MLA_NOPE + MLA_V)
    k_nope, v = kv[..., :MLA_NOPE], kv[..., MLA_NOPE:]
    k_r = apply_rope(k_rope[:, :, None, :], cos, sin)[:, :, 0]
    scale = (MLA_NOPE + MLA_ROPE) ** -0.5
    outs = []
    for blk in range(T // Q_BLOCK):
        s0 = blk * Q_BLOCK
        s1 = s0 + Q_BLOCK
        scores = (jnp.einsum('bqhd,bkhd->bhqk', q_nope[:, s0:s1], k_nope[:, :s1])
                  + jnp.einsum('bqhr,bkr->bhqk', q_rope[:, s0:s1], k_r[:, :s1]))
        scores = scores.astype(jnp.float32) * scale
        mask = (s0 + jnp.arange(Q_BLOCK))[:, None] >= jnp.arange(s1)[None, :]
        p = jax.nn.softmax(jnp.where(mask, scores, -jnp.inf), axis=-1).astype(v.dtype)
        outs.append(jnp.einsum('bhqk,bkhv->bqhv', p, v[:, :s1]))
    return jnp.concatenate(outs, axis=1).reshape(B, T, MLA_WIDTH)


def rwkv7_scan(r, w, k, v, a_vec, b_vec):
    B, T, H, N = r.shape

    def step(S, inp):
        rt, wt, kt, vt, at, bt = inp
        sa = jnp.einsum('bhvk,bhk->bhv', S, at)
        S = S * wt[:, :, None, :] + sa[..., None] * bt[:, :, None, :] + vt[..., None] * kt[:, :, None, :]
        return S, jnp.einsum('bhvk,bhk->bhv', S, rt)

    xs = tuple(jnp.moveaxis(t, 1, 0) for t in (r, w, k, v, a_vec, b_vec))
    _, y = lax.scan(step, jnp.zeros((B, H, N, N), jnp.float32), xs)
    return jnp.moveaxis(y, 0, 1)


def rwkv7_branch(rw_mix, v_first, vres, mu, w0, w2, a0, a2, k_k, k_a, r_k, ln_g, ln_b):
    B, T, _ = rw_mix.shape
    xm = rw_mix.astype(jnp.float32)
    xs = xm + mu * (token_shift(xm) - xm)
    r, k, v, xw, xa = jnp.split(xs, [RW_WIDTH, 2 * RW_WIDTH, 3 * RW_WIDTH,
                                     3 * RW_WIDTH + RW_DECAY_RANK], axis=-1)
    log_w = -jax.nn.softplus(-(w0 + jnp.tanh(xw) @ w2)) - 0.5
    decay = jnp.exp(-jnp.exp(log_w))
    a = jax.nn.sigmoid(a0 + xa @ a2)
    if vres is None:
        v_first = v
    else:
        v0, v1, v2 = vres
        v = v + (v_first - v) * jax.nn.sigmoid(v0 + (v @ v1) @ v2)

    def heads(t):
        return t.reshape(B, T, RW_HEADS, RW_HEAD_DIM)

    kk = heads(k * k_k)
    kk = kk / jnp.maximum(jnp.sqrt(jnp.sum(kk * kk, axis=-1, keepdims=True)), 1e-12)
    a_h = heads(a)
    k_h = heads(k * (1.0 + (a - 1.0) * k_a))
    r_h, v_h = heads(r), heads(v)
    y = rwkv7_scan(r_h, heads(decay), k_h, v_h, -kk, kk * a_h)
    y = head_layer_norm(y, ln_g.reshape(RW_HEADS, RW_HEAD_DIM), ln_b.reshape(RW_HEADS, RW_HEAD_DIM), RW_GN_EPS)
    y = y + jnp.sum(r_h * k_h * r_k, axis=-1, keepdims=True) * v_h
    return y.reshape(B, T, RW_WIDTH), v_first


def setup_inputs(seed: int = 0) -> dict:
    key = jax.random.key(seed)
    ks = jax.random.split(key, 32)

    def nrm(k, shape, scale):
        return scale * jax.random.normal(k, shape, jnp.float32)

    def uni(k, shape, lo, hi):
        return jax.random.uniform(k, shape, jnp.float32, lo, hi)

    x = jax.random.normal(ks[0], (BATCH, SEQ, D_MODEL), jnp.float32)
    offs = jax.random.randint(ks[1], (BATCH, 1), 0, 1024)
    positions = (jnp.arange(SEQ)[None, :] + offs).astype(jnp.int32)
    return {
        'x': x,
        'positions': positions,
        'norm_g': 1.0 + nrm(ks[2], (DEPTH, D_MODEL), 0.02),
        'w_in': nrm(ks[3], (DEPTH, D_MODEL, D_IN), D_MODEL ** -0.5),
        'ml_conv_w': nrm(ks[4], (DEPTH, ML_CONV, 2 * ML_WIDTH), 0.5),
        'ml_conv_b': nrm(ks[5], (DEPTH, 2 * ML_WIDTH), 0.01),
        'ml_i_bias': nrm(ks[6], (DEPTH, ML_HEADS), 0.1),
        'ml_f_bias': uni(ks[7], (DEPTH, ML_HEADS), 3.0, 6.0),
        'ml_norm_g': 1.0 + nrm(ks[8], (DEPTH, ML_WIDTH), 0.02),
        'mla_q_norm_g': 1.0 + nrm(ks[9], (DEPTH, MLA_Q_RANK), 0.02),
        'mla_w_uq': nrm(ks[10], (DEPTH, MLA_Q_RANK, MLA_HEADS * (MLA_NOPE + MLA_ROPE)), MLA_Q_RANK ** -0.5),
        'mla_kv_norm_g': 1.0 + nrm(ks[11], (DEPTH, MLA_KV_RANK), 0.02),
        'mla_w_ukv': nrm(ks[12], (DEPTH, MLA_KV_RANK, MLA_HEADS * (MLA_NOPE + MLA_V)), MLA_KV_RANK ** -0.5),
        'rw_mu': uni(ks[13], (DEPTH, RW_MIX_WIDTH), 0.0, 1.0),
        'rw_w0': uni(ks[14], (DEPTH, RW_WIDTH), -4.0, 0.0),
        'rw_w2': nrm(ks[15], (DEPTH, RW_DECAY_RANK, RW_WIDTH), 0.5 * RW_DECAY_RANK ** -0.5),
        'rw_a0': nrm(ks[16], (DEPTH, RW_WIDTH), 0.1),
        'rw_a2': nrm(ks[17], (DEPTH, RW_AAA_RANK, RW_WIDTH), 0.5 * RW_AAA_RANK ** -0.5),
        'rw_v0': nrm(ks[18], (DEPTH - 1, RW_WIDTH), 0.1),
        'rw_v1': nrm(ks[19], (DEPTH - 1, RW_WIDTH, RW_MV_RANK), RW_WIDTH ** -0.5),
        'rw_v2': nrm(ks[20], (DEPTH - 1, RW_MV_RANK, RW_WIDTH), 0.5 * RW_MV_RANK ** -0.5),
        'rw_k_k': 0.85 + nrm(ks[21], (DEPTH, RW_WIDTH), 0.05),
        'rw_k_a': 1.0 + nrm(ks[22], (DEPTH, RW_WIDTH), 0.05),
        'rw_r_k': nrm(ks[23], (DEPTH, RW_HEADS, RW_HEAD_DIM), 0.1),
        'rw_ln_g': 1.0 + nrm(ks[24], (DEPTH, RW_WIDTH), 0.02),
        'rw_ln_b': nrm(ks[25], (DEPTH, RW_WIDTH), 0.01),
        'w_out': nrm(ks[26], (DEPTH, D_MIX, D_MODEL), 0.5 * D_MIX ** -0.5),
        'final_norm_g': 1.0 + nrm(ks[27], (D_MODEL,), 0.02),
    }


def reference(x, positions, norm_g, w_in, ml_conv_w, ml_conv_b, ml_i_bias, ml_f_bias, ml_norm_g,
              mla_q_norm_g, mla_w_uq, mla_kv_norm_g, mla_w_ukv,
              rw_mu, rw_w0, rw_w2, rw_a0, rw_a2, rw_v0, rw_v1, rw_v2, rw_k_k, rw_k_a, rw_r_k,
              rw_ln_g, rw_ln_b, w_out, final_norm_g):
    cos, sin = rope_cos_sin(positions)
    offsets = split_offsets()
    v_first = None
    for l in range(DEPTH):
        h = rms_norm(x, norm_g[l])
        proj = h @ w_in[l]
        (ml_qk, ml_v, ml_i, ml_f, ml_o, ml_z, mla_cq, mla_ckv, mla_kr, mla_z,
         rw_mix, rw_z) = jnp.split(proj, offsets, axis=-1)
        y_ml = mlstm_branch(ml_qk, ml_v, ml_i, ml_f, ml_o, ml_conv_w[l], ml_conv_b[l],
                            ml_i_bias[l], ml_f_bias[l], ml_norm_g[l])
        y_mla = mla_branch(mla_cq, mla_ckv, mla_kr, cos, sin, mla_q_norm_g[l], mla_w_uq[l],
                           mla_kv_norm_g[l], mla_w_ukv[l])
        vres = None if l == 0 else (rw_v0[l - 1], rw_v1[l - 1], rw_v2[l - 1])
        y_rw, v_first = rwkv7_branch(rw_mix, v_first, vres, rw_mu[l], rw_w0[l], rw_w2[l], rw_a0[l],
                                     rw_a2[l], rw_k_k[l], rw_k_a[l], rw_r_k[l], rw_ln_g[l], rw_ln_b[l])
        mixed = jnp.concatenate([jax.nn.silu(ml_z) * y_ml.astype(x.dtype),
                                 jax.nn.silu(mla_z) * y_mla.astype(x.dtype),
                                 jax.nn.silu(rw_z) * y_rw.astype(x.dtype)], axis=-1)
        x = x + mixed @ w_out[l]
    return rms_norm(x, final_norm_g)
```

```python
import functools

import jax
import jax.numpy as jnp
from jax import lax
from jax.experimental import pallas as pl
from jax.experimental.pallas import tpu as pltpu

F32 = jnp.float32
BF16 = jnp.bfloat16

D_MODEL = 2048
NORM_EPS = 1e-6
ML_HEADS = 4
ML_HEAD_DIM = 128
ML_WIDTH = 512
ML_CONV = 4
MLA_HEADS = 8
MLA_NOPE = 128
MLA_ROPE = 64
MLA_V = 128
MLA_WIDTH = 1024
MLA_Q_RANK = 512
MLA_KV_RANK = 256
ROPE_BASE = 10000.0
RW_HEAD_DIM = 64
RW_WIDTH = 512
RW_HEADS = 8
RW_DECAY_RANK = 64
RW_AAA_RANK = 64
RW_MV_RANK = 32
RW_GN_EPS = 64e-5

LANES = 128
SUBLANES = 8
MLA_QK_PAD = 256
VMEM_LIMIT = 48 * 1024 * 1024
NEG_BIG = -1e30

OFF_QK = 0
OFF_MLV = 1024
OFF_MLO = 1536
OFF_MLZ = 2048
OFF_CQ = 2560
OFF_CKV = 3072
OFF_KRG = 3328
OFF_WA = 3456
OFF_RWR = 3584
OFF_RWK = 4096
OFF_RWV = 4608
OFF_RWZ = 5120
OFF_MLAZ = 5632
D_IN_PAD = 6656

TM_PROJ = 512
TN_PROJ = 1664
TM_OUT = 512
TM_MLA = 512
TQ = 512
ML_TT = 1024
ML_L = 256
RW_TM = 512
RW_TT = 512
RW_L = 64


def _split_bf16(x):
    hi = x.astype(BF16)
    lo = (x - hi.astype(F32)).astype(BF16)
    return hi, lo


def _mm(a, b, passes=1):
    if passes == 1:
        return jnp.dot(a.astype(BF16), b.astype(BF16), preferred_element_type=F32)
    ah, al = _split_bf16(a)
    bh, bl = _split_bf16(b)
    return (jnp.dot(ah, bh, preferred_element_type=F32)
            + jnp.dot(ah, bl, preferred_element_type=F32)
            + jnp.dot(al, bh, preferred_element_type=F32))


_NT = (((1,), (1,)), ((), ()))


def _mm_nt(a, b, passes=1):
    if passes == 1:
        return lax.dot_general(a.astype(BF16), b.astype(BF16), _NT, preferred_element_type=F32)
    ah, al = _split_bf16(a)
    bh, bl = _split_bf16(b)
    return (lax.dot_general(ah, bh, _NT, preferred_element_type=F32)
            + lax.dot_general(ah, bl, _NT, preferred_element_type=F32)
            + lax.dot_general(al, bh, _NT, preferred_element_type=F32))


def _sigmoid(x):
    return 1.0 / (1.0 + jnp.exp(-x))


def _silu(x):
    return x * _sigmoid(x)


def _log_sigmoid(x):
    return jnp.minimum(x, 0.0) - jnp.log(1.0 + jnp.exp(-jnp.abs(x)))


def _params(*sem):
    return pltpu.CompilerParams(dimension_semantics=sem, vmem_limit_bytes=VMEM_LIMIT)


def _inproj_kernel(x_ref, g_ref, w_ref, o_ref, h_ref):
    @pl.when(pl.program_id(1) == 0)
    def _():
        x = x_ref[...]
        ms = jnp.mean(x * x, axis=-1, keepdims=True)
        h_ref[...] = (x * lax.rsqrt(ms + NORM_EPS) * g_ref[...]).astype(BF16)

    o_ref[...] = jnp.dot(h_ref[...], w_ref[...], preferred_element_type=F32)


def _inproj(x2, norm_g3, w_in_p, layer):
    m = x2.shape[0]
    return pl.pallas_call(
        _inproj_kernel,
        grid=(m // TM_PROJ, D_IN_PAD // TN_PROJ),
        in_specs=[
            pl.BlockSpec((TM_PROJ, D_MODEL), lambda i, j: (i, 0)),
            pl.BlockSpec((None, 1, D_MODEL), lambda i, j: (layer, 0, 0)),
            pl.BlockSpec((None, D_MODEL, TN_PROJ), lambda i, j: (layer, 0, j)),
        ],
        out_specs=pl.BlockSpec((TM_PROJ, TN_PROJ), lambda i, j: (i, j)),
        out_shape=jax.ShapeDtypeStruct((m, D_IN_PAD), F32),
        scratch_shapes=[pltpu.VMEM((TM_PROJ, D_MODEL), BF16)],
        compiler_params=_params("parallel", "arbitrary"),
        name="inproj",
    )(x2, norm_g3, w_in_p)


def _outproj_kernel(x_ref, yml_ref, ymla_ref, yrw_ref, w_ref, g_ref, o_ref, *, final):
    acc = x_ref[...]
    acc = acc + jnp.dot(yml_ref[...], w_ref[0:ML_WIDTH, :], preferred_element_type=F32)
    acc = acc + jnp.dot(ymla_ref[...], w_ref[ML_WIDTH:ML_WIDTH + MLA_WIDTH, :],
                        preferred_element_type=F32)
    acc = acc + jnp.dot(yrw_ref[...], w_ref[ML_WIDTH + MLA_WIDTH:, :], preferred_element_type=F32)
    if final:
        ms = jnp.mean(acc * acc, axis=-1, keepdims=True)
        acc = acc * lax.rsqrt(ms + NORM_EPS) * g_ref[...]
    o_ref[...] = acc


def _outproj(x2, y_ml, y_mla, y_rw, w_out_b, final_g2, layer, final):
    m = x2.shape[0]
    return pl.pallas_call(
        functools.partial(_outproj_kernel, final=final),
        grid=(m // TM_OUT,),
        in_specs=[
            pl.BlockSpec((TM_OUT, D_MODEL), lambda i: (i, 0)),
            pl.BlockSpec((TM_OUT, ML_WIDTH), lambda i: (i, 0)),
            pl.BlockSpec((TM_OUT, MLA_WIDTH), lambda i: (i, 0)),
            pl.BlockSpec((TM_OUT, RW_WIDTH), lambda i: (i, 0)),
            pl.BlockSpec((None, D_MODEL, D_MODEL), lambda i: (layer, 0, 0)),
            pl.BlockSpec((1, D_MODEL), lambda i: (0, 0)),
        ],
        out_specs=pl.BlockSpec((TM_OUT, D_MODEL), lambda i: (i, 0)),
        out_shape=jax.ShapeDtypeStruct((m, D_MODEL), F32),
        compiler_params=_params("parallel"),
        name="outproj",
    )(x2, y_ml, y_mla, y_rw, w_out_b, final_g2)


def _mla_proj_kernel(cq_ref, ckv_ref, krg_ref, cos_ref, s1_ref, s2_ref, qg_ref, wq_ref, kvg_ref,
                     wk_ref, wv_ref, q_out, k_out, v_out):
    cq = cq_ref[...]
    cqn = (cq * lax.rsqrt(jnp.mean(cq * cq, axis=-1, keepdims=True) + NORM_EPS)
           * qg_ref[...]).astype(BF16)
    q = jnp.dot(cqn, wq_ref[...], preferred_element_type=F32)
    ckv = ckv_ref[...]
    ckvn = (ckv * lax.rsqrt(jnp.mean(ckv * ckv, axis=-1, keepdims=True) + NORM_EPS)
            * kvg_ref[...]).astype(BF16)
    kn = jnp.dot(ckvn, wk_ref[...], preferred_element_type=F32)
    v = jnp.dot(ckvn, wv_ref[...], preferred_element_type=F32)
    cosf, s1, s2 = cos_ref[...], s1_ref[...], s2_ref[...]

    def rope(xb):
        return (xb * cosf + pltpu.roll(xb, LANES - MLA_ROPE // 2, 1) * s1
                + pltpu.roll(xb, MLA_ROPE // 2, 1) * s2)

    scale = (MLA_NOPE + MLA_ROPE) ** -0.5
    kr = rope(krg_ref[...]).astype(BF16)
    for h in range(MLA_HEADS):
        c0 = h * MLA_QK_PAD
        q_out[:, c0:c0 + MLA_NOPE] = (q[:, c0:c0 + MLA_NOPE] * scale).astype(BF16)
        q_out[:, c0 + MLA_NOPE:c0 + MLA_QK_PAD] = (
            rope(q[:, c0 + MLA_NOPE:c0 + MLA_QK_PAD]) * scale).astype(BF16)
        k_out[:, c0:c0 + MLA_NOPE] = kn[:, h * MLA_NOPE:(h + 1) * MLA_NOPE].astype(BF16)
        k_out[:, c0 + MLA_NOPE:c0 + MLA_QK_PAD] = kr
    v_out[...] = v.astype(BF16)


def _mla_proj(proj, cosf, s1, s2, qg3, wq_p, kvg3, wk_p, wv_p, layer):
    m = proj.shape[0]
    tm = TM_MLA
    row = lambda i: (i, 0)
    return pl.pallas_call(
        _mla_proj_kernel,
        grid=(m // tm,),
        in_specs=[
            pl.BlockSpec((tm, MLA_Q_RANK), lambda i: (i, OFF_CQ // MLA_Q_RANK)),
            pl.BlockSpec((tm, MLA_KV_RANK), lambda i: (i, OFF_CKV // MLA_KV_RANK)),
            pl.BlockSpec((tm, LANES), lambda i: (i, OFF_KRG // LANES)),
            pl.BlockSpec((tm, LANES), row),
            pl.BlockSpec((tm, LANES), row),
            pl.BlockSpec((tm, LANES), row),
            pl.BlockSpec((None, 1, MLA_Q_RANK), lambda i: (layer, 0, 0)),
            pl.BlockSpec((None, MLA_Q_RANK, MLA_HEADS * MLA_QK_PAD), lambda i: (layer, 0, 0)),
            pl.BlockSpec((None, 1, MLA_KV_RANK), lambda i: (layer, 0, 0)),
            pl.BlockSpec((None, MLA_KV_RANK, MLA_HEADS * MLA_NOPE), lambda i: (layer, 0, 0)),
            pl.BlockSpec((None, MLA_KV_RANK, MLA_HEADS * MLA_V), lambda i: (layer, 0, 0)),
        ],
        out_specs=[
            pl.BlockSpec((tm, MLA_HEADS * MLA_QK_PAD), row),
            pl.BlockSpec((tm, MLA_HEADS * MLA_QK_PAD), row),
            pl.BlockSpec((tm, MLA_HEADS * MLA_V), row),
        ],
        out_shape=[
            jax.ShapeDtypeStruct((m, MLA_HEADS * MLA_QK_PAD), BF16),
            jax.ShapeDtypeStruct((m, MLA_HEADS * MLA_QK_PAD), BF16),
            jax.ShapeDtypeStruct((m, MLA_HEADS * MLA_V), BF16),
        ],
        compiler_params=_params("parallel"),
        name="mla_proj",
    )(proj, proj, proj, cosf, s1, s2, qg3, wq_p, kvg3, wk_p, wv_p)


def _attn_kernel(q_ref, k_ref, v_ref, z_ref, o_ref, *, tq):
    i = pl.program_id(2)
    q = q_ref[...]

    def step(j, carry, masked):
        m, l, acc = carry
        off = pl.multiple_of(j * tq, tq)
        kb = k_ref[pl.ds(off, tq), :]
        vb = v_ref[pl.ds(off, tq), :]
        s = lax.dot_general(q, kb, _NT, preferred_element_type=F32)
        if masked:
            rows = lax.broadcasted_iota(jnp.int32, (tq, tq), 0)
            cols = lax.broadcasted_iota(jnp.int32, (tq, tq), 1)
            s = jnp.where(rows >= cols, s, NEG_BIG)
        m_new = jnp.maximum(m, jnp.max(s, axis=-1, keepdims=True))
        alpha = jnp.exp(m - m_new)
        p = jnp.exp(s - m_new)
        l = alpha * l + jnp.sum(p, axis=-1, keepdims=True)
        acc = alpha * acc + jnp.dot(p.astype(BF16), vb, preferred_element_type=F32)
        return m_new, l, acc

    init = (jnp.full((tq, 1), NEG_BIG, F32), jnp.zeros((tq, 1), F32), jnp.zeros((tq, MLA_V), F32))
    carry = lax.fori_loop(0, i, lambda j, c: step(j, c, False), init)
    _, l, acc = step(i, carry, True)
    y = acc / l
    o_ref[...] = (_silu(z_ref[...]) * y).astype(BF16)


def _attention(q, k, v, proj, batch, seq):
    m = q.shape[0]
    nq = seq // TQ
    return pl.pallas_call(
        functools.partial(_attn_kernel, tq=TQ),
        grid=(batch, MLA_HEADS, nq),
        in_specs=[
            pl.BlockSpec((TQ, MLA_QK_PAD), lambda b, h, i: (b * nq + i, h)),
            pl.BlockSpec((seq, MLA_QK_PAD), lambda b, h, i: (b, h)),
            pl.BlockSpec((seq, MLA_V), lambda b, h, i: (b, h)),
            pl.BlockSpec((TQ, MLA_V), lambda b, h, i: (b * nq + i, OFF_MLAZ // MLA_V + h)),
        ],
        out_specs=pl.BlockSpec((TQ, MLA_V), lambda b, h, i: (b * nq + i, h)),
        out_shape=jax.ShapeDtypeStruct((m, MLA_WIDTH), BF16),
        compiler_params=_params("parallel", "parallel", "arbitrary"),
        name="mla_attn",
    )(q, k, v, proj)


ML_PASSES = 3


def _shift_rows(x, prev8, s):
    rolled = pltpu.roll(x, s, 0)
    prev_rolled = pltpu.roll(prev8, s, 0)
    rid = lax.broadcasted_iota(jnp.int32, (SUBLANES, x.shape[1]), 0)
    top = jnp.where(rid < s, prev_rolled, rolled[0:SUBLANES])
    return jnp.concatenate([top, rolled[SUBLANES:]], axis=0)


def _mlstm_kernel(ib_ref, fb_ref, q_ref, k_ref, v_ref, o_ref, z_ref, gi_ref, gf_ref, wq_ref, wk_ref,
                  bq_ref, bk_ref, ng_ref, out_ref, ct_ref, n_ref, m_ref, qp_ref, kp_ref,
                  *, layer, tt, chunk):
    h = pl.program_id(1)
    t = pl.program_id(2)

    @pl.when(t == 0)
    def _():
        ct_ref[...] = jnp.zeros_like(ct_ref)
        n_ref[...] = jnp.zeros_like(n_ref)
        m_ref[...] = jnp.zeros_like(m_ref)
        qp_ref[...] = jnp.zeros_like(qp_ref)
        kp_ref[...] = jnp.zeros_like(kp_ref)

    def conv_silu(x_ref, prev_ref, w_ref, b_ref):
        x = x_ref[...]
        prev8 = prev_ref[...]
        w = w_ref[...]
        y = b_ref[...] + w[ML_CONV - 1:ML_CONV] * x
        for s in range(1, ML_CONV):
            y = y + w[ML_CONV - 1 - s:ML_CONV - s] * _shift_rows(x, prev8, s)
        prev_ref[...] = x[tt - SUBLANES:tt]
        return _silu(y)

    q_all = conv_silu(q_ref, qp_ref, wq_ref, bq_ref) * (ML_HEAD_DIM ** -0.5)
    k_all = conv_silu(k_ref, kp_ref, wk_ref, bk_ref)
    li_row_all = gi_ref[...] + ib_ref[layer, h]
    lf_row_all = _log_sigmoid(gf_ref[...] + fb_ref[layer, h])

    L = chunk
    rows = lax.broadcasted_iota(jnp.int32, (L, L), 0)
    cols = lax.broadcasted_iota(jnp.int32, (L, L), 1)
    tri = rows >= cols
    eye = rows == cols
    for c in range(tt // L):
        sl = slice(c * L, (c + 1) * L)
        qc, kc, vc = q_all[sl], k_all[sl], v_ref[sl, :]
        li_row = li_row_all[:, sl]
        lf_row = lf_row_all[:, sl]
        lf_b = jnp.broadcast_to(lf_row, (L, L))
        li_b = jnp.broadcast_to(li_row, (L, L))
        b_col = jnp.sum(jnp.where(tri, lf_b, 0.0), axis=-1, keepdims=True)
        lf_col = jnp.sum(jnp.where(eye, lf_b, 0.0), axis=-1, keepdims=True)
        li_col = jnp.sum(jnp.where(eye, li_b, 0.0), axis=-1, keepdims=True)
        b_row = jnp.sum(jnp.where(rows <= cols, jnp.broadcast_to(lf_col, (L, L)), 0.0),
                        axis=0, keepdims=True)
        m_prev = m_ref[...][:, 0:1]
        dmat = jnp.where(tri, b_col - b_row + li_row, NEG_BIG)
        m_inter = b_col + m_prev
        m_t = jnp.maximum(m_inter, jnp.max(dmat, axis=-1, keepdims=True))
        inter = jnp.exp(m_inter - m_t)
        smat = _mm_nt(qc, kc, ML_PASSES) * jnp.exp(dmat - m_t)
        ct = ct_ref[...]
        n_row = n_ref[...]
        num = inter * _mm(qc, ct, ML_PASSES) + _mm(smat, vc, ML_PASSES)
        den = (inter * jnp.sum(qc * n_row, axis=-1, keepdims=True)
               + jnp.sum(smat, axis=-1, keepdims=True))
        hh = num / jnp.maximum(jnp.abs(den), jnp.exp(-m_t))
        b_last = b_col[L - 1:L, :]
        g_col = b_last - b_col + li_col
        m_new = jnp.maximum(b_last + m_prev, jnp.max(g_col, axis=0, keepdims=True))
        decay = jnp.exp(b_last + m_prev - m_new)
        kw = kc * jnp.exp(g_col - m_new)
        ct_ref[...] = decay * ct + _mm(kw.T, vc, ML_PASSES)
        n_ref[...] = decay * n_row + jnp.sum(kw, axis=0, keepdims=True)
        m_ref[...] = jnp.broadcast_to(m_new, m_ref.shape)
        mu = jnp.mean(hh, axis=-1, keepdims=True)
        dd = hh - mu
        var = jnp.mean(dd * dd, axis=-1, keepdims=True)
        yn = dd * lax.rsqrt(var + NORM_EPS) * ng_ref[...]
        out_ref[sl, :] = (_silu(z_ref[sl, :]) * (_sigmoid(o_ref[sl, :]) * yn)).astype(BF16)


def _mlstm(proj, gates_t, i_bias, f_bias, conv_w, conv_b3, norm_g3, layer, batch, seq):
    m = proj.shape[0]
    tt = min(ML_TT, seq)
    nt = seq // tt
    hd = ML_HEAD_DIM

    def col(off):
        return lambda b, h, t: (b * nt + t, off // hd + h)

    smem = pl.BlockSpec(memory_space=pltpu.SMEM)
    return pl.pallas_call(
        functools.partial(_mlstm_kernel, layer=layer, tt=tt, chunk=min(ML_L, tt)),
        grid=(batch, ML_HEADS, nt),
        in_specs=[
            smem, smem,
            pl.BlockSpec((tt, hd), col(OFF_QK)),
            pl.BlockSpec((tt, hd), col(OFF_QK + ML_WIDTH)),
            pl.BlockSpec((tt, hd), col(OFF_MLV)),
            pl.BlockSpec((tt, hd), col(OFF_MLO)),
            pl.BlockSpec((tt, hd), col(OFF_MLZ)),
            pl.BlockSpec((None, None, 1, tt), lambda b, h, t: (b, h, 0, t)),
            pl.BlockSpec((None, None, 1, tt), lambda b, h, t: (b, ML_HEADS + h, 0, t)),
            pl.BlockSpec((None, ML_CONV, hd), lambda b, h, t: (layer, 0, h)),
            pl.BlockSpec((None, ML_CONV, hd), lambda b, h, t: (layer, 0, ML_HEADS + h)),
            pl.BlockSpec((None, 1, hd), lambda b, h, t: (layer, 0, h)),
            pl.BlockSpec((None, 1, hd), lambda b, h, t: (layer, 0, ML_HEADS + h)),
            pl.BlockSpec((None, 1, hd), lambda b, h, t: (layer, 0, h)),
        ],
        out_specs=pl.BlockSpec((tt, hd), lambda b, h, t: (b * nt + t, h)),
        out_shape=jax.ShapeDtypeStruct((m, ML_WIDTH), BF16),
        scratch_shapes=[
            pltpu.VMEM((hd, hd), F32),
            pltpu.VMEM((1, hd), F32),
            pltpu.VMEM((1, hd), F32),
            pltpu.VMEM((SUBLANES, hd), F32),
            pltpu.VMEM((SUBLANES, hd), F32),
        ],
        compiler_params=_params("parallel", "parallel", "arbitrary"),
        name="mlstm",
    )(i_bias, f_bias, proj, proj, proj, proj, proj, gates_t, gates_t, conv_w, conv_w,
      conv_b3, conv_b3, norm_g3)


RW_PRE_PASSES = 3


def _rwkv_pre_kernel(*refs, has_vres, tm):
    if has_vres:
        (wa_ref, r_ref, k_ref, v_ref, mu_wa, mu_r, mu_k, mu_v, w0_ref, w2_ref, a0_ref, a2_ref,
         kk_ref, ka_ref, e_ref, vf_ref, v0_ref, v1_ref, v2_ref,
         r_out, ld_out, kh_out, v_out, kn_out, b_out, c_wa, c_r, c_k, c_v) = refs
    else:
        (wa_ref, r_ref, k_ref, v_ref, mu_wa, mu_r, mu_k, mu_v, w0_ref, w2_ref, a0_ref, a2_ref,
         kk_ref, ka_ref, e_ref,
         r_out, ld_out, kh_out, v_out, kn_out, b_out, c_wa, c_r, c_k, c_v) = refs

    @pl.when(pl.program_id(1) == 0)
    def _():
        for c in (c_wa, c_r, c_k, c_v):
            c[...] = jnp.zeros_like(c)

    def mix(x_ref, c_ref, mu_ref):
        x = x_ref[...]
        rolled = pltpu.roll(x, 1, 0)
        rid = lax.broadcasted_iota(jnp.int32, (SUBLANES, x.shape[1]), 0)
        prev_last = jnp.broadcast_to(c_ref[SUBLANES - 1:SUBLANES, :], (SUBLANES, x.shape[1]))
        top = jnp.where(rid == 0, prev_last, rolled[0:SUBLANES])
        xprev = jnp.concatenate([top, rolled[SUBLANES:]], axis=0)
        c_ref[...] = x[tm - SUBLANES:tm]
        return x + mu_ref[...] * (xprev - x)

    xwa = mix(wa_ref, c_wa, mu_wa)
    r = mix(r_ref, c_r, mu_r)
    k = mix(k_ref, c_k, mu_k)
    v = mix(v_ref, c_v, mu_v)
    zw = w0_ref[...] + _mm(jnp.tanh(xwa), w2_ref[...], RW_PRE_PASSES)
    za = a0_ref[...] + _mm(xwa, a2_ref[...], RW_PRE_PASSES)
    log_w = _log_sigmoid(zw) - 0.5
    ld_out[...] = -jnp.exp(log_w)
    a = _sigmoid(za)
    if has_vres:
        gate = _sigmoid(v0_ref[...] + _mm(_mm(v, v1_ref[...], RW_PRE_PASSES), v2_ref[...],
                                          RW_PRE_PASSES))
        v = v + (vf_ref[...] - v) * gate
    kk = k * kk_ref[...]
    ss = _mm(kk * kk, e_ref[...], RW_PRE_PASSES)
    kn = kk / jnp.maximum(jnp.sqrt(ss), 1e-12)
    r_out[...] = r
    kh_out[...] = k * (1.0 + (a - 1.0) * ka_ref[...])
    v_out[...] = v
    kn_out[...] = kn
    b_out[...] = kn * a


def _rwkv_pre(proj, v_first, mu_wa, mu_r, mu_k, mu_v, w0, w2p, a0, a2p, k_k, k_a, e_head,
              v0, v1p, v2p, layer, batch, seq):
    m = proj.shape[0]
    tm = min(RW_TM, seq)
    nt = seq // tm
    has_vres = layer > 0
    w = RW_WIDTH

    def colblk(off, width):
        return pl.BlockSpec((tm, width), lambda b, t: (b * nt + t, off // width))

    def lay(shape):
        return pl.BlockSpec((None,) + shape, lambda b, t: (layer,) + (0,) * len(shape))

    def lay1(shape):
        return pl.BlockSpec((None,) + shape, lambda b, t: (layer - 1,) + (0,) * len(shape))

    in_specs = [colblk(OFF_WA, LANES), colblk(OFF_RWR, w), colblk(OFF_RWK, w), colblk(OFF_RWV, w),
                lay((1, LANES)), lay((1, w)), lay((1, w)), lay((1, w)),
                lay((1, w)), lay((LANES, w)), lay((1, w)), lay((LANES, w)),
                lay((1, w)), lay((1, w)),
                pl.BlockSpec((w, w), lambda b, t: (0, 0))]
    args = [proj, proj, proj, proj, mu_wa, mu_r, mu_k, mu_v, w0, w2p, a0, a2p, k_k, k_a, e_head]
    if has_vres:
        in_specs += [pl.BlockSpec((tm, w), lambda b, t: (b * nt + t, 0)),
                     lay1((1, w)), lay1((w, LANES)), lay1((LANES, w))]
        args += [v_first, v0, v1p, v2p]
    row = pl.BlockSpec((tm, w), lambda b, t: (b * nt + t, 0))
    return pl.pallas_call(
        functools.partial(_rwkv_pre_kernel, has_vres=has_vres, tm=tm),
        grid=(batch, nt),
        in_specs=in_specs,
        out_specs=[row] * 6,
        out_shape=[jax.ShapeDtypeStruct((m, w), F32)] * 6,
        scratch_shapes=[pltpu.VMEM((SUBLANES, LANES), F32), pltpu.VMEM((SUBLANES, w), F32),
                        pltpu.VMEM((SUBLANES, w), F32), pltpu.VMEM((SUBLANES, w), F32)],
        compiler_params=_params("parallel", "arbitrary"),
        name="rwkv_pre",
    )(*args)


RW_PASSES = 3


def _rwkv_scan_kernel(r_ref, ld_ref, kh_ref, v_ref, kn_ref, b_ref, z_ref, rk_ref, g_ref, bias_ref,
                      e_ref, out_ref, m_ref, y_ref, *, tt, chunk):
    @pl.when(pl.program_id(2) == 0)
    def _():
        m_ref[...] = jnp.zeros_like(m_ref)

    L = chunk
    P = RW_PASSES
    rows = lax.broadcasted_iota(jnp.int32, (L, L), 0)
    cols = lax.broadcasted_iota(jnp.int32, (L, L), 1)
    tri_incl = rows >= cols
    tri_strict = rows > cols
    tri_f = jnp.where(tri_incl, 1.0, 0.0)
    lane = lax.broadcasted_iota(jnp.int32, (1, LANES), 1)
    head_masks = (jnp.where(lane < RW_HEAD_DIM, 1.0, 0.0), jnp.where(lane >= RW_HEAD_DIM, 1.0, 0.0))
    r2 = lax.broadcasted_iota(jnp.int32, (LANES, LANES), 0)
    c2 = lax.broadcasted_iota(jnp.int32, (LANES, LANES), 1)
    same_head = (r2 < RW_HEAD_DIM) == (c2 < RW_HEAD_DIM)
    eye2 = r2 == c2
    zeros_l = jnp.zeros((L, LANES), F32)
    n_double = max(1, (L - 1).bit_length())

    def chunk_body(c, carry):
        off = pl.multiple_of(c * L, L)
        sl = pl.ds(off, L)
        ld = ld_ref[sl, :]
        r, kh, v, kn, b = r_ref[sl, :], kh_ref[sl, :], v_ref[sl, :], kn_ref[sl, :], b_ref[sl, :]
        logp = _mm(tri_f, ld, P)
        cmid = logp[L // 2 - 1:L // 2, :]
        e = logp - cmid
        e_last = e[L - 1:L, :]
        at = -kn * jnp.exp(e - ld)
        rt = r * jnp.exp(e)
        inv = jnp.exp(-e)
        bt = b * inv
        kt = kh * inv
        tail = jnp.exp(e_last - e)
        bk_hat = jnp.concatenate([b * tail, kh * tail], axis=0)
        p_mid = jnp.exp(cmid)
        p_last = jnp.exp(e_last + cmid)

        w_sum, u0_sum, q_sum, y0_sum = zeros_l, zeros_l, zeros_l, zeros_l
        for mk in head_masks:
            lhs = jnp.concatenate([at * mk, rt * mk], axis=0)
            aa_b = _mm_nt(lhs, bt, P)
            aa_k = _mm_nt(lhs, kt, P)
            a_ab = jnp.where(tri_strict, aa_b[0:L], 0.0)
            a_ak = jnp.where(tri_strict, aa_k[0:L], 0.0)
            a_rb = jnp.where(tri_incl, aa_b[L:], 0.0)
            a_rk = jnp.where(tri_incl, aa_k[L:], 0.0)
            vm = v * mk
            x = jnp.concatenate([at * (mk * p_mid), _mm(a_ak, vm, P)], axis=1)
            nmat = a_ab
            for it in range(n_double):
                x = x + _mm(nmat, x, P)
                if it + 1 < n_double:
                    nmat = _mm(nmat, nmat, P)
            qy = _mm(a_rb, x, P)
            w_sum = w_sum + x[:, :LANES]
            u0_sum = u0_sum + x[:, LANES:]
            q_sum = q_sum + rt * (mk * p_mid) + qy[:, :LANES]
            y0_sum = y0_sum + qy[:, LANES:] + _mm(a_rk, vm, P)

        rhs = jnp.concatenate([jnp.concatenate([w_sum, u0_sum], axis=1),
                               jnp.concatenate([zeros_l, v], axis=1)], axis=0)
        cg = _mm(bk_hat.T, rhs, P)
        a_c = (jnp.where(eye2, jnp.broadcast_to(p_last, (LANES, LANES)), 0.0)
               + jnp.where(same_head, cg[:, :LANES], 0.0))
        g_c = jnp.where(same_head, cg[:, LANES:], 0.0)
        m0 = m_ref[...]
        y_ref[sl, :] = _mm(q_sum, m0, P) + y0_sum
        m_ref[...] = _mm(a_c, m0, P) + g_c
        return carry

    lax.fori_loop(0, tt // L, chunk_body, 0)

    y = y_ref[...]
    e_mat = e_ref[...]
    inv_n = 1.0 / RW_HEAD_DIM
    mu = _mm(y, e_mat, 3) * inv_n
    d = y - mu
    var = _mm(d * d, e_mat, 3) * inv_n
    yn = d * lax.rsqrt(var + RW_GN_EPS) * g_ref[...] + bias_ref[...]
    r_all, kh_all, v_all = r_ref[...], kh_ref[...], v_ref[...]
    bonus = _mm(r_all * kh_all * rk_ref[...], e_mat, 3)
    out = yn + bonus * v_all
    out_ref[...] = (_silu(z_ref[...]) * out).astype(BF16)


def _rwkv_scan(r, ld, kh, v, kn, b, proj, r_k3, ln_g3, ln_b3, e_pair, layer, batch, seq):
    m = r.shape[0]
    tt = min(RW_TT, seq)
    nt = seq // tt
    npair = RW_WIDTH // LANES
    blk = pl.BlockSpec((tt, LANES), lambda bb, p, t: (bb * nt + t, p))
    par = pl.BlockSpec((None, 1, LANES), lambda bb, p, t: (layer, 0, p))
    return pl.pallas_call(
        functools.partial(_rwkv_scan_kernel, tt=tt, chunk=min(RW_L, tt)),
        grid=(batch, npair, nt),
        in_specs=[blk] * 6 + [
            pl.BlockSpec((tt, LANES), lambda bb, p, t: (bb * nt + t, OFF_RWZ // LANES + p)),
            par, par, par,
            pl.BlockSpec((LANES, LANES), lambda bb, p, t: (0, 0)),
        ],
        out_specs=blk,
        out_shape=jax.ShapeDtypeStruct((m, RW_WIDTH), BF16),
        scratch_shapes=[pltpu.VMEM((LANES, LANES), F32), pltpu.VMEM((tt, LANES), F32)],
        compiler_params=_params("parallel", "parallel", "arbitrary"),
        name="rwkv_scan",
    )(r, ld, kh, v, kn, b, proj, r_k3, ln_g3, ln_b3, e_pair)


def _permute_w_in(w_in):
    def s(a, n):
        return w_in[:, :, a:a + n]

    pad = jnp.zeros(w_in.shape[:2] + (LANES - MLA_ROPE - 2 * ML_HEADS,), w_in.dtype)
    parts = [s(0, 1024), s(1024, 512), s(1544, 512), s(2056, 512), s(2568, 512), s(3080, 256),
             s(3336, 64), s(1536, 8), pad, s(5960, 128), s(4424, 512), s(4936, 512), s(5448, 512),
             s(6088, 512), s(3400, 1024)]
    return jnp.concatenate(parts, axis=-1).astype(BF16)


def _permute_w_uq(w_uq):
    d = w_uq.shape[0]
    w = w_uq.reshape(d, MLA_Q_RANK, MLA_HEADS, MLA_NOPE + MLA_ROPE)
    w = jnp.pad(w, ((0, 0), (0, 0), (0, 0), (0, MLA_QK_PAD - MLA_NOPE - MLA_ROPE)))
    return w.reshape(d, MLA_Q_RANK, MLA_HEADS * MLA_QK_PAD).astype(BF16)


def _split_w_ukv(w_ukv):
    d = w_ukv.shape[0]
    w = w_ukv.reshape(d, MLA_KV_RANK, MLA_HEADS, MLA_NOPE + MLA_V)
    wk = w[..., :MLA_NOPE].reshape(d, MLA_KV_RANK, MLA_HEADS * MLA_NOPE)
    wv = w[..., MLA_NOPE:].reshape(d, MLA_KV_RANK, MLA_HEADS * MLA_V)
    return wk.astype(BF16), wv.astype(BF16)


def _rope_tables(positions):
    inv_freq = jnp.power(ROPE_BASE, -jnp.arange(0, MLA_ROPE, 2, dtype=F32) / MLA_ROPE)
    ang = positions.astype(F32).reshape(-1, 1) * inv_freq
    cos, sin = jnp.cos(ang), jnp.sin(ang)
    z32 = jnp.zeros_like(cos)
    z64 = jnp.zeros((ang.shape[0], LANES - MLA_ROPE), F32)
    cosf = jnp.concatenate([cos, cos, z64], axis=-1)
    s1 = jnp.concatenate([-sin, z32, z64], axis=-1)
    s2 = jnp.concatenate([z32, sin, z64], axis=-1)
    return cosf, s1, s2


def kernel(x, positions, norm_g, w_in, ml_conv_w, ml_conv_b, ml_i_bias, ml_f_bias, ml_norm_g,
           mla_q_norm_g, mla_w_uq, mla_kv_norm_g, mla_w_ukv, rw_mu, rw_w0, rw_w2, rw_a0, rw_a2,
           rw_v0, rw_v1, rw_v2, rw_k_k, rw_k_a, rw_r_k, rw_ln_g, rw_ln_b, w_out, final_norm_g):
    batch, seq, _ = x.shape
    depth = w_in.shape[0]
    m = batch * seq
    x2 = x.reshape(m, D_MODEL)

    w_in_p = _permute_w_in(w_in)
    w_uq_p = _permute_w_uq(mla_w_uq)
    w_uk_p, w_uv_p = _split_w_ukv(mla_w_ukv)
    w_out_b = w_out.astype(BF16)
    cosf, s1, s2 = _rope_tables(positions)

    def row3(a):
        return a.reshape(a.shape[0], 1, a.shape[1])

    norm_g3 = row3(norm_g)
    conv_b3 = row3(ml_conv_b)
    ml_norm_g3 = row3(ml_norm_g)
    qg3, kvg3 = row3(mla_q_norm_g), row3(mla_kv_norm_g)
    w = RW_WIDTH
    mu_r, mu_k, mu_v = row3(rw_mu[:, 0:w]), row3(rw_mu[:, w:2 * w]), row3(rw_mu[:, 2 * w:3 * w])
    mu_wa = row3(rw_mu[:, 3 * w:])
    w0, a0, k_k, k_a = row3(rw_w0), row3(rw_a0), row3(rw_k_k), row3(rw_k_a)
    w2p = jnp.pad(rw_w2, ((0, 0), (0, LANES - RW_DECAY_RANK), (0, 0)))
    a2p = jnp.pad(rw_a2, ((0, 0), (LANES - RW_AAA_RANK, 0), (0, 0)))
    v0 = row3(rw_v0)
    v1p = jnp.pad(rw_v1, ((0, 0), (0, 0), (0, LANES - RW_MV_RANK)))
    v2p = jnp.pad(rw_v2, ((0, 0), (0, LANES - RW_MV_RANK), (0, 0)))
    r_k3 = rw_r_k.reshape(depth, 1, w)
    ln_g3, ln_b3 = row3(rw_ln_g), row3(rw_ln_b)
    hid = jnp.arange(w) // RW_HEAD_DIM
    e_head = (hid[:, None] == hid[None, :]).astype(F32)
    e_pair = e_head[:LANES, :LANES]
    final_g2 = final_norm_g.reshape(1, D_MODEL)

    v_first = None
    for layer in range(depth):
        proj = _inproj(x2, norm_g3, w_in_p, layer)
        gates_t = proj[:, OFF_KRG + MLA_ROPE:OFF_KRG + MLA_ROPE + 2 * ML_HEADS]
        gates_t = gates_t.reshape(batch, seq, 2 * ML_HEADS).transpose(0, 2, 1)
        gates_t = gates_t.reshape(batch, 2 * ML_HEADS, 1, seq)
        y_ml = _mlstm(proj, gates_t, ml_i_bias, ml_f_bias, ml_conv_w, conv_b3, ml_norm_g3,
                      layer, batch, seq)
        q, k, v = _mla_proj(proj, cosf, s1, s2, qg3, w_uq_p, kvg3, w_uk_p, w_uv_p, layer)
        y_mla = _attention(q, k, v, proj, batch, seq)
        r_s, ld, kh, v_rw, kn, b_rw = _rwkv_pre(proj, v_first, mu_wa, mu_r, mu_k, mu_v, w0, w2p,
                                                 a0, a2p, k_k, k_a, e_head, v0, v1p, v2p,
                                                 layer, batch, seq)
        if layer == 0:
            v_first = v_rw
        y_rw = _rwkv_scan(r_s, ld, kh, v_rw, kn, b_rw, proj, r_k3, ln_g3, ln_b3, e_pair,
                          layer, batch, seq)
        x2 = _outproj(x2, y_ml, y_mla, y_rw, w_out_b, final_g2, layer, layer == depth - 1)
    return x2.reshape(batch, seq, D_MODEL)
```

```python
import functools

import jax
import jax.numpy as jnp
from jax import lax
from jax.experimental import pallas as pl
from jax.experimental.pallas import tpu as pltpu

F32 = jnp.float32
BF16 = jnp.bfloat16

D_MODEL = 2048
NORM_EPS = 1e-6
ML_HEADS = 4
ML_HEAD_DIM = 128
ML_WIDTH = 512
ML_CONV = 4
MLA_HEADS = 8
MLA_NOPE = 128
MLA_ROPE = 64
MLA_V = 128
MLA_WIDTH = 1024
MLA_Q_RANK = 512
MLA_KV_RANK = 256
ROPE_BASE = 10000.0
RW_HEAD_DIM = 64
RW_WIDTH = 512
RW_HEADS = 8
RW_DECAY_RANK = 64
RW_AAA_RANK = 64
RW_MV_RANK = 32
RW_GN_EPS = 64e-5

LANES = 128
SUBLANES = 8
MLA_QK_PAD = 256
VMEM_LIMIT = 48 * 1024 * 1024
NEG_BIG = -1e30

OFF_QK = 0
OFF_MLV = 1024
OFF_MLO = 1536
OFF_MLZ = 2048
OFF_CQ = 2560
OFF_CKV = 3072
OFF_KRG = 3328
OFF_WA = 3456
OFF_RWR = 3584
OFF_RWK = 4096
OFF_RWV = 4608
OFF_RWZ = 5120
OFF_MLAZ = 5632
D_IN_PAD = 6656

TM_PROJ = 512
TN_PROJ = 1664
TM_OUT = 512
TM_MLA = 512
TQ = 512
ML_TT = 1024
ML_L = 256
RW_TM = 512
RW_TT = 512
RW_L = 64


def _split_bf16(x):
    hi = x.astype(BF16)
    lo = (x - hi.astype(F32)).astype(BF16)
    return hi, lo


def _mm(a, b, passes=1):
    if passes == 1:
        return jnp.dot(a.astype(BF16), b.astype(BF16), preferred_element_type=F32)
    ah, al = _split_bf16(a)
    bh, bl = _split_bf16(b)
    return (jnp.dot(ah, bh, preferred_element_type=F32)
            + jnp.dot(ah, bl, preferred_element_type=F32)
            + jnp.dot(al, bh, preferred_element_type=F32))


_NT = (((1,), (1,)), ((), ()))


def _mm_nt(a, b, passes=1):
    if passes == 1:
        return lax.dot_general(a.astype(BF16), b.astype(BF16), _NT, preferred_element_type=F32)
    ah, al = _split_bf16(a)
    bh, bl = _split_bf16(b)
    return (lax.dot_general(ah, bh, _NT, preferred_element_type=F32)
            + lax.dot_general(ah, bl, _NT, preferred_element_type=F32)
            + lax.dot_general(al, bh, _NT, preferred_element_type=F32))


def _sigmoid(x):
    return 1.0 / (1.0 + jnp.exp(-x))


def _silu(x):
    return x * _sigmoid(x)


def _log_sigmoid(x):
    return jnp.minimum(x, 0.0) - jnp.log(1.0 + jnp.exp(-jnp.abs(x)))


def _params(*sem):
    return pltpu.CompilerParams(dimension_semantics=sem, vmem_limit_bytes=VMEM_LIMIT)


def _inproj_kernel(x_ref, g_ref, w_ref, o_ref, h_ref):
    @pl.when(pl.program_id(1) == 0)
    def _():
        x = x_ref[...]
        ms = jnp.mean(x * x, axis=-1, keepdims=True)
        h_ref[...] = (x * lax.rsqrt(ms + NORM_EPS) * g_ref[...]).astype(BF16)

    o_ref[...] = jnp.dot(h_ref[...], w_ref[...], preferred_element_type=F32)


def _inproj(x2, norm_g3, w_in_p, layer):
    m = x2.shape[0]
    return pl.pallas_call(
        _inproj_kernel,
        grid=(m // TM_PROJ, D_IN_PAD // TN_PROJ),
        in_specs=[
            pl.BlockSpec((TM_PROJ, D_MODEL), lambda i, j: (i, 0)),
            pl.BlockSpec((None, 1, D_MODEL), lambda i, j: (layer, 0, 0)),
            pl.BlockSpec((None, D_MODEL, TN_PROJ), lambda i, j: (layer, 0, j)),
        ],
        out_specs=pl.BlockSpec((TM_PROJ, TN_PROJ), lambda i, j: (i, j)),
        out_shape=jax.ShapeDtypeStruct((m, D_IN_PAD), F32),
        scratch_shapes=[pltpu.VMEM((TM_PROJ, D_MODEL), BF16)],
        compiler_params=_params("parallel", "arbitrary"),
        name="inproj",
    )(x2, norm_g3, w_in_p)


def _outproj_kernel(x_ref, yml_ref, ymla_ref, yrw_ref, w_ref, g_ref, o_ref, *, final):
    acc = x_ref[...]
    acc = acc + jnp.dot(yml_ref[...], w_ref[0:ML_WIDTH, :], preferred_element_type=F32)
    acc = acc + jnp.dot(ymla_ref[...], w_ref[ML_WIDTH:ML_WIDTH + MLA_WIDTH, :],
                        preferred_element_type=F32)
    acc = acc + jnp.dot(yrw_ref[...], w_ref[ML_WIDTH + MLA_WIDTH:, :], preferred_element_type=F32)
    if final:
        ms = jnp.mean(acc * acc, axis=-1, keepdims=True)
        acc = acc * lax.rsqrt(ms + NORM_EPS) * g_ref[...]
    o_ref[...] = acc


def _outproj(x2, y_ml, y_mla, y_rw, w_out_b, final_g2, layer, final):
    m = x2.shape[0]
    return pl.pallas_call(
        functools.partial(_outproj_kernel, final=final),
        grid=(m // TM_OUT,),
        in_specs=[
            pl.BlockSpec((TM_OUT, D_MODEL), lambda i: (i, 0)),
            pl.BlockSpec((TM_OUT, ML_WIDTH), lambda i: (i, 0)),
            pl.BlockSpec((TM_OUT, MLA_WIDTH), lambda i: (i, 0)),
            pl.BlockSpec((TM_OUT, RW_WIDTH), lambda i: (i, 0)),
            pl.BlockSpec((None, D_MODEL, D_MODEL), lambda i: (layer, 0, 0)),
            pl.BlockSpec((1, D_MODEL), lambda i: (0, 0)),
        ],
        out_specs=pl.BlockSpec((TM_OUT, D_MODEL), lambda i: (i, 0)),
        out_shape=jax.ShapeDtypeStruct((m, D_MODEL), F32),
        compiler_params=_params("parallel"),
        name="outproj",
    )(x2, y_ml, y_mla, y_rw, w_out_b, final_g2)


def _mla_proj_kernel(cq_ref, ckv_ref, krg_ref, cos_ref, s1_ref, s2_ref, qg_ref, wq_ref, kvg_ref,
                     wk_ref, wv_ref, q_out, k_out, v_out):
    cq = cq_ref[...]
    cqn = (cq * lax.rsqrt(jnp.mean(cq * cq, axis=-1, keepdims=True) + NORM_EPS)
           * qg_ref[...]).astype(BF16)
    q = jnp.dot(cqn, wq_ref[...], preferred_element_type=F32)
    ckv = ckv_ref[...]
    ckvn = (ckv * lax.rsqrt(jnp.mean(ckv * ckv, axis=-1, keepdims=True) + NORM_EPS)
            * kvg_ref[...]).astype(BF16)
    kn = jnp.dot(ckvn, wk_ref[...], preferred_element_type=F32)
    v = jnp.dot(ckvn, wv_ref[...], preferred_element_type=F32)
    cosf, s1, s2 = cos_ref[...], s1_ref[...], s2_ref[...]

    def rope(xb):
        return (xb * cosf + pltpu.roll(xb, LANES - MLA_ROPE // 2, 1) * s1
                + pltpu.roll(xb, MLA_ROPE // 2, 1) * s2)

    scale = (MLA_NOPE + MLA_ROPE) ** -0.5
    kr = rope(krg_ref[...]).astype(BF16)
    for h in range(MLA_HEADS):
        c0 = h * MLA_QK_PAD
        q_out[:, c0:c0 + MLA_NOPE] = (q[:, c0:c0 + MLA_NOPE] * scale).astype(BF16)
        q_out[:, c0 + MLA_NOPE:c0 + MLA_QK_PAD] = (
            rope(q[:, c0 + MLA_NOPE:c0 + MLA_QK_PAD]) * scale).astype(BF16)
        k_out[:, c0:c0 + MLA_NOPE] = kn[:, h * MLA_NOPE:(h + 1) * MLA_NOPE].astype(BF16)
        k_out[:, c0 + MLA_NOPE:c0 + MLA_QK_PAD] = kr
    v_out[...] = v.astype(BF16)


def _mla_proj(proj, cosf, s1, s2, qg3, wq_p, kvg3, wk_p, wv_p, layer):
    m = proj.shape[0]
    tm = TM_MLA
    row = lambda i: (i, 0)
    return pl.pallas_call(
        _mla_proj_kernel,
        grid=(m // tm,),
        in_specs=[
            pl.BlockSpec((tm, MLA_Q_RANK), lambda i: (i, OFF_CQ // MLA_Q_RANK)),
            pl.BlockSpec((tm, MLA_KV_RANK), lambda i: (i, OFF_CKV // MLA_KV_RANK)),
            pl.BlockSpec((tm, LANES), lambda i: (i, OFF_KRG // LANES)),
            pl.BlockSpec((tm, LANES), row),
            pl.BlockSpec((tm, LANES), row),
            pl.BlockSpec((tm, LANES), row),
            pl.BlockSpec((None, 1, MLA_Q_RANK), lambda i: (layer, 0, 0)),
            pl.BlockSpec((None, MLA_Q_RANK, MLA_HEADS * MLA_QK_PAD), lambda i: (layer, 0, 0)),
            pl.BlockSpec((None, 1, MLA_KV_RANK), lambda i: (layer, 0, 0)),
            pl.BlockSpec((None, MLA_KV_RANK, MLA_HEADS * MLA_NOPE), lambda i: (layer, 0, 0)),
            pl.BlockSpec((None, MLA_KV_RANK, MLA_HEADS * MLA_V), lambda i: (layer, 0, 0)),
        ],
        out_specs=[
            pl.BlockSpec((tm, MLA_HEADS * MLA_QK_PAD), row),
            pl.BlockSpec((tm, MLA_HEADS * MLA_QK_PAD), row),
            pl.BlockSpec((tm, MLA_HEADS * MLA_V), row),
        ],
        out_shape=[
            jax.ShapeDtypeStruct((m, MLA_HEADS * MLA_QK_PAD), BF16),
            jax.ShapeDtypeStruct((m, MLA_HEADS * MLA_QK_PAD), BF16),
            jax.ShapeDtypeStruct((m, MLA_HEADS * MLA_V), BF16),
        ],
        compiler_params=_params("parallel"),
        name="mla_proj",
    )(proj, proj, proj, cosf, s1, s2, qg3, wq_p, kvg3, wk_p, wv_p)


def _attn_kernel(q_ref, k_ref, v_ref, z_ref, o_ref, *, tq):
    i = pl.program_id(2)
    q = q_ref[...]

    def step(j, carry, masked):
        m, l, acc = carry
        off = pl.multiple_of(j * tq, tq)
        kb = k_ref[pl.ds(off, tq), :]
        vb = v_ref[pl.ds(off, tq), :]
        s = lax.dot_general(q, kb, _NT, preferred_element_type=F32)
        if masked:
            rows = lax.broadcasted_iota(jnp.int32, (tq, tq), 0)
            cols = lax.broadcasted_iota(jnp.int32, (tq, tq), 1)
            s = jnp.where(rows >= cols, s, NEG_BIG)
        m_new = jnp.maximum(m, jnp.max(s, axis=-1, keepdims=True))
        alpha = jnp.exp(m - m_new)
        p = jnp.exp(s - m_new)
        l = alpha * l + jnp.sum(p, axis=-1, keepdims=True)
        acc = alpha * acc + jnp.dot(p.astype(BF16), vb, preferred_element_type=F32)
        return m_new, l, acc

    init = (jnp.full((tq, 1), NEG_BIG, F32), jnp.zeros((tq, 1), F32), jnp.zeros((tq, MLA_V), F32))
    carry = lax.fori_loop(0, i, lambda j, c: step(j, c, False), init)
    _, l, acc = step(i, carry, True)
    y = acc / l
    o_ref[...] = (_silu(z_ref[...]) * y).astype(BF16)


def _attention(q, k, v, proj, batch, seq):
    m = q.shape[0]
    nq = seq // TQ
    return pl.pallas_call(
        functools.partial(_attn_kernel, tq=TQ),
        grid=(batch, MLA_HEADS, nq),
        in_specs=[
            pl.BlockSpec((TQ, MLA_QK_PAD), lambda b, h, i: (b * nq + i, h)),
            pl.BlockSpec((seq, MLA_QK_PAD), lambda b, h, i: (b, h)),
            pl.BlockSpec((seq, MLA_V), lambda b, h, i: (b, h)),
            pl.BlockSpec((TQ, MLA_V), lambda b, h, i: (b * nq + i, OFF_MLAZ // MLA_V + h)),
        ],
        out_specs=pl.BlockSpec((TQ, MLA_V), lambda b, h, i: (b * nq + i, h)),
        out_shape=jax.ShapeDtypeStruct((m, MLA_WIDTH), BF16),
        compiler_params=_params("parallel", "parallel", "arbitrary"),
        name="mla_attn",
    )(q, k, v, proj)


ML_PASSES = 1


def _shift_rows(x, prev8, s):
    rolled = pltpu.roll(x, s, 0)
    prev_rolled = pltpu.roll(prev8, s, 0)
    rid = lax.broadcasted_iota(jnp.int32, (SUBLANES, x.shape[1]), 0)
    top = jnp.where(rid < s, prev_rolled, rolled[0:SUBLANES])
    return jnp.concatenate([top, rolled[SUBLANES:]], axis=0)


def _mlstm_kernel(ib_ref, fb_ref, q_ref, k_ref, v_ref, o_ref, z_ref, gi_ref, gf_ref, wq_ref, wk_ref,
                  bq_ref, bk_ref, ng_ref, out_ref, ct_ref, n_ref, m_ref, qp_ref, kp_ref,
                  *, layer, tt, chunk):
    h = pl.program_id(1)
    t = pl.program_id(2)

    @pl.when(t == 0)
    def _():
        ct_ref[...] = jnp.zeros_like(ct_ref)
        n_ref[...] = jnp.zeros_like(n_ref)
        m_ref[...] = jnp.zeros_like(m_ref)
        qp_ref[...] = jnp.zeros_like(qp_ref)
        kp_ref[...] = jnp.zeros_like(kp_ref)

    def conv_silu(x_ref, prev_ref, w_ref, b_ref):
        x = x_ref[...]
        prev8 = prev_ref[...]
        w = w_ref[...]
        y = b_ref[...] + w[ML_CONV - 1:ML_CONV] * x
        for s in range(1, ML_CONV):
            y = y + w[ML_CONV - 1 - s:ML_CONV - s] * _shift_rows(x, prev8, s)
        prev_ref[...] = x[tt - SUBLANES:tt]
        return _silu(y)

    q_all = conv_silu(q_ref, qp_ref, wq_ref, bq_ref) * (ML_HEAD_DIM ** -0.5)
    k_all = conv_silu(k_ref, kp_ref, wk_ref, bk_ref)
    li_row_all = gi_ref[...] + ib_ref[layer, h]
    lf_row_all = _log_sigmoid(gf_ref[...] + fb_ref[layer, h])

    L = chunk
    rows = lax.broadcasted_iota(jnp.int32, (L, L), 0)
    cols = lax.broadcasted_iota(jnp.int32, (L, L), 1)
    tri = rows >= cols
    eye = rows == cols
    for c in range(tt // L):
        sl = slice(c * L, (c + 1) * L)
        qc, kc, vc = q_all[sl], k_all[sl], v_ref[sl, :]
        li_row = li_row_all[:, sl]
        lf_row = lf_row_all[:, sl]
        lf_b = jnp.broadcast_to(lf_row, (L, L))
        li_b = jnp.broadcast_to(li_row, (L, L))
        b_col = jnp.sum(jnp.where(tri, lf_b, 0.0), axis=-1, keepdims=True)
        lf_col = jnp.sum(jnp.where(eye, lf_b, 0.0), axis=-1, keepdims=True)
        li_col = jnp.sum(jnp.where(eye, li_b, 0.0), axis=-1, keepdims=True)
        b_row = jnp.sum(jnp.where(rows <= cols, jnp.broadcast_to(lf_col, (L, L)), 0.0),
                        axis=0, keepdims=True)
        m_prev = m_ref[...][:, 0:1]
        dmat = jnp.where(tri, b_col - b_row + li_row, NEG_BIG)
        m_inter = b_col + m_prev
        m_t = jnp.maximum(m_inter, jnp.max(dmat, axis=-1, keepdims=True))
        inter = jnp.exp(m_inter - m_t)
        smat = _mm_nt(qc, kc, ML_PASSES) * jnp.exp(dmat - m_t)
        ct = ct_ref[...]
        n_row = n_ref[...]
        num = inter * _mm(qc, ct, ML_PASSES) + _mm(smat, vc, ML_PASSES)
        den = (inter * jnp.sum(qc * n_row, axis=-1, keepdims=True)
               + jnp.sum(smat, axis=-1, keepdims=True))
        hh = num / jnp.maximum(jnp.abs(den), jnp.exp(-m_t))
        b_last = b_col[L - 1:L, :]
        g_col = b_last - b_col + li_col
        m_new = jnp.maximum(b_last + m_prev, jnp.max(g_col, axis=0, keepdims=True))
        decay = jnp.exp(b_last + m_prev - m_new)
        kw = kc * jnp.exp(g_col - m_new)
        ct_ref[...] = decay * ct + _mm(kw.T, vc, ML_PASSES)
        n_ref[...] = decay * n_row + jnp.sum(kw, axis=0, keepdims=True)
        m_ref[...] = jnp.broadcast_to(m_new, m_ref.shape)
        mu = jnp.mean(hh, axis=-1, keepdims=True)
        dd = hh - mu
        var = jnp.mean(dd * dd, axis=-1, keepdims=True)
        yn = dd * lax.rsqrt(var + NORM_EPS) * ng_ref[...]
        out_ref[sl, :] = (_silu(z_ref[sl, :]) * (_sigmoid(o_ref[sl, :]) * yn)).astype(BF16)


def _mlstm(proj, gates_t, i_bias, f_bias, conv_w, conv_b3, norm_g3, layer, batch, seq):
    m = proj.shape[0]
    tt = min(ML_TT, seq)
    nt = seq // tt
    hd = ML_HEAD_DIM

    def col(off):
        return lambda b, h, t: (b * nt + t, off // hd + h)

    smem = pl.BlockSpec(memory_space=pltpu.SMEM)
    return pl.pallas_call(
        functools.partial(_mlstm_kernel, layer=layer, tt=tt, chunk=min(ML_L, tt)),
        grid=(batch, ML_HEADS, nt),
        in_specs=[
            smem, smem,
            pl.BlockSpec((tt, hd), col(OFF_QK)),
            pl.BlockSpec((tt, hd), col(OFF_QK + ML_WIDTH)),
            pl.BlockSpec((tt, hd), col(OFF_MLV)),
            pl.BlockSpec((tt, hd), col(OFF_MLO)),
            pl.BlockSpec((tt, hd), col(OFF_MLZ)),
            pl.BlockSpec((None, None, 1, tt), lambda b, h, t: (b, h, 0, t)),
            pl.BlockSpec((None, None, 1, tt), lambda b, h, t: (b, ML_HEADS + h, 0, t)),
            pl.BlockSpec((None, ML_CONV, hd), lambda b, h, t: (layer, 0, h)),
            pl.BlockSpec((None, ML_CONV, hd), lambda b, h, t: (layer, 0, ML_HEADS + h)),
            pl.BlockSpec((None, 1, hd), lambda b, h, t: (layer, 0, h)),
            pl.BlockSpec((None, 1, hd), lambda b, h, t: (layer, 0, ML_HEADS + h)),
            pl.BlockSpec((None, 1, hd), lambda b, h, t: (layer, 0, h)),
        ],
        out_specs=pl.BlockSpec((tt, hd), lambda b, h, t: (b * nt + t, h)),
        out_shape=jax.ShapeDtypeStruct((m, ML_WIDTH), BF16),
        scratch_shapes=[
            pltpu.VMEM((hd, hd), F32),
            pltpu.VMEM((1, hd), F32),
            pltpu.VMEM((1, hd), F32),
            pltpu.VMEM((SUBLANES, hd), F32),
            pltpu.VMEM((SUBLANES, hd), F32),
        ],
        compiler_params=_params("parallel", "parallel", "arbitrary"),
        name="mlstm",
    )(i_bias, f_bias, proj, proj, proj, proj, proj, gates_t, gates_t, conv_w, conv_w,
      conv_b3, conv_b3, norm_g3)


RW_PRE_PASSES = 3


def _rwkv_pre_kernel(*refs, has_vres, tm):
    if has_vres:
        (wa_ref, r_ref, k_ref, v_ref, mu_wa, mu_r, mu_k, mu_v, w0_ref, w2_ref, a0_ref, a2_ref,
         kk_ref, ka_ref, e_ref, vf_ref, v0_ref, v1_ref, v2_ref,
         r_out, ld_out, kh_out, v_out, kn_out, b_out, c_wa, c_r, c_k, c_v) = refs
    else:
        (wa_ref, r_ref, k_ref, v_ref, mu_wa, mu_r, mu_k, mu_v, w0_ref, w2_ref, a0_ref, a2_ref,
         kk_ref, ka_ref, e_ref,
         r_out, ld_out, kh_out, v_out, kn_out, b_out, c_wa, c_r, c_k, c_v) = refs

    @pl.when(pl.program_id(1) == 0)
    def _():
        for c in (c_wa, c_r, c_k, c_v):
            c[...] = jnp.zeros_like(c)

    def mix(x_ref, c_ref, mu_ref):
        x = x_ref[...]
        rolled = pltpu.roll(x, 1, 0)
        rid = lax.broadcasted_iota(jnp.int32, (SUBLANES, x.shape[1]), 0)
        prev_last = jnp.broadcast_to(c_ref[SUBLANES - 1:SUBLANES, :], (SUBLANES, x.shape[1]))
        top = jnp.where(rid == 0, prev_last, rolled[0:SUBLANES])
        xprev = jnp.concatenate([top, rolled[SUBLANES:]], axis=0)
        c_ref[...] = x[tm - SUBLANES:tm]
        return x + mu_ref[...] * (xprev - x)

    xwa = mix(wa_ref, c_wa, mu_wa)
    r = mix(r_ref, c_r, mu_r)
    k = mix(k_ref, c_k, mu_k)
    v = mix(v_ref, c_v, mu_v)
    zw = w0_ref[...] + _mm(jnp.tanh(xwa), w2_ref[...], RW_PRE_PASSES)
    za = a0_ref[...] + _mm(xwa, a2_ref[...], RW_PRE_PASSES)
    log_w = _log_sigmoid(zw) - 0.5
    ld_out[...] = -jnp.exp(log_w)
    a = _sigmoid(za)
    if has_vres:
        gate = _sigmoid(v0_ref[...] + _mm(_mm(v, v1_ref[...], RW_PRE_PASSES), v2_ref[...],
                                          RW_PRE_PASSES))
        v = v + (vf_ref[...] - v) * gate
    kk = k * kk_ref[...]
    ss = _mm(kk * kk, e_ref[...], RW_PRE_PASSES)
    kn = kk / jnp.maximum(jnp.sqrt(ss), 1e-12)
    r_out[...] = r
    kh_out[...] = k * (1.0 + (a - 1.0) * ka_ref[...])
    v_out[...] = v
    kn_out[...] = kn
    b_out[...] = kn * a


def _rwkv_pre(proj, v_first, mu_wa, mu_r, mu_k, mu_v, w0, w2p, a0, a2p, k_k, k_a, e_head,
              v0, v1p, v2p, layer, batch, seq):
    m = proj.shape[0]
    tm = min(RW_TM, seq)
    nt = seq // tm
    has_vres = layer > 0
    w = RW_WIDTH

    def colblk(off, width):
        return pl.BlockSpec((tm, width), lambda b, t: (b * nt + t, off // width))

    def lay(shape):
        return pl.BlockSpec((None,) + shape, lambda b, t: (layer,) + (0,) * len(shape))

    def lay1(shape):
        return pl.BlockSpec((None,) + shape, lambda b, t: (layer - 1,) + (0,) * len(shape))

    in_specs = [colblk(OFF_WA, LANES), colblk(OFF_RWR, w), colblk(OFF_RWK, w), colblk(OFF_RWV, w),
                lay((1, LANES)), lay((1, w)), lay((1, w)), lay((1, w)),
                lay((1, w)), lay((LANES, w)), lay((1, w)), lay((LANES, w)),
                lay((1, w)), lay((1, w)),
                pl.BlockSpec((w, w), lambda b, t: (0, 0))]
    args = [proj, proj, proj, proj, mu_wa, mu_r, mu_k, mu_v, w0, w2p, a0, a2p, k_k, k_a, e_head]
    if has_vres:
        in_specs += [pl.BlockSpec((tm, w), lambda b, t: (b * nt + t, 0)),
                     lay1((1, w)), lay1((w, LANES)), lay1((LANES, w))]
        args += [v_first, v0, v1p, v2p]
    row = pl.BlockSpec((tm, w), lambda b, t: (b * nt + t, 0))
    return pl.pallas_call(
        functools.partial(_rwkv_pre_kernel, has_vres=has_vres, tm=tm),
        grid=(batch, nt),
        in_specs=in_specs,
        out_specs=[row] * 6,
        out_shape=[jax.ShapeDtypeStruct((m, w), F32)] * 6,
        scratch_shapes=[pltpu.VMEM((SUBLANES, LANES), F32), pltpu.VMEM((SUBLANES, w), F32),
                        pltpu.VMEM((SUBLANES, w), F32), pltpu.VMEM((SUBLANES, w), F32)],
        compiler_params=_params("parallel", "arbitrary"),
        name="rwkv_pre",
    )(*args)


RW_GROUP = 2


def _rwkv_scan_kernel(r_ref, ld_ref, kh_ref, v_ref, kn_ref, b_ref, z_ref, rk_ref, g_ref, bias_ref,
                      e_ref, out_ref, m_ref, ac_ref, gc_ref, qs_ref, y_ref, *, tt, chunk):
    @pl.when(pl.program_id(1) == 0)
    def _():
        m_ref[...] = jnp.zeros_like(m_ref)

    L = chunk
    L2 = 2 * L
    nchunk = tt // L
    npair = RW_WIDTH // LANES
    rows = lax.broadcasted_iota(jnp.int32, (L, L), 0)
    cols = lax.broadcasted_iota(jnp.int32, (L, L), 1)
    tri_b = jnp.where(rows >= cols, 1.0, 0.0).astype(BF16)
    lane = lax.broadcasted_iota(jnp.int32, (1, LANES), 1)
    m0 = jnp.where(lane < RW_HEAD_DIM, 1.0, 0.0)
    m1 = 1.0 - m0
    r2 = lax.broadcasted_iota(jnp.int32, (L2, L2), 0)
    c2 = lax.broadcasted_iota(jnp.int32, (L2, L2), 1)
    same_blk = (r2 < L) == (c2 < L)
    t2 = jnp.where(r2 < L, r2, r2 - L)
    s2 = jnp.where(c2 < L, c2, c2 - L)
    mask_strict = jnp.logical_and(same_blk, t2 > s2)
    mask_incl = jnp.logical_and(same_blk, t2 >= s2)
    eye2 = lax.broadcasted_iota(jnp.int32, (LANES, LANES), 0) == lax.broadcasted_iota(
        jnp.int32, (LANES, LANES), 1)
    n_double = max(1, (L - 1).bit_length())

    def stack(x):
        return jnp.concatenate([x * m0, x * m1], axis=0)

    def prepare(p, c):
        sl = pl.ds(pl.multiple_of(c * L, L), L)
        cs = slice(p * LANES, (p + 1) * LANES)
        ld = ld_ref[sl, cs]
        r, kh, v, kn, b = r_ref[sl, cs], kh_ref[sl, cs], v_ref[sl, cs], kn_ref[sl, cs], b_ref[sl, cs]
        ld_hi, ld_lo = _split_bf16(ld)
        logp = (jnp.dot(tri_b, ld_hi, preferred_element_type=F32)
                + jnp.dot(tri_b, ld_lo, preferred_element_type=F32))
        cmid = logp[L // 2 - 1:L // 2, :]
        e = logp - cmid
        e_last = e[L - 1:L, :]
        p_mid = jnp.exp(cmid)
        at_s = stack(-kn * jnp.exp(e - ld))
        rt_s = stack(r * jnp.exp(e))
        inv = jnp.exp(-e)
        tail = jnp.exp(e_last - e)
        lhs = jnp.concatenate([at_s, rt_s], axis=0).astype(BF16)
        rhs = jnp.concatenate([stack(b * inv), stack(kh * inv)], axis=0).astype(BF16)
        hat_t = jnp.concatenate([stack(b * tail), stack(kh * tail)], axis=0).T.astype(BF16)
        return dict(sl=sl, cs=cs, idx=p * nchunk + c, lhs=lhs, rhs=rhs, hat_t=hat_t,
                    at_true=at_s * p_mid, rt_true=rt_s * p_mid, v_s=stack(v).astype(BF16),
                    p_last=jnp.exp(e_last + cmid))

    def coeff_body(cg, carry):
        insts = [prepare(p, cg * RW_GROUP + g) for p in range(npair) for g in range(RW_GROUP)]
        for s in insts:
            aa = lax.dot_general(s["lhs"], s["rhs"], _NT, preferred_element_type=F32)
            s["nmat"] = jnp.where(mask_strict, aa[0:L2, 0:L2], 0.0).astype(BF16)
            s["a_ak"] = jnp.where(mask_strict, aa[0:L2, L2:], 0.0).astype(BF16)
            s["a_r"] = jnp.concatenate([jnp.where(mask_incl, aa[L2:, 0:L2], 0.0),
                                        jnp.where(mask_incl, aa[L2:, L2:], 0.0)],
                                       axis=1).astype(BF16)
        for s in insts:
            s["x"] = jnp.concatenate(
                [s["at_true"], jnp.dot(s["a_ak"], s["v_s"], preferred_element_type=F32)],
                axis=1)
        for it in range(n_double):
            for s in insts:
                s["x"] = s["x"] + jnp.dot(s["nmat"], s["x"].astype(BF16),
                                          preferred_element_type=F32)
            if it + 1 < n_double:
                for s in insts:
                    s["nmat"] = jnp.dot(s["nmat"], s["nmat"],
                                        preferred_element_type=F32).astype(BF16)
        for s in insts:
            big_l = jnp.concatenate([s["a_r"], s["hat_t"]], axis=0)
            big_r = jnp.concatenate(
                [s["x"].astype(BF16),
                 jnp.concatenate([jnp.zeros((L2, LANES), BF16), s["v_s"]], axis=1)], axis=0)
            res = jnp.dot(big_l, big_r, preferred_element_type=F32)
            q_s = s["rt_true"] + res[0:L2, :LANES]
            y0_s = res[0:L2, LANES:]
            qs_ref[s["sl"], s["cs"]] = q_s[0:L] + q_s[L:]
            y_ref[s["sl"], s["cs"]] = y0_s[0:L] + y0_s[L:]
            ac_ref[s["idx"]] = (jnp.where(eye2, jnp.broadcast_to(s["p_last"], (LANES, LANES)), 0.0)
                                + res[L2:, :LANES])
            gc_ref[s["idx"]] = res[L2:, LANES:]
        return carry

    lax.fori_loop(0, nchunk // RW_GROUP, coeff_body, 0)

    def chain_body(c, carry):
        sl = pl.ds(pl.multiple_of(c * L, L), L)
        for p in range(npair):
            cs = slice(p * LANES, (p + 1) * LANES)
            idx = p * nchunk + c
            mp = m_ref[p]
            y_ref[sl, cs] = _mm(qs_ref[sl, cs], mp) + y_ref[sl, cs]
            m_ref[p] = _mm(ac_ref[idx], mp, 3) + gc_ref[idx]
        return carry

    lax.fori_loop(0, nchunk, chain_body, 0)

    e_b = e_ref[...]
    inv_n = 1.0 / RW_HEAD_DIM

    def head_sum(x):
        hi, lo = _split_bf16(x)
        return (jnp.dot(hi, e_b, preferred_element_type=F32)
                + jnp.dot(lo, e_b, preferred_element_type=F32))

    for p in range(npair):
        cs = slice(p * LANES, (p + 1) * LANES)
        y = y_ref[:, cs]
        mu = head_sum(y) * inv_n
        d = y - mu
        var = head_sum(d * d) * inv_n
        yn = d * lax.rsqrt(var + RW_GN_EPS) * g_ref[:, cs] + bias_ref[:, cs]
        v_all = v_ref[:, cs]
        bonus = head_sum(r_ref[:, cs] * kh_ref[:, cs] * rk_ref[:, cs])
        out_ref[:, cs] = (_silu(z_ref[:, cs]) * (yn + bonus * v_all)).astype(BF16)


def _rwkv_scan(r, ld, kh, v, kn, b, proj, r_k3, ln_g3, ln_b3, e_pair, layer, batch, seq):
    m = r.shape[0]
    tt = min(RW_TT, seq)
    nt = seq // tt
    chunk = min(RW_L, tt)
    npair = RW_WIDTH // LANES
    w = RW_WIDTH
    blk = pl.BlockSpec((tt, w), lambda bb, t: (bb * nt + t, 0))
    par = pl.BlockSpec((None, 1, w), lambda bb, t: (layer, 0, 0))
    return pl.pallas_call(
        functools.partial(_rwkv_scan_kernel, tt=tt, chunk=chunk),
        grid=(batch, nt),
        in_specs=[blk] * 6 + [
            pl.BlockSpec((tt, w), lambda bb, t: (bb * nt + t, OFF_RWZ // w)),
            par, par, par,
            pl.BlockSpec((LANES, LANES), lambda bb, t: (0, 0)),
        ],
        out_specs=blk,
        out_shape=jax.ShapeDtypeStruct((m, w), BF16),
        scratch_shapes=[
            pltpu.VMEM((npair, LANES, LANES), F32),
            pltpu.VMEM((npair * (tt // chunk), LANES, LANES), F32),
            pltpu.VMEM((npair * (tt // chunk), LANES, LANES), F32),
            pltpu.VMEM((tt, w), F32),
            pltpu.VMEM((tt, w), F32),
        ],
        compiler_params=_params("parallel", "arbitrary"),
        name="rwkv_scan",
    )(r, ld, kh, v, kn, b, proj, r_k3, ln_g3, ln_b3, e_pair)


def _permute_w_in(w_in):
    def s(a, n):
        return w_in[:, :, a:a + n]

    pad = jnp.zeros(w_in.shape[:2] + (LANES - MLA_ROPE - 2 * ML_HEADS,), w_in.dtype)
    parts = [s(0, 1024), s(1024, 512), s(1544, 512), s(2056, 512), s(2568, 512), s(3080, 256),
             s(3336, 64), s(1536, 8), pad, s(5960, 128), s(4424, 512), s(4936, 512), s(5448, 512),
             s(6088, 512), s(3400, 1024)]
    return jnp.concatenate(parts, axis=-1).astype(BF16)


def _permute_w_uq(w_uq):
    d = w_uq.shape[0]
    w = w_uq.reshape(d, MLA_Q_RANK, MLA_HEADS, MLA_NOPE + MLA_ROPE)
    w = jnp.pad(w, ((0, 0), (0, 0), (0, 0), (0, MLA_QK_PAD - MLA_NOPE - MLA_ROPE)))
    return w.reshape(d, MLA_Q_RANK, MLA_HEADS * MLA_QK_PAD).astype(BF16)


def _split_w_ukv(w_ukv):
    d = w_ukv.shape[0]
    w = w_ukv.reshape(d, MLA_KV_RANK, MLA_HEADS, MLA_NOPE + MLA_V)
    wk = w[..., :MLA_NOPE].reshape(d, MLA_KV_RANK, MLA_HEADS * MLA_NOPE)
    wv = w[..., MLA_NOPE:].reshape(d, MLA_KV_RANK, MLA_HEADS * MLA_V)
    return wk.astype(BF16), wv.astype(BF16)


def _rope_tables(positions):
    inv_freq = jnp.power(ROPE_BASE, -jnp.arange(0, MLA_ROPE, 2, dtype=F32) / MLA_ROPE)
    ang = positions.astype(F32).reshape(-1, 1) * inv_freq
    cos, sin = jnp.cos(ang), jnp.sin(ang)
    z32 = jnp.zeros_like(cos)
    z64 = jnp.zeros((ang.shape[0], LANES - MLA_ROPE), F32)
    cosf = jnp.concatenate([cos, cos, z64], axis=-1)
    s1 = jnp.concatenate([-sin, z32, z64], axis=-1)
    s2 = jnp.concatenate([z32, sin, z64], axis=-1)
    return cosf, s1, s2


def kernel(x, positions, norm_g, w_in, ml_conv_w, ml_conv_b, ml_i_bias, ml_f_bias, ml_norm_g,
           mla_q_norm_g, mla_w_uq, mla_kv_norm_g, mla_w_ukv, rw_mu, rw_w0, rw_w2, rw_a0, rw_a2,
           rw_v0, rw_v1, rw_v2, rw_k_k, rw_k_a, rw_r_k, rw_ln_g, rw_ln_b, w_out, final_norm_g):
    batch, seq, _ = x.shape
    depth = w_in.shape[0]
    m = batch * seq
    x2 = x.reshape(m, D_MODEL)

    w_in_p = _permute_w_in(w_in)
    w_uq_p = _permute_w_uq(mla_w_uq)
    w_uk_p, w_uv_p = _split_w_ukv(mla_w_ukv)
    w_out_b = w_out.astype(BF16)
    cosf, s1, s2 = _rope_tables(positions)

    def row3(a):
        return a.reshape(a.shape[0], 1, a.shape[1])

    norm_g3 = row3(norm_g)
    conv_b3 = row3(ml_conv_b)
    ml_norm_g3 = row3(ml_norm_g)
    qg3, kvg3 = row3(mla_q_norm_g), row3(mla_kv_norm_g)
    w = RW_WIDTH
    mu_r, mu_k, mu_v = row3(rw_mu[:, 0:w]), row3(rw_mu[:, w:2 * w]), row3(rw_mu[:, 2 * w:3 * w])
    mu_wa = row3(rw_mu[:, 3 * w:])
    w0, a0, k_k, k_a = row3(rw_w0), row3(rw_a0), row3(rw_k_k), row3(rw_k_a)
    w2p = jnp.pad(rw_w2, ((0, 0), (0, LANES - RW_DECAY_RANK), (0, 0)))
    a2p = jnp.pad(rw_a2, ((0, 0), (LANES - RW_AAA_RANK, 0), (0, 0)))
    v0 = row3(rw_v0)
    v1p = jnp.pad(rw_v1, ((0, 0), (0, 0), (0, LANES - RW_MV_RANK)))
    v2p = jnp.pad(rw_v2, ((0, 0), (0, LANES - RW_MV_RANK), (0, 0)))
    r_k3 = rw_r_k.reshape(depth, 1, w)
    ln_g3, ln_b3 = row3(rw_ln_g), row3(rw_ln_b)
    hid = jnp.arange(w) // RW_HEAD_DIM
    e_head = (hid[:, None] == hid[None, :]).astype(F32)
    e_pair = e_head[:LANES, :LANES].astype(BF16)
    final_g2 = final_norm_g.reshape(1, D_MODEL)

    v_first = None
    for layer in range(depth):
        proj = _inproj(x2, norm_g3, w_in_p, layer)
        gates_t = proj[:, OFF_KRG + MLA_ROPE:OFF_KRG + MLA_ROPE + 2 * ML_HEADS]
        gates_t = gates_t.reshape(batch, seq, 2 * ML_HEADS).transpose(0, 2, 1)
        gates_t = gates_t.reshape(batch, 2 * ML_HEADS, 1, seq)
        y_ml = _mlstm(proj, gates_t, ml_i_bias, ml_f_bias, ml_conv_w, conv_b3, ml_norm_g3,
                      layer, batch, seq)
        q, k, v = _mla_proj(proj, cosf, s1, s2, qg3, w_uq_p, kvg3, w_uk_p, w_uv_p, layer)
        y_mla = _attention(q, k, v, proj, batch, seq)
        r_s, ld, kh, v_rw, kn, b_rw = _rwkv_pre(proj, v_first, mu_wa, mu_r, mu_k, mu_v, w0, w2p,
                                                 a0, a2p, k_k, k_a, e_head, v0, v1p, v2p,
                                                 layer, batch, seq)
        if layer == 0:
            v_first = v_rw
        y_rw = _rwkv_scan(r_s, ld, kh, v_rw, kn, b_rw, proj, r_k3, ln_g3, ln_b3, e_pair,
                          layer, batch, seq)
        x2 = _outproj(x2, y_ml, y_mla, y_rw, w_out_b, final_g2, layer, layer == depth - 1)
    return x2.reshape(batch, seq, D_MODEL)
```

```python
import functools

import jax
import jax.numpy as jnp
from jax import lax
from jax.experimental import pallas as pl
from jax.experimental.pallas import tpu as pltpu

F32 = jnp.float32
BF16 = jnp.bfloat16

D_MODEL = 2048
NORM_EPS = 1e-6
ML_HEADS = 4
ML_HEAD_DIM = 128
ML_WIDTH = 512
ML_CONV = 4
MLA_HEADS = 8
MLA_NOPE = 128
MLA_ROPE = 64
MLA_V = 128
MLA_WIDTH = 1024
MLA_Q_RANK = 512
MLA_KV_RANK = 256
ROPE_BASE = 10000.0
RW_HEAD_DIM = 64
RW_WIDTH = 512
RW_HEADS = 8
RW_DECAY_RANK = 64
RW_AAA_RANK = 64
RW_MV_RANK = 32
RW_GN_EPS = 64e-5

LANES = 128
SUBLANES = 8
MLA_QK_PAD = 256
VMEM_LIMIT = 48 * 1024 * 1024
NEG_BIG = -1e30
LOG2E = 1.4426950408889634

OFF_QK = 0
OFF_MLV = 1024
OFF_MLO = 1536
OFF_MLZ = 2048
OFF_CQ = 2560
OFF_CKV = 3072
OFF_KRG = 3328
OFF_WA = 3456
OFF_RWR = 3584
OFF_RWK = 4096
OFF_RWV = 4608
OFF_RWZ = 5120
OFF_MLAZ = 5632
D_IN_PAD = 6656

TM_PROJ = 512
TN_PROJ = 1664
TM_OUT = 512
TQ = 512
ML_TT = 1024
ML_L = 256
RW_TM = 512
RW_TT = 512
RW_L = 64


def _split_bf16(x):
    hi = x.astype(BF16)
    lo = (x - hi.astype(F32)).astype(BF16)
    return hi, lo


def _mm(a, b, passes=1):
    if passes == 1:
        return jnp.dot(a.astype(BF16), b.astype(BF16), preferred_element_type=F32)
    ah, al = _split_bf16(a)
    bh, bl = _split_bf16(b)
    return (jnp.dot(ah, bh, preferred_element_type=F32)
            + jnp.dot(ah, bl, preferred_element_type=F32)
            + jnp.dot(al, bh, preferred_element_type=F32))


_NT = (((1,), (1,)), ((), ()))


def _mm_nt(a, b, passes=1):
    if passes == 1:
        return lax.dot_general(a.astype(BF16), b.astype(BF16), _NT, preferred_element_type=F32)
    ah, al = _split_bf16(a)
    bh, bl = _split_bf16(b)
    return (lax.dot_general(ah, bh, _NT, preferred_element_type=F32)
            + lax.dot_general(ah, bl, _NT, preferred_element_type=F32)
            + lax.dot_general(al, bh, _NT, preferred_element_type=F32))


def _sigmoid(x):
    return 1.0 / (1.0 + jnp.exp(-x))


def _silu(x):
    return x * _sigmoid(x)


def _log_sigmoid(x):
    return jnp.minimum(x, 0.0) - jnp.log(1.0 + jnp.exp(-jnp.abs(x)))


def _params(*sem):
    return pltpu.CompilerParams(dimension_semantics=sem, vmem_limit_bytes=VMEM_LIMIT)


def _inproj_kernel(x_ref, g_ref, w_ref, o_ref, h_ref):
    @pl.when(pl.program_id(1) == 0)
    def _():
        x = x_ref[...]
        ms = jnp.mean(x * x, axis=-1, keepdims=True)
        h_ref[...] = (x * lax.rsqrt(ms + NORM_EPS) * g_ref[...]).astype(BF16)

    o_ref[...] = jnp.dot(h_ref[...], w_ref[...], preferred_element_type=F32)


def _inproj(x2, norm_g3, w_in_p, layer):
    m = x2.shape[0]
    return pl.pallas_call(
        _inproj_kernel,
        grid=(m // TM_PROJ, D_IN_PAD // TN_PROJ),
        in_specs=[
            pl.BlockSpec((TM_PROJ, D_MODEL), lambda i, j: (i, 0)),
            pl.BlockSpec((None, 1, D_MODEL), lambda i, j: (layer, 0, 0)),
            pl.BlockSpec((None, D_MODEL, TN_PROJ), lambda i, j: (layer, 0, j)),
        ],
        out_specs=pl.BlockSpec((TM_PROJ, TN_PROJ), lambda i, j: (i, j)),
        out_shape=jax.ShapeDtypeStruct((m, D_IN_PAD), F32),
        scratch_shapes=[pltpu.VMEM((TM_PROJ, D_MODEL), BF16)],
        compiler_params=_params("parallel", "arbitrary"),
        name="inproj",
    )(x2, norm_g3, w_in_p)


def _outproj_kernel(x_ref, yml_ref, ymla_ref, yrw_ref, w_ref, g_ref, o_ref, *, final):
    acc = x_ref[...]
    acc = acc + jnp.dot(yml_ref[...], w_ref[0:ML_WIDTH, :], preferred_element_type=F32)
    acc = acc + jnp.dot(ymla_ref[...], w_ref[ML_WIDTH:ML_WIDTH + MLA_WIDTH, :],
                        preferred_element_type=F32)
    acc = acc + jnp.dot(yrw_ref[...], w_ref[ML_WIDTH + MLA_WIDTH:, :], preferred_element_type=F32)
    if final:
        ms = jnp.mean(acc * acc, axis=-1, keepdims=True)
        acc = acc * lax.rsqrt(ms + NORM_EPS) * g_ref[...]
    o_ref[...] = acc


def _outproj(x2, y_ml, y_mla, y_rw, w_out_b, final_g2, layer, final):
    m = x2.shape[0]
    return pl.pallas_call(
        functools.partial(_outproj_kernel, final=final),
        grid=(m // TM_OUT,),
        in_specs=[
            pl.BlockSpec((TM_OUT, D_MODEL), lambda i: (i, 0)),
            pl.BlockSpec((TM_OUT, ML_WIDTH), lambda i: (i, 0)),
            pl.BlockSpec((TM_OUT, MLA_WIDTH), lambda i: (i, 0)),
            pl.BlockSpec((TM_OUT, RW_WIDTH), lambda i: (i, 0)),
            pl.BlockSpec((None, D_MODEL, D_MODEL), lambda i: (layer, 0, 0)),
            pl.BlockSpec((1, D_MODEL), lambda i: (0, 0)),
        ],
        out_specs=pl.BlockSpec((TM_OUT, D_MODEL), lambda i: (i, 0)),
        out_shape=jax.ShapeDtypeStruct((m, D_MODEL), F32),
        compiler_params=_params("parallel"),
        name="outproj",
    )(x2, y_ml, y_mla, y_rw, w_out_b, final_g2)


def _mla_proj_kernel(cq_ref, ckv_ref, krg_ref, cos_ref, s1_ref, s2_ref, qg_ref, wq_ref, kvg_ref,
                     wk_ref, wv_ref, q_out, k_out, v_out):
    cq = cq_ref[...]
    cqn = (cq * lax.rsqrt(jnp.mean(cq * cq, axis=-1, keepdims=True) + NORM_EPS)
           * qg_ref[...]).astype(BF16)
    q = jnp.dot(cqn, wq_ref[...], preferred_element_type=F32)
    ckv = ckv_ref[...]
    ckvn = (ckv * lax.rsqrt(jnp.mean(ckv * ckv, axis=-1, keepdims=True) + NORM_EPS)
            * kvg_ref[...]).astype(BF16)
    kn = jnp.dot(ckvn, wk_ref[...], preferred_element_type=F32)
    v = jnp.dot(ckvn, wv_ref[...], preferred_element_type=F32)
    cosf, s1, s2 = cos_ref[...], s1_ref[...], s2_ref[...]

    def rope(xb):
        return (xb * cosf + pltpu.roll(xb, LANES - MLA_ROPE // 2, 1) * s1
                + pltpu.roll(xb, MLA_ROPE // 2, 1) * s2)

    scale = (MLA_NOPE + MLA_ROPE) ** -0.5 * LOG2E
    kr = rope(krg_ref[...]).astype(BF16)
    for h in range(MLA_HEADS):
        c0 = h * MLA_QK_PAD
        q_out[:, c0:c0 + MLA_NOPE] = (q[:, c0:c0 + MLA_NOPE] * scale).astype(BF16)
        q_out[:, c0 + MLA_NOPE:c0 + MLA_QK_PAD] = (
            rope(q[:, c0 + MLA_NOPE:c0 + MLA_QK_PAD]) * scale).astype(BF16)
        k_out[:, c0:c0 + MLA_NOPE] = kn[:, h * MLA_NOPE:(h + 1) * MLA_NOPE].astype(BF16)
        k_out[:, c0 + MLA_NOPE:c0 + MLA_QK_PAD] = kr
    v_out[...] = v.T.astype(BF16)


def _mla_proj(proj, cosf, s1, s2, qg3, wq_p, kvg3, wk_p, wv_p, layer):
    m = proj.shape[0]
    tm = TQ
    row = lambda i: (i, 0)
    return pl.pallas_call(
        _mla_proj_kernel,
        grid=(m // tm,),
        in_specs=[
            pl.BlockSpec((tm, MLA_Q_RANK), lambda i: (i, OFF_CQ // MLA_Q_RANK)),
            pl.BlockSpec((tm, MLA_KV_RANK), lambda i: (i, OFF_CKV // MLA_KV_RANK)),
            pl.BlockSpec((tm, LANES), lambda i: (i, OFF_KRG // LANES)),
            pl.BlockSpec((tm, LANES), row),
            pl.BlockSpec((tm, LANES), row),
            pl.BlockSpec((tm, LANES), row),
            pl.BlockSpec((None, 1, MLA_Q_RANK), lambda i: (layer, 0, 0)),
            pl.BlockSpec((None, MLA_Q_RANK, MLA_HEADS * MLA_QK_PAD), lambda i: (layer, 0, 0)),
            pl.BlockSpec((None, 1, MLA_KV_RANK), lambda i: (layer, 0, 0)),
            pl.BlockSpec((None, MLA_KV_RANK, MLA_HEADS * MLA_NOPE), lambda i: (layer, 0, 0)),
            pl.BlockSpec((None, MLA_KV_RANK, MLA_HEADS * MLA_V), lambda i: (layer, 0, 0)),
        ],
        out_specs=[
            pl.BlockSpec((tm, MLA_HEADS * MLA_QK_PAD), row),
            pl.BlockSpec((tm, MLA_HEADS * MLA_QK_PAD), row),
            pl.BlockSpec((None, MLA_HEADS * MLA_V, tm), lambda i: (i, 0, 0)),
        ],
        out_shape=[
            jax.ShapeDtypeStruct((m, MLA_HEADS * MLA_QK_PAD), BF16),
            jax.ShapeDtypeStruct((m, MLA_HEADS * MLA_QK_PAD), BF16),
            jax.ShapeDtypeStruct((m // tm, MLA_HEADS * MLA_V, tm), BF16),
        ],
        compiler_params=_params("parallel"),
        name="mla_proj",
    )(proj, proj, proj, cosf, s1, s2, qg3, wq_p, kvg3, wk_p, wv_p)


ATTN_HEADS_PER_STEP = 4


def _attn_kernel(q_ref, k_ref, vt_ref, z_ref, o_ref, *, tq):
    i = pl.program_id(2)
    nh = ATTN_HEADS_PER_STEP
    qs = [q_ref[:, h * MLA_QK_PAD:(h + 1) * MLA_QK_PAD] for h in range(nh)]
    kidx = lax.broadcasted_iota(jnp.int32, (tq, tq), 0)
    qidx = lax.broadcasted_iota(jnp.int32, (tq, tq), 1)

    def step(j, carry, diagonal):
        off = pl.multiple_of(j * tq, tq)
        ss = []
        for h in range(nh):
            kb = k_ref[pl.ds(off, tq), h * MLA_QK_PAD:(h + 1) * MLA_QK_PAD]
            s = lax.dot_general(kb, qs[h], _NT, preferred_element_type=F32)
            ss.append(jnp.where(kidx <= qidx, s, NEG_BIG) if diagonal else s)
        out = []
        for h in range(nh):
            m, l, acc = carry[h]
            s = ss[h]
            m_new = jnp.maximum(m, jnp.max(s, axis=0, keepdims=True))
            alpha = jnp.exp2(m - m_new)
            p = jnp.exp2(s - m_new)
            l = alpha * l + jnp.sum(p, axis=0, keepdims=True)
            acc = alpha * acc + jnp.dot(vt_ref[j, h], p.astype(BF16), preferred_element_type=F32)
            out.append((m_new, l, acc))
        return tuple(out)

    init = tuple((jnp.full((1, tq), NEG_BIG, F32), jnp.zeros((1, tq), F32),
                  jnp.zeros((MLA_V, tq), F32)) for _ in range(nh))
    carry = lax.fori_loop(0, i, lambda j, c: step(j, c, False), init)
    carry = step(i, carry, True)
    for h in range(nh):
        _, l, acc = carry[h]
        y = (acc / l).T
        cs = slice(h * MLA_V, (h + 1) * MLA_V)
        o_ref[:, cs] = (_silu(z_ref[:, cs]) * y).astype(BF16)


def _attention(q, k, v_t, proj, batch, seq):
    m = q.shape[0]
    nq = seq // TQ
    nh = ATTN_HEADS_PER_STEP
    v_t5 = v_t.reshape(batch, nq, MLA_HEADS, MLA_V, TQ)
    return pl.pallas_call(
        functools.partial(_attn_kernel, tq=TQ),
        grid=(batch, MLA_HEADS // nh, nq),
        in_specs=[
            pl.BlockSpec((TQ, nh * MLA_QK_PAD), lambda b, h, i: (b * nq + i, h)),
            pl.BlockSpec((seq, nh * MLA_QK_PAD), lambda b, h, i: (b, h)),
            pl.BlockSpec((None, nq, nh, MLA_V, TQ), lambda b, h, i: (b, 0, h, 0, 0)),
            pl.BlockSpec((TQ, nh * MLA_V),
                         lambda b, h, i: (b * nq + i, OFF_MLAZ // (nh * MLA_V) + h)),
        ],
        out_specs=pl.BlockSpec((TQ, nh * MLA_V), lambda b, h, i: (b * nq + i, h)),
        out_shape=jax.ShapeDtypeStruct((m, MLA_WIDTH), BF16),
        compiler_params=_params("parallel", "parallel", "arbitrary"),
        name="mla_attn",
    )(q, k, v_t5, proj)


ML_PASSES = 1


def _shift_rows(x, prev8, s):
    rolled = pltpu.roll(x, s, 0)
    prev_rolled = pltpu.roll(prev8, s, 0)
    rid = lax.broadcasted_iota(jnp.int32, (SUBLANES, x.shape[1]), 0)
    top = jnp.where(rid < s, prev_rolled, rolled[0:SUBLANES])
    return jnp.concatenate([top, rolled[SUBLANES:]], axis=0)


def _mlstm_kernel(ib_ref, fb_ref, q_ref, k_ref, v_ref, o_ref, z_ref, gi_ref, gf_ref, wq_ref, wk_ref,
                  bq_ref, bk_ref, ng_ref, out_ref, ct_ref, n_ref, m_ref, qp_ref, kp_ref,
                  *, layer, tt, chunk):
    h = pl.program_id(1)
    t = pl.program_id(2)

    @pl.when(t == 0)
    def _():
        ct_ref[...] = jnp.zeros_like(ct_ref)
        n_ref[...] = jnp.zeros_like(n_ref)
        m_ref[...] = jnp.zeros_like(m_ref)
        qp_ref[...] = jnp.zeros_like(qp_ref)
        kp_ref[...] = jnp.zeros_like(kp_ref)

    def conv_silu(x_ref, prev_ref, w_ref, b_ref):
        x = x_ref[...]
        prev8 = prev_ref[...]
        w = w_ref[...]
        y = b_ref[...] + w[ML_CONV - 1:ML_CONV] * x
        for s in range(1, ML_CONV):
            y = y + w[ML_CONV - 1 - s:ML_CONV - s] * _shift_rows(x, prev8, s)
        prev_ref[...] = x[tt - SUBLANES:tt]
        return _silu(y)

    q_all = conv_silu(q_ref, qp_ref, wq_ref, bq_ref) * (ML_HEAD_DIM ** -0.5)
    k_all = conv_silu(k_ref, kp_ref, wk_ref, bk_ref)
    li_row_all = gi_ref[...] + ib_ref[layer, h]
    lf_row_all = _log_sigmoid(gf_ref[...] + fb_ref[layer, h])

    L = chunk
    rows = lax.broadcasted_iota(jnp.int32, (L, L), 0)
    cols = lax.broadcasted_iota(jnp.int32, (L, L), 1)
    tri = rows >= cols
    eye = rows == cols
    for c in range(tt // L):
        sl = slice(c * L, (c + 1) * L)
        qc, kc, vc = q_all[sl], k_all[sl], v_ref[sl, :]
        li_row = li_row_all[:, sl]
        lf_row = lf_row_all[:, sl]
        lf_b = jnp.broadcast_to(lf_row, (L, L))
        li_b = jnp.broadcast_to(li_row, (L, L))
        b_col = jnp.sum(jnp.where(tri, lf_b, 0.0), axis=-1, keepdims=True)
        lf_col = jnp.sum(jnp.where(eye, lf_b, 0.0), axis=-1, keepdims=True)
        li_col = jnp.sum(jnp.where(eye, li_b, 0.0), axis=-1, keepdims=True)
        b_row = jnp.sum(jnp.where(rows <= cols, jnp.broadcast_to(lf_col, (L, L)), 0.0),
                        axis=0, keepdims=True)
        m_prev = m_ref[...][:, 0:1]
        dmat = jnp.where(tri, b_col - b_row + li_row, NEG_BIG)
        m_inter = b_col + m_prev
        m_t = jnp.maximum(m_inter, jnp.max(dmat, axis=-1, keepdims=True))
        inter = jnp.exp(m_inter - m_t)
        smat = _mm_nt(qc, kc, ML_PASSES) * jnp.exp(dmat - m_t)
        ct = ct_ref[...]
        n_row = n_ref[...]
        num = inter * _mm(qc, ct, ML_PASSES) + _mm(smat, vc, ML_PASSES)
        den = (inter * jnp.sum(qc * n_row, axis=-1, keepdims=True)
               + jnp.sum(smat, axis=-1, keepdims=True))
        hh = num / jnp.maximum(jnp.abs(den), jnp.exp(-m_t))
        b_last = b_col[L - 1:L, :]
        g_col = b_last - b_col + li_col
        m_new = jnp.maximum(b_last + m_prev, jnp.max(g_col, axis=0, keepdims=True))
        decay = jnp.exp(b_last + m_prev - m_new)
        kw = kc * jnp.exp(g_col - m_new)
        ct_ref[...] = decay * ct + _mm(kw.T, vc, ML_PASSES)
        n_ref[...] = decay * n_row + jnp.sum(kw, axis=0, keepdims=True)
        m_ref[...] = jnp.broadcast_to(m_new, m_ref.shape)
        mu = jnp.mean(hh, axis=-1, keepdims=True)
        dd = hh - mu
        var = jnp.mean(dd * dd, axis=-1, keepdims=True)
        yn = dd * lax.rsqrt(var + NORM_EPS) * ng_ref[...]
        out_ref[sl, :] = (_silu(z_ref[sl, :]) * (_sigmoid(o_ref[sl, :]) * yn)).astype(BF16)


def _mlstm(proj, gates_t, i_bias, f_bias, conv_w, conv_b3, norm_g3, layer, batch, seq):
    m = proj.shape[0]
    tt = min(ML_TT, seq)
    nt = seq // tt
    hd = ML_HEAD_DIM

    def col(off):
        return lambda b, h, t: (b * nt + t, off // hd + h)

    smem = pl.BlockSpec(memory_space=pltpu.SMEM)
    return pl.pallas_call(
        functools.partial(_mlstm_kernel, layer=layer, tt=tt, chunk=min(ML_L, tt)),
        grid=(batch, ML_HEADS, nt),
        in_specs=[
            smem, smem,
            pl.BlockSpec((tt, hd), col(OFF_QK)),
            pl.BlockSpec((tt, hd), col(OFF_QK + ML_WIDTH)),
            pl.BlockSpec((tt, hd), col(OFF_MLV)),
            pl.BlockSpec((tt, hd), col(OFF_MLO)),
            pl.BlockSpec((tt, hd), col(OFF_MLZ)),
            pl.BlockSpec((None, None, 1, tt), lambda b, h, t: (b, h, 0, t)),
            pl.BlockSpec((None, None, 1, tt), lambda b, h, t: (b, ML_HEADS + h, 0, t)),
            pl.BlockSpec((None, ML_CONV, hd), lambda b, h, t: (layer, 0, h)),
            pl.BlockSpec((None, ML_CONV, hd), lambda b, h, t: (layer, 0, ML_HEADS + h)),
            pl.BlockSpec((None, 1, hd), lambda b, h, t: (layer, 0, h)),
            pl.BlockSpec((None, 1, hd), lambda b, h, t: (layer, 0, ML_HEADS + h)),
            pl.BlockSpec((None, 1, hd), lambda b, h, t: (layer, 0, h)),
        ],
        out_specs=pl.BlockSpec((tt, hd), lambda b, h, t: (b * nt + t, h)),
        out_shape=jax.ShapeDtypeStruct((m, ML_WIDTH), BF16),
        scratch_shapes=[
            pltpu.VMEM((hd, hd), F32),
            pltpu.VMEM((1, hd), F32),
            pltpu.VMEM((1, hd), F32),
            pltpu.VMEM((SUBLANES, hd), F32),
            pltpu.VMEM((SUBLANES, hd), F32),
        ],
        compiler_params=_params("parallel", "parallel", "arbitrary"),
        name="mlstm",
    )(i_bias, f_bias, proj, proj, proj, proj, proj, gates_t, gates_t, conv_w, conv_w,
      conv_b3, conv_b3, norm_g3)


RW_PRE_PASSES = 3


def _rwkv_pre_kernel(*refs, has_vres, tm):
    if has_vres:
        (wa_ref, r_ref, k_ref, v_ref, mu_wa, mu_r, mu_k, mu_v, w0_ref, w2_ref, a0_ref, a2_ref,
         kk_ref, ka_ref, e_ref, vf_ref, v0_ref, v1_ref, v2_ref,
         r_out, ld_out, kh_out, v_out, kn_out, b_out, c_wa, c_r, c_k, c_v) = refs
    else:
        (wa_ref, r_ref, k_ref, v_ref, mu_wa, mu_r, mu_k, mu_v, w0_ref, w2_ref, a0_ref, a2_ref,
         kk_ref, ka_ref, e_ref,
         r_out, ld_out, kh_out, v_out, kn_out, b_out, c_wa, c_r, c_k, c_v) = refs

    @pl.when(pl.program_id(1) == 0)
    def _():
        for c in (c_wa, c_r, c_k, c_v):
            c[...] = jnp.zeros_like(c)

    def mix(x_ref, c_ref, mu_ref):
        x = x_ref[...]
        rolled = pltpu.roll(x, 1, 0)
        rid = lax.broadcasted_iota(jnp.int32, (SUBLANES, x.shape[1]), 0)
        prev_last = jnp.broadcast_to(c_ref[SUBLANES - 1:SUBLANES, :], (SUBLANES, x.shape[1]))
        top = jnp.where(rid == 0, prev_last, rolled[0:SUBLANES])
        xprev = jnp.concatenate([top, rolled[SUBLANES:]], axis=0)
        c_ref[...] = x[tm - SUBLANES:tm]
        return x + mu_ref[...] * (xprev - x)

    xwa = mix(wa_ref, c_wa, mu_wa)
    r = mix(r_ref, c_r, mu_r)
    k = mix(k_ref, c_k, mu_k)
    v = mix(v_ref, c_v, mu_v)
    zw = w0_ref[...] + _mm(jnp.tanh(xwa), w2_ref[...], RW_PRE_PASSES)
    za = a0_ref[...] + _mm(xwa, a2_ref[...], RW_PRE_PASSES)
    log_w = _log_sigmoid(zw) - 0.5
    ld_out[...] = -jnp.exp(log_w)
    a = _sigmoid(za)
    if has_vres:
        gate = _sigmoid(v0_ref[...] + _mm(_mm(v, v1_ref[...], RW_PRE_PASSES), v2_ref[...],
                                          RW_PRE_PASSES))
        v = v + (vf_ref[...] - v) * gate
    kk = k * kk_ref[...]
    ss = _mm(kk * kk, e_ref[...], RW_PRE_PASSES)
    kn = kk / jnp.maximum(jnp.sqrt(ss), 1e-12)
    r_out[...] = r
    kh_out[...] = k * (1.0 + (a - 1.0) * ka_ref[...])
    v_out[...] = v
    kn_out[...] = kn
    b_out[...] = kn * a


def _rwkv_pre(proj, v_first, mu_wa, mu_r, mu_k, mu_v, w0, w2p, a0, a2p, k_k, k_a, e_head,
              v0, v1p, v2p, layer, batch, seq):
    m = proj.shape[0]
    tm = min(RW_TM, seq)
    nt = seq // tm
    has_vres = layer > 0
    w = RW_WIDTH

    def colblk(off, width):
        return pl.BlockSpec((tm, width), lambda b, t: (b * nt + t, off // width))

    def lay(shape):
        return pl.BlockSpec((None,) + shape, lambda b, t: (layer,) + (0,) * len(shape))

    def lay1(shape):
        return pl.BlockSpec((None,) + shape, lambda b, t: (layer - 1,) + (0,) * len(shape))

    in_specs = [colblk(OFF_WA, LANES), colblk(OFF_RWR, w), colblk(OFF_RWK, w), colblk(OFF_RWV, w),
                lay((1, LANES)), lay((1, w)), lay((1, w)), lay((1, w)),
                lay((1, w)), lay((LANES, w)), lay((1, w)), lay((LANES, w)),
                lay((1, w)), lay((1, w)),
                pl.BlockSpec((w, w), lambda b, t: (0, 0))]
    args = [proj, proj, proj, proj, mu_wa, mu_r, mu_k, mu_v, w0, w2p, a0, a2p, k_k, k_a, e_head]
    if has_vres:
        in_specs += [pl.BlockSpec((tm, w), lambda b, t: (b * nt + t, 0)),
                     lay1((1, w)), lay1((w, LANES)), lay1((LANES, w))]
        args += [v_first, v0, v1p, v2p]
    row = pl.BlockSpec((tm, w), lambda b, t: (b * nt + t, 0))
    return pl.pallas_call(
        functools.partial(_rwkv_pre_kernel, has_vres=has_vres, tm=tm),
        grid=(batch, nt),
        in_specs=in_specs,
        out_specs=[row] * 6,
        out_shape=[jax.ShapeDtypeStruct((m, w), F32)] * 6,
        scratch_shapes=[pltpu.VMEM((SUBLANES, LANES), F32), pltpu.VMEM((SUBLANES, w), F32),
                        pltpu.VMEM((SUBLANES, w), F32), pltpu.VMEM((SUBLANES, w), F32)],
        compiler_params=_params("parallel", "arbitrary"),
        name="rwkv_pre",
    )(*args)


RW_GROUP = 2


def _rwkv_scan_kernel(r_ref, ld_ref, kh_ref, v_ref, kn_ref, b_ref, z_ref, rk_ref, g_ref, bias_ref,
                      e_ref, out_ref, m_ref, ac_ref, gc_ref, qs_ref, y_ref, *, tt, chunk):
    @pl.when(pl.program_id(1) == 0)
    def _():
        m_ref[...] = jnp.zeros_like(m_ref)

    L = chunk
    L2 = 2 * L
    nchunk = tt // L
    npair = RW_WIDTH // LANES
    rows = lax.broadcasted_iota(jnp.int32, (L, L), 0)
    cols = lax.broadcasted_iota(jnp.int32, (L, L), 1)
    tri_b = jnp.where(rows >= cols, 1.0, 0.0).astype(BF16)
    lane = lax.broadcasted_iota(jnp.int32, (1, LANES), 1)
    m0 = jnp.where(lane < RW_HEAD_DIM, 1.0, 0.0)
    m1 = 1.0 - m0
    r2 = lax.broadcasted_iota(jnp.int32, (L2, L2), 0)
    c2 = lax.broadcasted_iota(jnp.int32, (L2, L2), 1)
    same_blk = (r2 < L) == (c2 < L)
    t2 = jnp.where(r2 < L, r2, r2 - L)
    s2 = jnp.where(c2 < L, c2, c2 - L)
    mask_strict = jnp.logical_and(same_blk, t2 > s2)
    mask_incl = jnp.logical_and(same_blk, t2 >= s2)
    eye2 = lax.broadcasted_iota(jnp.int32, (LANES, LANES), 0) == lax.broadcasted_iota(
        jnp.int32, (LANES, LANES), 1)
    n_double = max(1, (L - 1).bit_length())

    def stack(x):
        return jnp.concatenate([x * m0, x * m1], axis=0)

    def prepare(p, c):
        sl = pl.ds(pl.multiple_of(c * L, L), L)
        cs = slice(p * LANES, (p + 1) * LANES)
        ld = ld_ref[sl, cs]
        r, kh, v, kn, b = r_ref[sl, cs], kh_ref[sl, cs], v_ref[sl, cs], kn_ref[sl, cs], b_ref[sl, cs]
        ld_hi, ld_lo = _split_bf16(ld)
        logp = (jnp.dot(tri_b, ld_hi, preferred_element_type=F32)
                + jnp.dot(tri_b, ld_lo, preferred_element_type=F32))
        cmid = logp[L // 2 - 1:L // 2, :]
        e = logp - cmid
        e_last = e[L - 1:L, :]
        p_mid = jnp.exp(cmid)
        at_s = stack(-kn * jnp.exp(e - ld))
        rt_s = stack(r * jnp.exp(e))
        inv = jnp.exp(-e)
        tail = jnp.exp(e_last - e)
        lhs = jnp.concatenate([at_s, rt_s], axis=0).astype(BF16)
        rhs = jnp.concatenate([stack(b * inv), stack(kh * inv)], axis=0).astype(BF16)
        hat_t = jnp.concatenate([stack(b * tail), stack(kh * tail)], axis=0).T.astype(BF16)
        return dict(sl=sl, cs=cs, idx=p * nchunk + c, lhs=lhs, rhs=rhs, hat_t=hat_t,
                    at_true=at_s * p_mid, rt_true=rt_s * p_mid, v_s=stack(v).astype(BF16),
                    p_last=jnp.exp(e_last + cmid))

    def coeff_body(cg, carry):
        insts = [prepare(p, cg * RW_GROUP + g) for p in range(npair) for g in range(RW_GROUP)]
        for s in insts:
            aa = lax.dot_general(s["lhs"], s["rhs"], _NT, preferred_element_type=F32)
            s["nmat"] = jnp.where(mask_strict, aa[0:L2, 0:L2], 0.0).astype(BF16)
            s["a_ak"] = jnp.where(mask_strict, aa[0:L2, L2:], 0.0).astype(BF16)
            s["a_r"] = jnp.concatenate([jnp.where(mask_incl, aa[L2:, 0:L2], 0.0),
                                        jnp.where(mask_incl, aa[L2:, L2:], 0.0)],
                                       axis=1).astype(BF16)
        for s in insts:
            s["x"] = jnp.concatenate(
                [s["at_true"], jnp.dot(s["a_ak"], s["v_s"], preferred_element_type=F32)],
                axis=1)
        for it in range(n_double):
            for s in insts:
                s["x"] = s["x"] + jnp.dot(s["nmat"], s["x"].astype(BF16),
                                          preferred_element_type=F32)
            if it + 1 < n_double:
                for s in insts:
                    s["nmat"] = jnp.dot(s["nmat"], s["nmat"],
                                        preferred_element_type=F32).astype(BF16)
        for s in insts:
            big_l = jnp.concatenate([s["a_r"], s["hat_t"]], axis=0)
            big_r = jnp.concatenate(
                [s["x"].astype(BF16),
                 jnp.concatenate([jnp.zeros((L2, LANES), BF16), s["v_s"]], axis=1)], axis=0)
            res = jnp.dot(big_l, big_r, preferred_element_type=F32)
            q_s = s["rt_true"] + res[0:L2, :LANES]
            y0_s = res[0:L2, LANES:]
            qs_ref[s["sl"], s["cs"]] = q_s[0:L] + q_s[L:]
            y_ref[s["sl"], s["cs"]] = y0_s[0:L] + y0_s[L:]
            ac_ref[s["idx"]] = (jnp.where(eye2, jnp.broadcast_to(s["p_last"], (LANES, LANES)), 0.0)
                                + res[L2:, :LANES])
            gc_ref[s["idx"]] = res[L2:, LANES:]
        return carry

    lax.fori_loop(0, nchunk // RW_GROUP, coeff_body, 0)

    def chain_body(c, carry):
        sl = pl.ds(pl.multiple_of(c * L, L), L)
        for p in range(npair):
            cs = slice(p * LANES, (p + 1) * LANES)
            idx = p * nchunk + c
            mp = m_ref[p]
            y_ref[sl, cs] = _mm(qs_ref[sl, cs], mp) + y_ref[sl, cs]
            m_ref[p] = _mm(ac_ref[idx], mp, 3) + gc_ref[idx]
        return carry

    lax.fori_loop(0, nchunk, chain_body, 0)

    e_b = e_ref[...]
    inv_n = 1.0 / RW_HEAD_DIM

    def head_sum(x):
        hi, lo = _split_bf16(x)
        return (jnp.dot(hi, e_b, preferred_element_type=F32)
                + jnp.dot(lo, e_b, preferred_element_type=F32))

    for p in range(npair):
        cs = slice(p * LANES, (p + 1) * LANES)
        y = y_ref[:, cs]
        mu = head_sum(y) * inv_n
        d = y - mu
        var = head_sum(d * d) * inv_n
        yn = d * lax.rsqrt(var + RW_GN_EPS) * g_ref[:, cs] + bias_ref[:, cs]
        v_all = v_ref[:, cs]
        bonus = head_sum(r_ref[:, cs] * kh_ref[:, cs] * rk_ref[:, cs])
        out_ref[:, cs] = (_silu(z_ref[:, cs]) * (yn + bonus * v_all)).astype(BF16)


def _rwkv_scan(r, ld, kh, v, kn, b, proj, r_k3, ln_g3, ln_b3, e_pair, layer, batch, seq):
    m = r.shape[0]
    tt = min(RW_TT, seq)
    nt = seq // tt
    chunk = min(RW_L, tt)
    npair = RW_WIDTH // LANES
    w = RW_WIDTH
    blk = pl.BlockSpec((tt, w), lambda bb, t: (bb * nt + t, 0))
    par = pl.BlockSpec((None, 1, w), lambda bb, t: (layer, 0, 0))
    return pl.pallas_call(
        functools.partial(_rwkv_scan_kernel, tt=tt, chunk=chunk),
        grid=(batch, nt),
        in_specs=[blk] * 6 + [
            pl.BlockSpec((tt, w), lambda bb, t: (bb * nt + t, OFF_RWZ // w)),
            par, par, par,
            pl.BlockSpec((LANES, LANES), lambda bb, t: (0, 0)),
        ],
        out_specs=blk,
        out_shape=jax.ShapeDtypeStruct((m, w), BF16),
        scratch_shapes=[
            pltpu.VMEM((npair, LANES, LANES), F32),
            pltpu.VMEM((npair * (tt // chunk), LANES, LANES), F32),
            pltpu.VMEM((npair * (tt // chunk), LANES, LANES), F32),
            pltpu.VMEM((tt, w), F32),
            pltpu.VMEM((tt, w), F32),
        ],
        compiler_params=_params("parallel", "arbitrary"),
        name="rwkv_scan",
    )(r, ld, kh, v, kn, b, proj, r_k3, ln_g3, ln_b3, e_pair)


def _permute_w_in(w_in):
    def s(a, n):
        return w_in[:, :, a:a + n]

    pad = jnp.zeros(w_in.shape[:2] + (LANES - MLA_ROPE - 2 * ML_HEADS,), w_in.dtype)
    parts = [s(0, 1024), s(1024, 512), s(1544, 512), s(2056, 512), s(2568, 512), s(3080, 256),
             s(3336, 64), s(1536, 8), pad, s(5960, 128), s(4424, 512), s(4936, 512), s(5448, 512),
             s(6088, 512), s(3400, 1024)]
    return jnp.concatenate([p.astype(BF16) for p in parts], axis=-1)


def _permute_w_uq(w_uq):
    d = w_uq.shape[0]
    w = w_uq.reshape(d, MLA_Q_RANK, MLA_HEADS, MLA_NOPE + MLA_ROPE)
    w = jnp.pad(w, ((0, 0), (0, 0), (0, 0), (0, MLA_QK_PAD - MLA_NOPE - MLA_ROPE)))
    return w.reshape(d, MLA_Q_RANK, MLA_HEADS * MLA_QK_PAD).astype(BF16)


def _split_w_ukv(w_ukv):
    d = w_ukv.shape[0]
    w = w_ukv.reshape(d, MLA_KV_RANK, MLA_HEADS, MLA_NOPE + MLA_V)
    wk = w[..., :MLA_NOPE].reshape(d, MLA_KV_RANK, MLA_HEADS * MLA_NOPE)
    wv = w[..., MLA_NOPE:].reshape(d, MLA_KV_RANK, MLA_HEADS * MLA_V)
    return wk.astype(BF16), wv.astype(BF16)


def _rope_tables(positions):
    inv_freq = jnp.power(ROPE_BASE, -jnp.arange(0, MLA_ROPE, 2, dtype=F32) / MLA_ROPE)
    ang = positions.astype(F32).reshape(-1, 1) * inv_freq
    cos, sin = jnp.cos(ang), jnp.sin(ang)
    z32 = jnp.zeros_like(cos)
    z64 = jnp.zeros((ang.shape[0], LANES - MLA_ROPE), F32)
    cosf = jnp.concatenate([cos, cos, z64], axis=-1)
    s1 = jnp.concatenate([-sin, z32, z64], axis=-1)
    s2 = jnp.concatenate([z32, sin, z64], axis=-1)
    return cosf, s1, s2


def kernel(x, positions, norm_g, w_in, ml_conv_w, ml_conv_b, ml_i_bias, ml_f_bias, ml_norm_g,
           mla_q_norm_g, mla_w_uq, mla_kv_norm_g, mla_w_ukv, rw_mu, rw_w0, rw_w2, rw_a0, rw_a2,
           rw_v0, rw_v1, rw_v2, rw_k_k, rw_k_a, rw_r_k, rw_ln_g, rw_ln_b, w_out, final_norm_g):
    batch, seq, _ = x.shape
    depth = w_in.shape[0]
    m = batch * seq
    x2 = x.reshape(m, D_MODEL)

    w_in_p = _permute_w_in(w_in)
    w_uq_p = _permute_w_uq(mla_w_uq)
    w_uk_p, w_uv_p = _split_w_ukv(mla_w_ukv)
    w_out_b = w_out.astype(BF16)
    cosf, s1, s2 = _rope_tables(positions)

    def row3(a):
        return a.reshape(a.shape[0], 1, a.shape[1])

    norm_g3 = row3(norm_g)
    conv_b3 = row3(ml_conv_b)
    ml_norm_g3 = row3(ml_norm_g)
    qg3, kvg3 = row3(mla_q_norm_g), row3(mla_kv_norm_g)
    w = RW_WIDTH
    mu_r, mu_k, mu_v = row3(rw_mu[:, 0:w]), row3(rw_mu[:, w:2 * w]), row3(rw_mu[:, 2 * w:3 * w])
    mu_wa = row3(rw_mu[:, 3 * w:])
    w0, a0, k_k, k_a = row3(rw_w0), row3(rw_a0), row3(rw_k_k), row3(rw_k_a)
    w2p = jnp.pad(rw_w2, ((0, 0), (0, LANES - RW_DECAY_RANK), (0, 0)))
    a2p = jnp.pad(rw_a2, ((0, 0), (LANES - RW_AAA_RANK, 0), (0, 0)))
    v0 = row3(rw_v0)
    v1p = jnp.pad(rw_v1, ((0, 0), (0, 0), (0, LANES - RW_MV_RANK)))
    v2p = jnp.pad(rw_v2, ((0, 0), (0, LANES - RW_MV_RANK), (0, 0)))
    r_k3 = rw_r_k.reshape(depth, 1, w)
    ln_g3, ln_b3 = row3(rw_ln_g), row3(rw_ln_b)
    hid = jnp.arange(w) // RW_HEAD_DIM
    e_head = (hid[:, None] == hid[None, :]).astype(F32)
    e_pair = e_head[:LANES, :LANES].astype(BF16)
    final_g2 = final_norm_g.reshape(1, D_MODEL)

    v_first = None
    for layer in range(depth):
        proj = _inproj(x2, norm_g3, w_in_p, layer)
        gates_t = proj[:, OFF_KRG + MLA_ROPE:OFF_KRG + MLA_ROPE + 2 * ML_HEADS]
        gates_t = gates_t.reshape(batch, seq, 2 * ML_HEADS).transpose(0, 2, 1)
        gates_t = gates_t.reshape(batch, 2 * ML_HEADS, 1, seq)
        y_ml = _mlstm(proj, gates_t, ml_i_bias, ml_f_bias, ml_conv_w, conv_b3, ml_norm_g3,
                      layer, batch, seq)
        q, k, v = _mla_proj(proj, cosf, s1, s2, qg3, w_uq_p, kvg3, w_uk_p, w_uv_p, layer)
        y_mla = _attention(q, k, v, proj, batch, seq)
        r_s, ld, kh, v_rw, kn, b_rw = _rwkv_pre(proj, v_first, mu_wa, mu_r, mu_k, mu_v, w0, w2p,
                                                 a0, a2p, k_k, k_a, e_head, v0, v1p, v2p,
                                                 layer, batch, seq)
        if layer == 0:
            v_first = v_rw
        y_rw = _rwkv_scan(r_s, ld, kh, v_rw, kn, b_rw, proj, r_k3, ln_g3, ln_b3, e_pair,
                          layer, batch, seq)
        x2 = _outproj(x2, y_ml, y_mla, y_rw, w_out_b, final_g2, layer, layer == depth - 1)
    return x2.reshape(batch, seq, D_MODEL)
```

```python
import functools

import jax
import jax.numpy as jnp
from jax import lax
from jax.experimental import pallas as pl
from jax.experimental.pallas import tpu as pltpu

F32 = jnp.float32
BF16 = jnp.bfloat16

D_MODEL = 2048
NORM_EPS = 1e-6
ML_HEADS = 4
ML_HEAD_DIM = 128
ML_WIDTH = 512
ML_CONV = 4
MLA_HEADS = 8
MLA_NOPE = 128
MLA_ROPE = 64
MLA_V = 128
MLA_WIDTH = 1024
MLA_Q_RANK = 512
MLA_KV_RANK = 256
ROPE_BASE = 10000.0
RW_HEAD_DIM = 64
RW_WIDTH = 512
RW_HEADS = 8
RW_DECAY_RANK = 64
RW_AAA_RANK = 64
RW_MV_RANK = 32
RW_GN_EPS = 64e-5

LANES = 128
SUBLANES = 8
MLA_QK_PAD = 256
VMEM_LIMIT = 48 * 1024 * 1024
NEG_BIG = -1e30
LOG2E = 1.4426950408889634

OFF_QK = 0
OFF_MLV = 1024
OFF_MLO = 1536
OFF_MLZ = 2048
OFF_CQ = 2560
OFF_CKV = 3072
OFF_KRG = 3328
OFF_WA = 3456
OFF_RWR = 3584
OFF_RWK = 4096
OFF_RWV = 4608
OFF_RWZ = 5120
OFF_MLAZ = 5632
D_IN_PAD = 6656

TM_PROJ = 1024
TN_PROJ = 1664
TM_OUT = 512
TQ = 512
ML_TT = 1024
ML_L = 256
RW_TM = 512
RW_TT = 512
RW_L = 64


def _split_bf16(x):
    hi = x.astype(BF16)
    lo = (x - hi.astype(F32)).astype(BF16)
    return hi, lo


def _mm(a, b, passes=1):
    if passes == 1:
        return jnp.dot(a.astype(BF16), b.astype(BF16), preferred_element_type=F32)
    ah, al = _split_bf16(a)
    bh, bl = _split_bf16(b)
    return (jnp.dot(ah, bh, preferred_element_type=F32)
            + jnp.dot(ah, bl, preferred_element_type=F32)
            + jnp.dot(al, bh, preferred_element_type=F32))


_NT = (((1,), (1,)), ((), ()))


def _mm_nt(a, b, passes=1):
    if passes == 1:
        return lax.dot_general(a.astype(BF16), b.astype(BF16), _NT, preferred_element_type=F32)
    ah, al = _split_bf16(a)
    bh, bl = _split_bf16(b)
    return (lax.dot_general(ah, bh, _NT, preferred_element_type=F32)
            + lax.dot_general(ah, bl, _NT, preferred_element_type=F32)
            + lax.dot_general(al, bh, _NT, preferred_element_type=F32))


def _sigmoid(x):
    return 1.0 / (1.0 + jnp.exp(-x))


def _silu(x):
    return x * _sigmoid(x)


def _log_sigmoid(x):
    return jnp.minimum(x, 0.0) - jnp.log(1.0 + jnp.exp(-jnp.abs(x)))


def _params(*sem):
    return pltpu.CompilerParams(dimension_semantics=sem, vmem_limit_bytes=VMEM_LIMIT)


def _rms_scale(x, g):
    return x * lax.rsqrt(jnp.mean(x * x, axis=-1, keepdims=True) + NORM_EPS) * g


def _rmsnorm_kernel(x_ref, g_ref, h_ref):
    h_ref[...] = _rms_scale(x_ref[...], g_ref[...]).astype(BF16)


def _rmsnorm(x2, norm_g3, layer):
    m = x2.shape[0]
    return pl.pallas_call(
        _rmsnorm_kernel,
        grid=(m // TM_OUT,),
        in_specs=[pl.BlockSpec((TM_OUT, D_MODEL), lambda i: (i, 0)),
                  pl.BlockSpec((None, 1, D_MODEL), lambda i: (layer, 0, 0))],
        out_specs=pl.BlockSpec((TM_OUT, D_MODEL), lambda i: (i, 0)),
        out_shape=jax.ShapeDtypeStruct((m, D_MODEL), BF16),
        compiler_params=_params("parallel"),
        name="rmsnorm",
    )(x2, norm_g3)


def _inproj_kernel(h_ref, w_ref, o_ref):
    o_ref[...] = jnp.dot(h_ref[...], w_ref[...], preferred_element_type=F32)


def _inproj(h, w_in_p, layer):
    m = h.shape[0]
    return pl.pallas_call(
        _inproj_kernel,
        grid=(m // TM_PROJ, D_IN_PAD // TN_PROJ),
        in_specs=[
            pl.BlockSpec((TM_PROJ, D_MODEL), lambda i, j: (i, 0)),
            pl.BlockSpec((None, D_MODEL, TN_PROJ), lambda i, j: (layer, 0, j)),
        ],
        out_specs=pl.BlockSpec((TM_PROJ, TN_PROJ), lambda i, j: (i, j)),
        out_shape=jax.ShapeDtypeStruct((m, D_IN_PAD), F32),
        compiler_params=_params("parallel", "arbitrary"),
        name="inproj",
    )(h, w_in_p)


def _outproj_kernel(x_ref, yml_ref, ymla_ref, yrw_ref, w_ref, g_ref, *out_refs, final):
    acc = x_ref[...]
    acc = acc + jnp.dot(yml_ref[...], w_ref[0:ML_WIDTH, :], preferred_element_type=F32)
    acc = acc + jnp.dot(ymla_ref[...], w_ref[ML_WIDTH:ML_WIDTH + MLA_WIDTH, :],
                        preferred_element_type=F32)
    acc = acc + jnp.dot(yrw_ref[...], w_ref[ML_WIDTH + MLA_WIDTH:, :], preferred_element_type=F32)
    normed = _rms_scale(acc, g_ref[...])
    if final:
        out_refs[0][...] = normed
    else:
        out_refs[0][...] = acc
        out_refs[1][...] = normed.astype(BF16)


def _outproj(x2, y_ml, y_mla, y_rw, w_out_b, gains3, layer, final):
    m = x2.shape[0]
    g_index = 0 if final else layer + 1
    row = pl.BlockSpec((TM_OUT, D_MODEL), lambda i: (i, 0))
    out_specs = [row] if final else [row, row]
    out_shape = [jax.ShapeDtypeStruct((m, D_MODEL), F32)]
    if not final:
        out_shape.append(jax.ShapeDtypeStruct((m, D_MODEL), BF16))
    return pl.pallas_call(
        functools.partial(_outproj_kernel, final=final),
        grid=(m // TM_OUT,),
        in_specs=[
            row,
            pl.BlockSpec((TM_OUT, ML_WIDTH), lambda i: (i, 0)),
            pl.BlockSpec((TM_OUT, MLA_WIDTH), lambda i: (i, 0)),
            pl.BlockSpec((TM_OUT, RW_WIDTH), lambda i: (i, 0)),
            pl.BlockSpec((None, D_MODEL, D_MODEL), lambda i: (layer, 0, 0)),
            pl.BlockSpec((None, 1, D_MODEL), lambda i: (g_index, 0, 0)),
        ],
        out_specs=out_specs,
        out_shape=out_shape,
        compiler_params=_params("parallel"),
        name="outproj",
    )(x2, y_ml, y_mla, y_rw, w_out_b, gains3)


def _mla_proj_kernel(cq_ref, ckv_ref, krg_ref, cos_ref, s1_ref, s2_ref, qg_ref, wq_ref, kvg_ref,
                     wk_ref, wv_ref, q_out, k_out, v_out):
    cq = cq_ref[...]
    cqn = (cq * lax.rsqrt(jnp.mean(cq * cq, axis=-1, keepdims=True) + NORM_EPS)
           * qg_ref[...]).astype(BF16)
    q = jnp.dot(cqn, wq_ref[...], preferred_element_type=F32)
    ckv = ckv_ref[...]
    ckvn = (ckv * lax.rsqrt(jnp.mean(ckv * ckv, axis=-1, keepdims=True) + NORM_EPS)
            * kvg_ref[...]).astype(BF16)
    kn = jnp.dot(ckvn, wk_ref[...], preferred_element_type=F32)
    v = jnp.dot(ckvn, wv_ref[...], preferred_element_type=F32)
    cosf, s1, s2 = cos_ref[...], s1_ref[...], s2_ref[...]

    def rope(xb):
        return (xb * cosf + pltpu.roll(xb, LANES - MLA_ROPE // 2, 1) * s1
                + pltpu.roll(xb, MLA_ROPE // 2, 1) * s2)

    scale = (MLA_NOPE + MLA_ROPE) ** -0.5 * LOG2E
    kr = rope(krg_ref[...]).astype(BF16)
    for h in range(MLA_HEADS):
        c0 = h * MLA_QK_PAD
        q_out[:, c0:c0 + MLA_NOPE] = (q[:, c0:c0 + MLA_NOPE] * scale).astype(BF16)
        q_out[:, c0 + MLA_NOPE:c0 + MLA_QK_PAD] = (
            rope(q[:, c0 + MLA_NOPE:c0 + MLA_QK_PAD]) * scale).astype(BF16)
        k_out[:, c0:c0 + MLA_NOPE] = kn[:, h * MLA_NOPE:(h + 1) * MLA_NOPE].astype(BF16)
        k_out[:, c0 + MLA_NOPE:c0 + MLA_QK_PAD] = kr
    v_out[...] = v.T.astype(BF16)


def _mla_proj(proj, cosf, s1, s2, qg3, wq_p, kvg3, wk_p, wv_p, layer):
    m = proj.shape[0]
    tm = TQ
    row = lambda i: (i, 0)
    return pl.pallas_call(
        _mla_proj_kernel,
        grid=(m // tm,),
        in_specs=[
            pl.BlockSpec((tm, MLA_Q_RANK), lambda i: (i, OFF_CQ // MLA_Q_RANK)),
            pl.BlockSpec((tm, MLA_KV_RANK), lambda i: (i, OFF_CKV // MLA_KV_RANK)),
            pl.BlockSpec((tm, LANES), lambda i: (i, OFF_KRG // LANES)),
            pl.BlockSpec((tm, LANES), row),
            pl.BlockSpec((tm, LANES), row),
            pl.BlockSpec((tm, LANES), row),
            pl.BlockSpec((None, 1, MLA_Q_RANK), lambda i: (layer, 0, 0)),
            pl.BlockSpec((None, MLA_Q_RANK, MLA_HEADS * MLA_QK_PAD), lambda i: (layer, 0, 0)),
            pl.BlockSpec((None, 1, MLA_KV_RANK), lambda i: (layer, 0, 0)),
            pl.BlockSpec((None, MLA_KV_RANK, MLA_HEADS * MLA_NOPE), lambda i: (layer, 0, 0)),
            pl.BlockSpec((None, MLA_KV_RANK, MLA_HEADS * MLA_V), lambda i: (layer, 0, 0)),
        ],
        out_specs=[
            pl.BlockSpec((tm, MLA_HEADS * MLA_QK_PAD), row),
            pl.BlockSpec((tm, MLA_HEADS * MLA_QK_PAD), row),
            pl.BlockSpec((None, MLA_HEADS * MLA_V, tm), lambda i: (i, 0, 0)),
        ],
        out_shape=[
            jax.ShapeDtypeStruct((m, MLA_HEADS * MLA_QK_PAD), BF16),
            jax.ShapeDtypeStruct((m, MLA_HEADS * MLA_QK_PAD), BF16),
            jax.ShapeDtypeStruct((m // tm, MLA_HEADS * MLA_V, tm), BF16),
        ],
        compiler_params=_params("parallel"),
        name="mla_proj",
    )(proj, proj, proj, cosf, s1, s2, qg3, wq_p, kvg3, wk_p, wv_p)


ATTN_HEADS_PER_STEP = 4


def _attn_kernel(q_ref, k_ref, vt_ref, z_ref, o_ref, *, tq):
    i = pl.program_id(2)
    nh = ATTN_HEADS_PER_STEP
    qs = [q_ref[:, h * MLA_QK_PAD:(h + 1) * MLA_QK_PAD] for h in range(nh)]
    kidx = lax.broadcasted_iota(jnp.int32, (tq, tq), 0)
    qidx = lax.broadcasted_iota(jnp.int32, (tq, tq), 1)

    def step(j, carry, diagonal):
        off = pl.multiple_of(j * tq, tq)
        ss = []
        for h in range(nh):
            kb = k_ref[pl.ds(off, tq), h * MLA_QK_PAD:(h + 1) * MLA_QK_PAD]
            s = lax.dot_general(kb, qs[h], _NT, preferred_element_type=F32)
            ss.append(jnp.where(kidx <= qidx, s, NEG_BIG) if diagonal else s)
        out = []
        for h in range(nh):
            m, l, acc = carry[h]
            s = ss[h]
            m_new = jnp.maximum(m, jnp.max(s, axis=0, keepdims=True))
            alpha = jnp.exp2(m - m_new)
            p = jnp.exp2(s - m_new)
            l = alpha * l + jnp.sum(p, axis=0, keepdims=True)
            acc = alpha * acc + jnp.dot(vt_ref[j, h], p.astype(BF16), preferred_element_type=F32)
            out.append((m_new, l, acc))
        return tuple(out)

    init = tuple((jnp.full((1, tq), NEG_BIG, F32), jnp.zeros((1, tq), F32),
                  jnp.zeros((MLA_V, tq), F32)) for _ in range(nh))
    carry = lax.fori_loop(0, i, lambda j, c: step(j, c, False), init)
    carry = step(i, carry, True)
    for h in range(nh):
        _, l, acc = carry[h]
        y = (acc / l).T
        cs = slice(h * MLA_V, (h + 1) * MLA_V)
        o_ref[:, cs] = (_silu(z_ref[:, cs]) * y).astype(BF16)


def _attention(q, k, v_t, proj, batch, seq):
    m = q.shape[0]
    nq = seq // TQ
    nh = ATTN_HEADS_PER_STEP
    v_t5 = v_t.reshape(batch, nq, MLA_HEADS, MLA_V, TQ)
    return pl.pallas_call(
        functools.partial(_attn_kernel, tq=TQ),
        grid=(batch, MLA_HEADS // nh, nq),
        in_specs=[
            pl.BlockSpec((TQ, nh * MLA_QK_PAD), lambda b, h, i: (b * nq + i, h)),
            pl.BlockSpec((seq, nh * MLA_QK_PAD), lambda b, h, i: (b, h)),
            pl.BlockSpec((None, nq, nh, MLA_V, TQ), lambda b, h, i: (b, 0, h, 0, 0)),
            pl.BlockSpec((TQ, nh * MLA_V),
                         lambda b, h, i: (b * nq + i, OFF_MLAZ // (nh * MLA_V) + h)),
        ],
        out_specs=pl.BlockSpec((TQ, nh * MLA_V), lambda b, h, i: (b * nq + i, h)),
        out_shape=jax.ShapeDtypeStruct((m, MLA_WIDTH), BF16),
        compiler_params=_params("parallel", "parallel", "arbitrary"),
        name="mla_attn",
    )(q, k, v_t5, proj)


def _shift_rows(x, prev8, s):
    rolled = pltpu.roll(x, s, 0)
    prev_rolled = pltpu.roll(prev8, s, 0)
    rid = lax.broadcasted_iota(jnp.int32, (SUBLANES, x.shape[1]), 0)
    top = jnp.where(rid < s, prev_rolled, rolled[0:SUBLANES])
    return jnp.concatenate([top, rolled[SUBLANES:]], axis=0)


def _mlstm_kernel(ib_ref, fb_ref, q_ref, k_ref, v_ref, o_ref, z_ref, g_ref, w_ref, b_ref, ng_ref,
                  out_ref, ct_ref, n_ref, m_ref, qp_ref, kp_ref, *, layer, tt, chunk):
    @pl.when(pl.program_id(1) == 0)
    def _():
        for ref in (ct_ref, n_ref, m_ref, qp_ref, kp_ref):
            ref[...] = jnp.zeros_like(ref)

    hd = ML_HEAD_DIM
    w_all = w_ref[...]
    b_all = b_ref[...]

    def conv_silu(x_ref, prev_ref, w, b):
        x = x_ref[...]
        prev8 = prev_ref[...]
        y = b + w[ML_CONV - 1:ML_CONV] * x
        for s in range(1, ML_CONV):
            y = y + w[ML_CONV - 1 - s:ML_CONV - s] * _shift_rows(x, prev8, s)
        prev_ref[...] = x[tt - SUBLANES:tt]
        return _silu(y)

    q_all = conv_silu(q_ref, qp_ref, w_all[:, :ML_WIDTH], b_all[:, :ML_WIDTH]) * (hd ** -0.5)
    k_all = conv_silu(k_ref, kp_ref, w_all[:, ML_WIDTH:], b_all[:, ML_WIDTH:])
    li_rows = [g_ref[h] + ib_ref[layer, h] for h in range(ML_HEADS)]
    lf_rows = [_log_sigmoid(g_ref[ML_HEADS + h] + fb_ref[layer, h]) for h in range(ML_HEADS)]

    L = chunk
    rows = lax.broadcasted_iota(jnp.int32, (L, L), 0)
    cols = lax.broadcasted_iota(jnp.int32, (L, L), 1)
    tri = rows >= cols
    eye = rows == cols
    for c in range(tt // L):
        sl = slice(c * L, (c + 1) * L)
        st = []
        for h in range(ML_HEADS):
            cs = slice(h * hd, (h + 1) * hd)
            li_row = li_rows[h][:, sl]
            lf_b = jnp.broadcast_to(lf_rows[h][:, sl], (L, L))
            li_b = jnp.broadcast_to(li_row, (L, L))
            b_col = jnp.sum(jnp.where(tri, lf_b, 0.0), axis=-1, keepdims=True)
            lf_col = jnp.sum(jnp.where(eye, lf_b, 0.0), axis=-1, keepdims=True)
            li_col = jnp.sum(jnp.where(eye, li_b, 0.0), axis=-1, keepdims=True)
            b_row = jnp.sum(jnp.where(rows <= cols, jnp.broadcast_to(lf_col, (L, L)), 0.0),
                            axis=0, keepdims=True)
            m_prev = m_ref[h][:, 0:1]
            dmat = jnp.where(tri, b_col - b_row + li_row, NEG_BIG)
            m_inter = b_col + m_prev
            m_t = jnp.maximum(m_inter, jnp.max(dmat, axis=-1, keepdims=True))
            st.append(dict(cs=cs, qc=q_all[sl, cs].astype(BF16), kc=k_all[sl, cs],
                           vc=v_ref[sl, cs].astype(BF16), pm=jnp.exp(dmat - m_t),
                           inter=jnp.exp(m_inter - m_t), m_t=m_t, b_col=b_col, li_col=li_col,
                           m_prev=m_prev))
        for s in st:
            s["smat"] = lax.dot_general(s["qc"], s["kc"].astype(BF16), _NT,
                                        preferred_element_type=F32) * s["pm"]
        for h, s in enumerate(st):
            ct = ct_ref[h]
            n_row = n_ref[h]
            num = (s["inter"] * jnp.dot(s["qc"], ct.astype(BF16), preferred_element_type=F32)
                   + jnp.dot(s["smat"].astype(BF16), s["vc"], preferred_element_type=F32))
            den = (s["inter"] * jnp.sum(s["qc"].astype(F32) * n_row, axis=-1, keepdims=True)
                   + jnp.sum(s["smat"], axis=-1, keepdims=True))
            s["hh"] = num / jnp.maximum(jnp.abs(den), jnp.exp(-s["m_t"]))
            b_last = s["b_col"][L - 1:L, :]
            g_col = b_last - s["b_col"] + s["li_col"]
            m_new = jnp.maximum(b_last + s["m_prev"], jnp.max(g_col, axis=0, keepdims=True))
            decay = jnp.exp(b_last + s["m_prev"] - m_new)
            kw = s["kc"] * jnp.exp(g_col - m_new)
            ct_ref[h] = decay * ct + jnp.dot(kw.T.astype(BF16), s["vc"],
                                             preferred_element_type=F32)
            n_ref[h] = decay * n_row + jnp.sum(kw, axis=0, keepdims=True)
            m_ref[h] = jnp.broadcast_to(m_new, (1, hd))
        for s in st:
            cs = s["cs"]
            hh = s["hh"]
            mu = jnp.mean(hh, axis=-1, keepdims=True)
            dd = hh - mu
            var = jnp.mean(dd * dd, axis=-1, keepdims=True)
            yn = dd * lax.rsqrt(var + NORM_EPS) * ng_ref[:, cs]
            out_ref[sl, cs] = (_silu(z_ref[sl, cs]) * (_sigmoid(o_ref[sl, cs]) * yn)).astype(BF16)


def _mlstm(proj, gates_t, i_bias, f_bias, conv_w, conv_b3, norm_g3, layer, batch, seq):
    m = proj.shape[0]
    tt = min(ML_TT, seq)
    nt = seq // tt
    hd = ML_HEAD_DIM
    w = ML_WIDTH

    def col(off):
        return pl.BlockSpec((tt, w), lambda b, t: (b * nt + t, off // w))

    smem = pl.BlockSpec(memory_space=pltpu.SMEM)
    return pl.pallas_call(
        functools.partial(_mlstm_kernel, layer=layer, tt=tt, chunk=min(ML_L, tt)),
        grid=(batch, nt),
        in_specs=[
            smem, smem,
            col(OFF_QK), col(OFF_QK + w), col(OFF_MLV), col(OFF_MLO), col(OFF_MLZ),
            pl.BlockSpec((None, 2 * ML_HEADS, 1, tt), lambda b, t: (b, 0, 0, t)),
            pl.BlockSpec((None, ML_CONV, 2 * w), lambda b, t: (layer, 0, 0)),
            pl.BlockSpec((None, 1, 2 * w), lambda b, t: (layer, 0, 0)),
            pl.BlockSpec((None, 1, w), lambda b, t: (layer, 0, 0)),
        ],
        out_specs=pl.BlockSpec((tt, w), lambda b, t: (b * nt + t, 0)),
        out_shape=jax.ShapeDtypeStruct((m, w), BF16),
        scratch_shapes=[
            pltpu.VMEM((ML_HEADS, hd, hd), F32),
            pltpu.VMEM((ML_HEADS, 1, hd), F32),
            pltpu.VMEM((ML_HEADS, 1, hd), F32),
            pltpu.VMEM((SUBLANES, w), F32),
            pltpu.VMEM((SUBLANES, w), F32),
        ],
        compiler_params=_params("parallel", "arbitrary"),
        name="mlstm",
    )(i_bias, f_bias, proj, proj, proj, proj, proj, gates_t, conv_w, conv_b3, norm_g3)


RW_PRE_PASSES = 3


def _rwkv_pre_kernel(*refs, has_vres, tm):
    if has_vres:
        (wa_ref, r_ref, k_ref, v_ref, mu_wa, mu_r, mu_k, mu_v, w0_ref, w2_ref, a0_ref, a2_ref,
         kk_ref, ka_ref, e_ref, vf_ref, v0_ref, v1_ref, v2_ref,
         r_out, ld_out, kh_out, v_out, kn_out, b_out, c_wa, c_r, c_k, c_v) = refs
    else:
        (wa_ref, r_ref, k_ref, v_ref, mu_wa, mu_r, mu_k, mu_v, w0_ref, w2_ref, a0_ref, a2_ref,
         kk_ref, ka_ref, e_ref,
         r_out, ld_out, kh_out, v_out, kn_out, b_out, c_wa, c_r, c_k, c_v) = refs

    @pl.when(pl.program_id(1) == 0)
    def _():
        for c in (c_wa, c_r, c_k, c_v):
            c[...] = jnp.zeros_like(c)

    def mix(x_ref, c_ref, mu_ref):
        x = x_ref[...]
        rolled = pltpu.roll(x, 1, 0)
        rid = lax.broadcasted_iota(jnp.int32, (SUBLANES, x.shape[1]), 0)
        prev_last = jnp.broadcast_to(c_ref[SUBLANES - 1:SUBLANES, :], (SUBLANES, x.shape[1]))
        top = jnp.where(rid == 0, prev_last, rolled[0:SUBLANES])
        xprev = jnp.concatenate([top, rolled[SUBLANES:]], axis=0)
        c_ref[...] = x[tm - SUBLANES:tm]
        return x + mu_ref[...] * (xprev - x)

    xwa = mix(wa_ref, c_wa, mu_wa)
    r = mix(r_ref, c_r, mu_r)
    k = mix(k_ref, c_k, mu_k)
    v = mix(v_ref, c_v, mu_v)
    zw = w0_ref[...] + _mm(jnp.tanh(xwa), w2_ref[...], RW_PRE_PASSES)
    za = a0_ref[...] + _mm(xwa, a2_ref[...], RW_PRE_PASSES)
    log_w = _log_sigmoid(zw) - 0.5
    ld_out[...] = -jnp.exp(log_w)
    a = _sigmoid(za)
    if has_vres:
        gate = _sigmoid(v0_ref[...] + _mm(_mm(v, v1_ref[...], RW_PRE_PASSES), v2_ref[...],
                                          RW_PRE_PASSES))
        v = v + (vf_ref[...] - v) * gate
    kk = k * kk_ref[...]
    ss = _mm(kk * kk, e_ref[...], RW_PRE_PASSES)
    kn = kk / jnp.maximum(jnp.sqrt(ss), 1e-12)
    r_out[...] = r
    kh_out[...] = k * (1.0 + (a - 1.0) * ka_ref[...])
    v_out[...] = v
    kn_out[...] = kn
    b_out[...] = kn * a


def _rwkv_pre(proj, v_first, mu_wa, mu_r, mu_k, mu_v, w0, w2p, a0, a2p, k_k, k_a, e_head,
              v0, v1p, v2p, layer, batch, seq):
    m = proj.shape[0]
    tm = min(RW_TM, seq)
    nt = seq // tm
    has_vres = layer > 0
    w = RW_WIDTH

    def colblk(off, width):
        return pl.BlockSpec((tm, width), lambda b, t: (b * nt + t, off // width))

    def lay(shape):
        return pl.BlockSpec((None,) + shape, lambda b, t: (layer,) + (0,) * len(shape))

    def lay1(shape):
        return pl.BlockSpec((None,) + shape, lambda b, t: (layer - 1,) + (0,) * len(shape))

    in_specs = [colblk(OFF_WA, LANES), colblk(OFF_RWR, w), colblk(OFF_RWK, w), colblk(OFF_RWV, w),
                lay((1, LANES)), lay((1, w)), lay((1, w)), lay((1, w)),
                lay((1, w)), lay((LANES, w)), lay((1, w)), lay((LANES, w)),
                lay((1, w)), lay((1, w)),
                pl.BlockSpec((w, w), lambda b, t: (0, 0))]
    args = [proj, proj, proj, proj, mu_wa, mu_r, mu_k, mu_v, w0, w2p, a0, a2p, k_k, k_a, e_head]
    if has_vres:
        in_specs += [pl.BlockSpec((tm, w), lambda b, t: (b * nt + t, 0)),
                     lay1((1, w)), lay1((w, LANES)), lay1((LANES, w))]
        args += [v_first, v0, v1p, v2p]
    row = pl.BlockSpec((tm, w), lambda b, t: (b * nt + t, 0))
    return pl.pallas_call(
        functools.partial(_rwkv_pre_kernel, has_vres=has_vres, tm=tm),
        grid=(batch, nt),
        in_specs=in_specs,
        out_specs=[row] * 6,
        out_shape=[jax.ShapeDtypeStruct((m, w), F32)] * 6,
        scratch_shapes=[pltpu.VMEM((SUBLANES, LANES), F32), pltpu.VMEM((SUBLANES, w), F32),
                        pltpu.VMEM((SUBLANES, w), F32), pltpu.VMEM((SUBLANES, w), F32)],
        compiler_params=_params("parallel", "arbitrary"),
        name="rwkv_pre",
    )(*args)


RW_GROUP = 2


def _rwkv_scan_kernel(r_ref, ld_ref, kh_ref, v_ref, kn_ref, b_ref, z_ref, rk_ref, g_ref, bias_ref,
                      e_ref, out_ref, m_ref, ac_ref, gc_ref, qs_ref, y_ref, *, tt, chunk):
    @pl.when(pl.program_id(1) == 0)
    def _():
        m_ref[...] = jnp.zeros_like(m_ref)

    L = chunk
    L2 = 2 * L
    nchunk = tt // L
    npair = RW_WIDTH // LANES
    rows = lax.broadcasted_iota(jnp.int32, (L, L), 0)
    cols = lax.broadcasted_iota(jnp.int32, (L, L), 1)
    tri_b = jnp.where(rows >= cols, 1.0, 0.0).astype(BF16)
    lane = lax.broadcasted_iota(jnp.int32, (1, LANES), 1)
    m0 = jnp.where(lane < RW_HEAD_DIM, 1.0, 0.0)
    m1 = 1.0 - m0
    r2 = lax.broadcasted_iota(jnp.int32, (L2, L2), 0)
    c2 = lax.broadcasted_iota(jnp.int32, (L2, L2), 1)
    same_blk = (r2 < L) == (c2 < L)
    t2 = jnp.where(r2 < L, r2, r2 - L)
    s2 = jnp.where(c2 < L, c2, c2 - L)
    mask_strict = jnp.logical_and(same_blk, t2 > s2)
    mask_incl = jnp.logical_and(same_blk, t2 >= s2)
    eye2 = lax.broadcasted_iota(jnp.int32, (LANES, LANES), 0) == lax.broadcasted_iota(
        jnp.int32, (LANES, LANES), 1)
    n_double = max(1, (L - 1).bit_length())

    def stack(x):
        return jnp.concatenate([x * m0, x * m1], axis=0)

    def prepare(p, c):
        sl = pl.ds(pl.multiple_of(c * L, L), L)
        cs = slice(p * LANES, (p + 1) * LANES)
        ld = ld_ref[sl, cs]
        r, kh, v, kn, b = r_ref[sl, cs], kh_ref[sl, cs], v_ref[sl, cs], kn_ref[sl, cs], b_ref[sl, cs]
        ld_hi, ld_lo = _split_bf16(ld)
        logp = (jnp.dot(tri_b, ld_hi, preferred_element_type=F32)
                + jnp.dot(tri_b, ld_lo, preferred_element_type=F32))
        cmid = logp[L // 2 - 1:L // 2, :]
        e = logp - cmid
        e_last = e[L - 1:L, :]
        p_mid = jnp.exp(cmid)
        at_s = stack(-kn * jnp.exp(e - ld))
        rt_s = stack(r * jnp.exp(e))
        inv = jnp.exp(-e)
        tail = jnp.exp(e_last - e)
        lhs = jnp.concatenate([at_s, rt_s], axis=0).astype(BF16)
        rhs = jnp.concatenate([stack(b * inv), stack(kh * inv)], axis=0).astype(BF16)
        hat_t = jnp.concatenate([stack(b * tail), stack(kh * tail)], axis=0).T.astype(BF16)
        return dict(sl=sl, cs=cs, idx=p * nchunk + c, lhs=lhs, rhs=rhs, hat_t=hat_t,
                    at_true=at_s * p_mid, rt_true=rt_s * p_mid, v_s=stack(v).astype(BF16),
                    p_last=jnp.exp(e_last + cmid))

    def coeff_body(cg, carry):
        insts = [prepare(p, cg * RW_GROUP + g) for p in range(npair) for g in range(RW_GROUP)]
        for s in insts:
            aa = lax.dot_general(s["lhs"], s["rhs"], _NT, preferred_element_type=F32)
            s["nmat"] = jnp.where(mask_strict, aa[0:L2, 0:L2], 0.0).astype(BF16)
            s["a_ak"] = jnp.where(mask_strict, aa[0:L2, L2:], 0.0).astype(BF16)
            s["a_r"] = jnp.concatenate([jnp.where(mask_incl, aa[L2:, 0:L2], 0.0),
                                        jnp.where(mask_incl, aa[L2:, L2:], 0.0)],
                                       axis=1).astype(BF16)
        for s in insts:
            s["x"] = jnp.concatenate(
                [s["at_true"], jnp.dot(s["a_ak"], s["v_s"], preferred_element_type=F32)],
                axis=1)
        for it in range(n_double):
            for s in insts:
                s["x"] = s["x"] + jnp.dot(s["nmat"], s["x"].astype(BF16),
                                          preferred_element_type=F32)
            if it + 1 < n_double:
                for s in insts:
                    s["nmat"] = jnp.dot(s["nmat"], s["nmat"],
                                        preferred_element_type=F32).astype(BF16)
        for s in insts:
            big_l = jnp.concatenate([s["a_r"], s["hat_t"]], axis=0)
            big_r = jnp.concatenate(
                [s["x"].astype(BF16),
                 jnp.concatenate([jnp.zeros((L2, LANES), BF16), s["v_s"]], axis=1)], axis=0)
            res = jnp.dot(big_l, big_r, preferred_element_type=F32)
            q_s = s["rt_true"] + res[0:L2, :LANES]
            y0_s = res[0:L2, LANES:]
            qs_ref[s["sl"], s["cs"]] = q_s[0:L] + q_s[L:]
            y_ref[s["sl"], s["cs"]] = y0_s[0:L] + y0_s[L:]
            ac_ref[s["idx"]] = (jnp.where(eye2, jnp.broadcast_to(s["p_last"], (LANES, LANES)), 0.0)
                                + res[L2:, :LANES])
            gc_ref[s["idx"]] = res[L2:, LANES:]
        return carry

    lax.fori_loop(0, nchunk // RW_GROUP, coeff_body, 0)

    def chain_body(c, carry):
        sl = pl.ds(pl.multiple_of(c * L, L), L)
        for p in range(npair):
            cs = slice(p * LANES, (p + 1) * LANES)
            idx = p * nchunk + c
            mp = m_ref[p]
            y_ref[sl, cs] = _mm(qs_ref[sl, cs], mp) + y_ref[sl, cs]
            m_ref[p] = _mm(ac_ref[idx], mp, 3) + gc_ref[idx]
        return carry

    lax.fori_loop(0, nchunk, chain_body, 0)

    e_b = e_ref[...]
    inv_n = 1.0 / RW_HEAD_DIM

    def head_sum(x):
        hi, lo = _split_bf16(x)
        return (jnp.dot(hi, e_b, preferred_element_type=F32)
                + jnp.dot(lo, e_b, preferred_element_type=F32))

    for p in range(npair):
        cs = slice(p * LANES, (p + 1) * LANES)
        y = y_ref[:, cs]
        mu = head_sum(y) * inv_n
        d = y - mu
        var = head_sum(d * d) * inv_n
        yn = d * lax.rsqrt(var + RW_GN_EPS) * g_ref[:, cs] + bias_ref[:, cs]
        v_all = v_ref[:, cs]
        bonus = head_sum(r_ref[:, cs] * kh_ref[:, cs] * rk_ref[:, cs])
        out_ref[:, cs] = (_silu(z_ref[:, cs]) * (yn + bonus * v_all)).astype(BF16)


def _rwkv_scan(r, ld, kh, v, kn, b, proj, r_k3, ln_g3, ln_b3, e_pair, layer, batch, seq):
    m = r.shape[0]
    tt = min(RW_TT, seq)
    nt = seq // tt
    chunk = min(RW_L, tt)
    npair = RW_WIDTH // LANES
    w = RW_WIDTH
    blk = pl.BlockSpec((tt, w), lambda bb, t: (bb * nt + t, 0))
    par = pl.BlockSpec((None, 1, w), lambda bb, t: (layer, 0, 0))
    return pl.pallas_call(
        functools.partial(_rwkv_scan_kernel, tt=tt, chunk=chunk),
        grid=(batch, nt),
        in_specs=[blk] * 6 + [
            pl.BlockSpec((tt, w), lambda bb, t: (bb * nt + t, OFF_RWZ // w)),
            par, par, par,
            pl.BlockSpec((LANES, LANES), lambda bb, t: (0, 0)),
        ],
        out_specs=blk,
        out_shape=jax.ShapeDtypeStruct((m, w), BF16),
        scratch_shapes=[
            pltpu.VMEM((npair, LANES, LANES), F32),
            pltpu.VMEM((npair * (tt // chunk), LANES, LANES), F32),
            pltpu.VMEM((npair * (tt // chunk), LANES, LANES), F32),
            pltpu.VMEM((tt, w), F32),
            pltpu.VMEM((tt, w), F32),
        ],
        compiler_params=_params("parallel", "arbitrary"),
        name="rwkv_scan",
    )(r, ld, kh, v, kn, b, proj, r_k3, ln_g3, ln_b3, e_pair)


def _permute_w_in(w_in):
    def s(a, n):
        return w_in[:, :, a:a + n]

    pad = jnp.zeros(w_in.shape[:2] + (LANES - MLA_ROPE - 2 * ML_HEADS,), w_in.dtype)
    parts = [s(0, 1024), s(1024, 512), s(1544, 512), s(2056, 512), s(2568, 512), s(3080, 256),
             s(3336, 64), s(1536, 8), pad, s(5960, 128), s(4424, 512), s(4936, 512), s(5448, 512),
             s(6088, 512), s(3400, 1024)]
    return jnp.concatenate([p.astype(BF16) for p in parts], axis=-1)


def _permute_w_uq(w_uq):
    d = w_uq.shape[0]
    w = w_uq.reshape(d, MLA_Q_RANK, MLA_HEADS, MLA_NOPE + MLA_ROPE)
    w = jnp.pad(w, ((0, 0), (0, 0), (0, 0), (0, MLA_QK_PAD - MLA_NOPE - MLA_ROPE)))
    return w.reshape(d, MLA_Q_RANK, MLA_HEADS * MLA_QK_PAD).astype(BF16)


def _split_w_ukv(w_ukv):
    d = w_ukv.shape[0]
    w = w_ukv.reshape(d, MLA_KV_RANK, MLA_HEADS, MLA_NOPE + MLA_V)
    wk = w[..., :MLA_NOPE].reshape(d, MLA_KV_RANK, MLA_HEADS * MLA_NOPE)
    wv = w[..., MLA_NOPE:].reshape(d, MLA_KV_RANK, MLA_HEADS * MLA_V)
    return wk.astype(BF16), wv.astype(BF16)


def _rope_tables(positions):
    inv_freq = jnp.power(ROPE_BASE, -jnp.arange(0, MLA_ROPE, 2, dtype=F32) / MLA_ROPE)
    ang = positions.astype(F32).reshape(-1, 1) * inv_freq
    cos, sin = jnp.cos(ang), jnp.sin(ang)
    z32 = jnp.zeros_like(cos)
    z64 = jnp.zeros((ang.shape[0], LANES - MLA_ROPE), F32)
    cosf = jnp.concatenate([cos, cos, z64], axis=-1)
    s1 = jnp.concatenate([-sin, z32, z64], axis=-1)
    s2 = jnp.concatenate([z32, sin, z64], axis=-1)
    return cosf, s1, s2


def kernel(x, positions, norm_g, w_in, ml_conv_w, ml_conv_b, ml_i_bias, ml_f_bias, ml_norm_g,
           mla_q_norm_g, mla_w_uq, mla_kv_norm_g, mla_w_ukv, rw_mu, rw_w0, rw_w2, rw_a0, rw_a2,
           rw_v0, rw_v1, rw_v2, rw_k_k, rw_k_a, rw_r_k, rw_ln_g, rw_ln_b, w_out, final_norm_g):
    batch, seq, _ = x.shape
    depth = w_in.shape[0]
    m = batch * seq
    x2 = x.reshape(m, D_MODEL)

    w_in_p = _permute_w_in(w_in)
    w_uq_p = _permute_w_uq(mla_w_uq)
    w_uk_p, w_uv_p = _split_w_ukv(mla_w_ukv)
    w_out_b = w_out.astype(BF16)
    cosf, s1, s2 = _rope_tables(positions)

    def row3(a):
        return a.reshape(a.shape[0], 1, a.shape[1])

    norm_g3 = row3(norm_g)
    conv_b3 = row3(ml_conv_b)
    ml_norm_g3 = row3(ml_norm_g)
    qg3, kvg3 = row3(mla_q_norm_g), row3(mla_kv_norm_g)
    w = RW_WIDTH
    mu_r, mu_k, mu_v = row3(rw_mu[:, 0:w]), row3(rw_mu[:, w:2 * w]), row3(rw_mu[:, 2 * w:3 * w])
    mu_wa = row3(rw_mu[:, 3 * w:])
    w0, a0, k_k, k_a = row3(rw_w0), row3(rw_a0), row3(rw_k_k), row3(rw_k_a)
    w2p = jnp.pad(rw_w2, ((0, 0), (0, LANES - RW_DECAY_RANK), (0, 0)))
    a2p = jnp.pad(rw_a2, ((0, 0), (LANES - RW_AAA_RANK, 0), (0, 0)))
    v0 = row3(rw_v0)
    v1p = jnp.pad(rw_v1, ((0, 0), (0, 0), (0, LANES - RW_MV_RANK)))
    v2p = jnp.pad(rw_v2, ((0, 0), (0, LANES - RW_MV_RANK), (0, 0)))
    r_k3 = rw_r_k.reshape(depth, 1, w)
    ln_g3, ln_b3 = row3(rw_ln_g), row3(rw_ln_b)
    hid = jnp.arange(w) // RW_HEAD_DIM
    e_head = (hid[:, None] == hid[None, :]).astype(F32)
    e_pair = e_head[:LANES, :LANES].astype(BF16)
    final_g3 = final_norm_g.reshape(1, 1, D_MODEL)

    v_first = None
    h = _rmsnorm(x2, norm_g3, 0)
    for layer in range(depth):
        proj = _inproj(h, w_in_p, layer)
        gates_t = proj[:, OFF_KRG + MLA_ROPE:OFF_KRG + MLA_ROPE + 2 * ML_HEADS]
        gates_t = gates_t.reshape(batch, seq, 2 * ML_HEADS).transpose(0, 2, 1)
        gates_t = gates_t.reshape(batch, 2 * ML_HEADS, 1, seq)
        y_ml = _mlstm(proj, gates_t, ml_i_bias, ml_f_bias, ml_conv_w, conv_b3, ml_norm_g3,
                      layer, batch, seq)
        q, k, v = _mla_proj(proj, cosf, s1, s2, qg3, w_uq_p, kvg3, w_uk_p, w_uv_p, layer)
        y_mla = _attention(q, k, v, proj, batch, seq)
        r_s, ld, kh, v_rw, kn, b_rw = _rwkv_pre(proj, v_first, mu_wa, mu_r, mu_k, mu_v, w0, w2p,
                                                 a0, a2p, k_k, k_a, e_head, v0, v1p, v2p,
                                                 layer, batch, seq)
        if layer == 0:
            v_first = v_rw
        y_rw = _rwkv_scan(r_s, ld, kh, v_rw, kn, b_rw, proj, r_k3, ln_g3, ln_b3, e_pair,
                          layer, batch, seq)
        if layer == depth - 1:
            (x2,) = _outproj(x2, y_ml, y_mla, y_rw, w_out_b, final_g3, layer, True)
        else:
            x2, h = _outproj(x2, y_ml, y_mla, y_rw, w_out_b, norm_g3, layer, False)
    return x2.reshape(batch, seq, D_MODEL)
```

```python
import functools

import jax
import jax.numpy as jnp
from jax import lax
from jax.experimental import pallas as pl
from jax.experimental.pallas import tpu as pltpu

F32 = jnp.float32
BF16 = jnp.bfloat16

D_MODEL = 2048
NORM_EPS = 1e-6
ML_HEADS = 4
ML_HEAD_DIM = 128
ML_WIDTH = 512
ML_CONV = 4
MLA_HEADS = 8
MLA_NOPE = 128
MLA_ROPE = 64
MLA_V = 128
MLA_WIDTH = 1024
MLA_Q_RANK = 512
MLA_KV_RANK = 256
ROPE_BASE = 10000.0
RW_HEAD_DIM = 64
RW_WIDTH = 512
RW_HEADS = 8
RW_DECAY_RANK = 64
RW_AAA_RANK = 64
RW_MV_RANK = 32
RW_GN_EPS = 64e-5

LANES = 128
SUBLANES = 8
MLA_QK_PAD = 256
VMEM_LIMIT = 48 * 1024 * 1024
NEG_BIG = -1e30
LOG2E = 1.4426950408889634

OFF_QK = 0
OFF_MLV = 1024
OFF_MLO = 1536
OFF_MLZ = 2048
OFF_CQ = 2560
OFF_CKV = 3072
OFF_KRG = 3328
OFF_WA = 3456
OFF_RWR = 3584
OFF_RWK = 4096
OFF_RWV = 4608
OFF_RWZ = 5120
OFF_MLAZ = 5632
D_IN_PAD = 6656

TM_PROJ = 1024
TN_PROJ = 1664
TM_OUT = 512
TQ = 512
ML_TT = 1024
ML_L = 256
RW_TT = 512
RW_L = 64


def _split_bf16(x):
    hi = x.astype(BF16)
    lo = (x - hi.astype(F32)).astype(BF16)
    return hi, lo


def _mm(a, b, passes=1):
    if passes == 1:
        return jnp.dot(a.astype(BF16), b.astype(BF16), preferred_element_type=F32)
    ah, al = _split_bf16(a)
    bh, bl = _split_bf16(b)
    return (jnp.dot(ah, bh, preferred_element_type=F32)
            + jnp.dot(ah, bl, preferred_element_type=F32)
            + jnp.dot(al, bh, preferred_element_type=F32))


_NT = (((1,), (1,)), ((), ()))


def _mm_nt(a, b, passes=1):
    if passes == 1:
        return lax.dot_general(a.astype(BF16), b.astype(BF16), _NT, preferred_element_type=F32)
    ah, al = _split_bf16(a)
    bh, bl = _split_bf16(b)
    return (lax.dot_general(ah, bh, _NT, preferred_element_type=F32)
            + lax.dot_general(ah, bl, _NT, preferred_element_type=F32)
            + lax.dot_general(al, bh, _NT, preferred_element_type=F32))


def _sigmoid(x):
    return 1.0 / (1.0 + jnp.exp(-x))


def _silu(x):
    return x * _sigmoid(x)


def _log_sigmoid(x):
    return jnp.minimum(x, 0.0) - jnp.log(1.0 + jnp.exp(-jnp.abs(x)))


def _params(*sem):
    return pltpu.CompilerParams(dimension_semantics=sem, vmem_limit_bytes=VMEM_LIMIT)


def _rms_scale(x, g):
    return x * lax.rsqrt(jnp.mean(x * x, axis=-1, keepdims=True) + NORM_EPS) * g


def _rmsnorm_kernel(x_ref, g_ref, h_ref):
    h_ref[...] = _rms_scale(x_ref[...], g_ref[...]).astype(BF16)


def _rmsnorm(x2, norm_g3, layer):
    m = x2.shape[0]
    return pl.pallas_call(
        _rmsnorm_kernel,
        grid=(m // TM_OUT,),
        in_specs=[pl.BlockSpec((TM_OUT, D_MODEL), lambda i: (i, 0)),
                  pl.BlockSpec((None, 1, D_MODEL), lambda i: (layer, 0, 0))],
        out_specs=pl.BlockSpec((TM_OUT, D_MODEL), lambda i: (i, 0)),
        out_shape=jax.ShapeDtypeStruct((m, D_MODEL), BF16),
        compiler_params=_params("parallel"),
        name="rmsnorm",
    )(x2, norm_g3)


def _inproj_kernel(h_ref, w_ref, o_ref):
    o_ref[...] = jnp.dot(h_ref[...], w_ref[...], preferred_element_type=F32)


def _inproj(h, w_in_p, layer):
    m = h.shape[0]
    return pl.pallas_call(
        _inproj_kernel,
        grid=(m // TM_PROJ, D_IN_PAD // TN_PROJ),
        in_specs=[
            pl.BlockSpec((TM_PROJ, D_MODEL), lambda i, j: (i, 0)),
            pl.BlockSpec((None, D_MODEL, TN_PROJ), lambda i, j: (layer, 0, j)),
        ],
        out_specs=pl.BlockSpec((TM_PROJ, TN_PROJ), lambda i, j: (i, j)),
        out_shape=jax.ShapeDtypeStruct((m, D_IN_PAD), F32),
        compiler_params=_params("parallel", "arbitrary"),
        name="inproj",
    )(h, w_in_p)


def _outproj_kernel(x_ref, yml_ref, ymla_ref, yrw_ref, w_ref, g_ref, *out_refs, final):
    acc = x_ref[...]
    acc = acc + jnp.dot(yml_ref[...], w_ref[0:ML_WIDTH, :], preferred_element_type=F32)
    acc = acc + jnp.dot(ymla_ref[...], w_ref[ML_WIDTH:ML_WIDTH + MLA_WIDTH, :],
                        preferred_element_type=F32)
    acc = acc + jnp.dot(yrw_ref[...], w_ref[ML_WIDTH + MLA_WIDTH:, :], preferred_element_type=F32)
    normed = _rms_scale(acc, g_ref[...])
    if final:
        out_refs[0][...] = normed
    else:
        out_refs[0][...] = acc
        out_refs[1][...] = normed.astype(BF16)


def _outproj(x2, y_ml, y_mla, y_rw, w_out_b, gains3, layer, final):
    m = x2.shape[0]
    g_index = 0 if final else layer + 1
    row = pl.BlockSpec((TM_OUT, D_MODEL), lambda i: (i, 0))
    out_specs = [row] if final else [row, row]
    out_shape = [jax.ShapeDtypeStruct((m, D_MODEL), F32)]
    if not final:
        out_shape.append(jax.ShapeDtypeStruct((m, D_MODEL), BF16))
    return pl.pallas_call(
        functools.partial(_outproj_kernel, final=final),
        grid=(m // TM_OUT,),
        in_specs=[
            row,
            pl.BlockSpec((TM_OUT, ML_WIDTH), lambda i: (i, 0)),
            pl.BlockSpec((TM_OUT, MLA_WIDTH), lambda i: (i, 0)),
            pl.BlockSpec((TM_OUT, RW_WIDTH), lambda i: (i, 0)),
            pl.BlockSpec((None, D_MODEL, D_MODEL), lambda i: (layer, 0, 0)),
            pl.BlockSpec((None, 1, D_MODEL), lambda i: (g_index, 0, 0)),
        ],
        out_specs=out_specs,
        out_shape=out_shape,
        compiler_params=_params("parallel"),
        name="outproj",
    )(x2, y_ml, y_mla, y_rw, w_out_b, gains3)


def _mla_proj_kernel(cq_ref, ckv_ref, krg_ref, cos_ref, s1_ref, s2_ref, qg_ref, wq_ref, kvg_ref,
                     wk_ref, wv_ref, q_out, k_out, v_out):
    cq = cq_ref[...]
    cqn = (cq * lax.rsqrt(jnp.mean(cq * cq, axis=-1, keepdims=True) + NORM_EPS)
           * qg_ref[...]).astype(BF16)
    q = jnp.dot(cqn, wq_ref[...], preferred_element_type=F32)
    ckv = ckv_ref[...]
    ckvn = (ckv * lax.rsqrt(jnp.mean(ckv * ckv, axis=-1, keepdims=True) + NORM_EPS)
            * kvg_ref[...]).astype(BF16)
    kn = jnp.dot(ckvn, wk_ref[...], preferred_element_type=F32)
    v = jnp.dot(ckvn, wv_ref[...], preferred_element_type=F32)
    cosf, s1, s2 = cos_ref[...], s1_ref[...], s2_ref[...]

    def rope(xb):
        return (xb * cosf + pltpu.roll(xb, LANES - MLA_ROPE // 2, 1) * s1
                + pltpu.roll(xb, MLA_ROPE // 2, 1) * s2)

    scale = (MLA_NOPE + MLA_ROPE) ** -0.5 * LOG2E
    kr = rope(krg_ref[...]).astype(BF16)
    for h in range(MLA_HEADS):
        c0 = h * MLA_QK_PAD
        q_out[:, c0:c0 + MLA_NOPE] = (q[:, c0:c0 + MLA_NOPE] * scale).astype(BF16)
        q_out[:, c0 + MLA_NOPE:c0 + MLA_QK_PAD] = (
            rope(q[:, c0 + MLA_NOPE:c0 + MLA_QK_PAD]) * scale).astype(BF16)
        k_out[:, c0:c0 + MLA_NOPE] = kn[:, h * MLA_NOPE:(h + 1) * MLA_NOPE].astype(BF16)
        k_out[:, c0 + MLA_NOPE:c0 + MLA_QK_PAD] = kr
    v_out[...] = v.T.astype(BF16)


def _mla_proj(proj, cosf, s1, s2, qg3, wq_p, kvg3, wk_p, wv_p, layer):
    m = proj.shape[0]
    tm = TQ
    row = lambda i: (i, 0)
    return pl.pallas_call(
        _mla_proj_kernel,
        grid=(m // tm,),
        in_specs=[
            pl.BlockSpec((tm, MLA_Q_RANK), lambda i: (i, OFF_CQ // MLA_Q_RANK)),
            pl.BlockSpec((tm, MLA_KV_RANK), lambda i: (i, OFF_CKV // MLA_KV_RANK)),
            pl.BlockSpec((tm, LANES), lambda i: (i, OFF_KRG // LANES)),
            pl.BlockSpec((tm, LANES), row),
            pl.BlockSpec((tm, LANES), row),
            pl.BlockSpec((tm, LANES), row),
            pl.BlockSpec((None, 1, MLA_Q_RANK), lambda i: (layer, 0, 0)),
            pl.BlockSpec((None, MLA_Q_RANK, MLA_HEADS * MLA_QK_PAD), lambda i: (layer, 0, 0)),
            pl.BlockSpec((None, 1, MLA_KV_RANK), lambda i: (layer, 0, 0)),
            pl.BlockSpec((None, MLA_KV_RANK, MLA_HEADS * MLA_NOPE), lambda i: (layer, 0, 0)),
            pl.BlockSpec((None, MLA_KV_RANK, MLA_HEADS * MLA_V), lambda i: (layer, 0, 0)),
        ],
        out_specs=[
            pl.BlockSpec((tm, MLA_HEADS * MLA_QK_PAD), row),
            pl.BlockSpec((tm, MLA_HEADS * MLA_QK_PAD), row),
            pl.BlockSpec((None, MLA_HEADS * MLA_V, tm), lambda i: (i, 0, 0)),
        ],
        out_shape=[
            jax.ShapeDtypeStruct((m, MLA_HEADS * MLA_QK_PAD), BF16),
            jax.ShapeDtypeStruct((m, MLA_HEADS * MLA_QK_PAD), BF16),
            jax.ShapeDtypeStruct((m // tm, MLA_HEADS * MLA_V, tm), BF16),
        ],
        compiler_params=_params("parallel"),
        name="mla_proj",
    )(proj, proj, proj, cosf, s1, s2, qg3, wq_p, kvg3, wk_p, wv_p)


ATTN_HEADS_PER_STEP = 4


def _attn_kernel(q_ref, k_ref, vt_ref, z_ref, o_ref, *, tq):
    i = pl.program_id(2)
    nh = ATTN_HEADS_PER_STEP
    qs = [q_ref[:, h * MLA_QK_PAD:(h + 1) * MLA_QK_PAD] for h in range(nh)]
    kidx = lax.broadcasted_iota(jnp.int32, (tq, tq), 0)
    qidx = lax.broadcasted_iota(jnp.int32, (tq, tq), 1)

    def step(j, carry, diagonal):
        off = pl.multiple_of(j * tq, tq)
        ss = []
        for h in range(nh):
            kb = k_ref[pl.ds(off, tq), h * MLA_QK_PAD:(h + 1) * MLA_QK_PAD]
            s = lax.dot_general(kb, qs[h], _NT, preferred_element_type=F32)
            ss.append(jnp.where(kidx <= qidx, s, NEG_BIG) if diagonal else s)
        out = []
        for h in range(nh):
            m, l, acc = carry[h]
            s = ss[h]
            m_new = jnp.maximum(m, jnp.max(s, axis=0, keepdims=True))
            alpha = jnp.exp2(m - m_new)
            p = jnp.exp2(s - m_new)
            l = alpha * l + jnp.sum(p, axis=0, keepdims=True)
            acc = alpha * acc + jnp.dot(vt_ref[j, h], p.astype(BF16), preferred_element_type=F32)
            out.append((m_new, l, acc))
        return tuple(out)

    init = tuple((jnp.full((1, tq), NEG_BIG, F32), jnp.zeros((1, tq), F32),
                  jnp.zeros((MLA_V, tq), F32)) for _ in range(nh))
    carry = lax.fori_loop(0, i, lambda j, c: step(j, c, False), init)
    carry = step(i, carry, True)
    for h in range(nh):
        _, l, acc = carry[h]
        y = (acc / l).T
        cs = slice(h * MLA_V, (h + 1) * MLA_V)
        o_ref[:, cs] = (_silu(z_ref[:, cs]) * y).astype(BF16)


def _attention(q, k, v_t, proj, batch, seq):
    m = q.shape[0]
    nq = seq // TQ
    nh = ATTN_HEADS_PER_STEP
    v_t5 = v_t.reshape(batch, nq, MLA_HEADS, MLA_V, TQ)
    return pl.pallas_call(
        functools.partial(_attn_kernel, tq=TQ),
        grid=(batch, MLA_HEADS // nh, nq),
        in_specs=[
            pl.BlockSpec((TQ, nh * MLA_QK_PAD), lambda b, h, i: (b * nq + i, h)),
            pl.BlockSpec((seq, nh * MLA_QK_PAD), lambda b, h, i: (b, h)),
            pl.BlockSpec((None, nq, nh, MLA_V, TQ), lambda b, h, i: (b, 0, h, 0, 0)),
            pl.BlockSpec((TQ, nh * MLA_V),
                         lambda b, h, i: (b * nq + i, OFF_MLAZ // (nh * MLA_V) + h)),
        ],
        out_specs=pl.BlockSpec((TQ, nh * MLA_V), lambda b, h, i: (b * nq + i, h)),
        out_shape=jax.ShapeDtypeStruct((m, MLA_WIDTH), BF16),
        compiler_params=_params("parallel", "parallel", "arbitrary"),
        name="mla_attn",
    )(q, k, v_t5, proj)


def _shift_rows(x, prev8, s):
    rolled = pltpu.roll(x, s, 0)
    prev_rolled = pltpu.roll(prev8, s, 0)
    rid = lax.broadcasted_iota(jnp.int32, (SUBLANES, x.shape[1]), 0)
    top = jnp.where(rid < s, prev_rolled, rolled[0:SUBLANES])
    return jnp.concatenate([top, rolled[SUBLANES:]], axis=0)


def _mlstm_kernel(ib_ref, fb_ref, q_ref, k_ref, v_ref, o_ref, z_ref, g_ref, w_ref, b_ref, ng_ref,
                  out_ref, ct_ref, n_ref, m_ref, qp_ref, kp_ref, *, layer, tt, chunk):
    @pl.when(pl.program_id(1) == 0)
    def _():
        for ref in (ct_ref, n_ref, m_ref, qp_ref, kp_ref):
            ref[...] = jnp.zeros_like(ref)

    hd = ML_HEAD_DIM
    w_all = w_ref[...]
    b_all = b_ref[...]

    def conv_silu(x_ref, prev_ref, w, b):
        x = x_ref[...]
        prev8 = prev_ref[...]
        y = b + w[ML_CONV - 1:ML_CONV] * x
        for s in range(1, ML_CONV):
            y = y + w[ML_CONV - 1 - s:ML_CONV - s] * _shift_rows(x, prev8, s)
        prev_ref[...] = x[tt - SUBLANES:tt]
        return _silu(y)

    q_all = conv_silu(q_ref, qp_ref, w_all[:, :ML_WIDTH], b_all[:, :ML_WIDTH]) * (hd ** -0.5)
    k_all = conv_silu(k_ref, kp_ref, w_all[:, ML_WIDTH:], b_all[:, ML_WIDTH:])
    li_rows = [g_ref[h] + ib_ref[layer, h] for h in range(ML_HEADS)]
    lf_rows = [_log_sigmoid(g_ref[ML_HEADS + h] + fb_ref[layer, h]) for h in range(ML_HEADS)]

    L = chunk
    rows = lax.broadcasted_iota(jnp.int32, (L, L), 0)
    cols = lax.broadcasted_iota(jnp.int32, (L, L), 1)
    tri = rows >= cols
    eye = rows == cols
    for c in range(tt // L):
        sl = slice(c * L, (c + 1) * L)
        st = []
        for h in range(ML_HEADS):
            cs = slice(h * hd, (h + 1) * hd)
            li_row = li_rows[h][:, sl]
            lf_b = jnp.broadcast_to(lf_rows[h][:, sl], (L, L))
            li_b = jnp.broadcast_to(li_row, (L, L))
            b_col = jnp.sum(jnp.where(tri, lf_b, 0.0), axis=-1, keepdims=True)
            lf_col = jnp.sum(jnp.where(eye, lf_b, 0.0), axis=-1, keepdims=True)
            li_col = jnp.sum(jnp.where(eye, li_b, 0.0), axis=-1, keepdims=True)
            b_row = jnp.sum(jnp.where(rows <= cols, jnp.broadcast_to(lf_col, (L, L)), 0.0),
                            axis=0, keepdims=True)
            m_prev = m_ref[h][:, 0:1]
            dmat = jnp.where(tri, b_col - b_row + li_row, NEG_BIG)
            m_inter = b_col + m_prev
            m_t = jnp.maximum(m_inter, jnp.max(dmat, axis=-1, keepdims=True))
            st.append(dict(cs=cs, qc=q_all[sl, cs].astype(BF16), kc=k_all[sl, cs],
                           vc=v_ref[sl, cs].astype(BF16), pm=jnp.exp(dmat - m_t),
                           inter=jnp.exp(m_inter - m_t), m_t=m_t, b_col=b_col, li_col=li_col,
                           m_prev=m_prev))
        for s in st:
            s["smat"] = lax.dot_general(s["qc"], s["kc"].astype(BF16), _NT,
                                        preferred_element_type=F32) * s["pm"]
        for h, s in enumerate(st):
            ct = ct_ref[h]
            n_row = n_ref[h]
            num = (s["inter"] * jnp.dot(s["qc"], ct.astype(BF16), preferred_element_type=F32)
                   + jnp.dot(s["smat"].astype(BF16), s["vc"], preferred_element_type=F32))
            den = (s["inter"] * jnp.sum(s["qc"].astype(F32) * n_row, axis=-1, keepdims=True)
                   + jnp.sum(s["smat"], axis=-1, keepdims=True))
            s["hh"] = num / jnp.maximum(jnp.abs(den), jnp.exp(-s["m_t"]))
            b_last = s["b_col"][L - 1:L, :]
            g_col = b_last - s["b_col"] + s["li_col"]
            m_new = jnp.maximum(b_last + s["m_prev"], jnp.max(g_col, axis=0, keepdims=True))
            decay = jnp.exp(b_last + s["m_prev"] - m_new)
            kw = s["kc"] * jnp.exp(g_col - m_new)
            ct_ref[h] = decay * ct + jnp.dot(kw.T.astype(BF16), s["vc"],
                                             preferred_element_type=F32)
            n_ref[h] = decay * n_row + jnp.sum(kw, axis=0, keepdims=True)
            m_ref[h] = jnp.broadcast_to(m_new, (1, hd))
        for s in st:
            cs = s["cs"]
            hh = s["hh"]
            mu = jnp.mean(hh, axis=-1, keepdims=True)
            dd = hh - mu
            var = jnp.mean(dd * dd, axis=-1, keepdims=True)
            yn = dd * lax.rsqrt(var + NORM_EPS) * ng_ref[:, cs]
            out_ref[sl, cs] = (_silu(z_ref[sl, cs]) * (_sigmoid(o_ref[sl, cs]) * yn)).astype(BF16)


def _mlstm(proj, gates_t, i_bias, f_bias, conv_w, conv_b3, norm_g3, layer, batch, seq):
    m = proj.shape[0]
    tt = min(ML_TT, seq)
    nt = seq // tt
    hd = ML_HEAD_DIM
    w = ML_WIDTH

    def col(off):
        return pl.BlockSpec((tt, w), lambda b, t: (b * nt + t, off // w))

    smem = pl.BlockSpec(memory_space=pltpu.SMEM)
    return pl.pallas_call(
        functools.partial(_mlstm_kernel, layer=layer, tt=tt, chunk=min(ML_L, tt)),
        grid=(batch, nt),
        in_specs=[
            smem, smem,
            col(OFF_QK), col(OFF_QK + w), col(OFF_MLV), col(OFF_MLO), col(OFF_MLZ),
            pl.BlockSpec((None, 2 * ML_HEADS, 1, tt), lambda b, t: (b, 0, 0, t)),
            pl.BlockSpec((None, ML_CONV, 2 * w), lambda b, t: (layer, 0, 0)),
            pl.BlockSpec((None, 1, 2 * w), lambda b, t: (layer, 0, 0)),
            pl.BlockSpec((None, 1, w), lambda b, t: (layer, 0, 0)),
        ],
        out_specs=pl.BlockSpec((tt, w), lambda b, t: (b * nt + t, 0)),
        out_shape=jax.ShapeDtypeStruct((m, w), BF16),
        scratch_shapes=[
            pltpu.VMEM((ML_HEADS, hd, hd), F32),
            pltpu.VMEM((ML_HEADS, 1, hd), F32),
            pltpu.VMEM((ML_HEADS, 1, hd), F32),
            pltpu.VMEM((SUBLANES, w), F32),
            pltpu.VMEM((SUBLANES, w), F32),
        ],
        compiler_params=_params("parallel", "arbitrary"),
        name="mlstm",
    )(i_bias, f_bias, proj, proj, proj, proj, proj, gates_t, conv_w, conv_b3, norm_g3)


RW_PRE_PASSES = 3


def _rwkv_pre_kernel(*refs, has_vres, tm):
    if has_vres:
        (wa_ref, r_ref, k_ref, v_ref, mu_wa, mu_r, mu_k, mu_v, w0_ref, w2_ref, a0_ref, a2_ref,
         kk_ref, ka_ref, e_ref, vf_ref, v0_ref, v1_ref, v2_ref,
         r_out, ld_out, kh_out, v_out, kn_out, b_out, c_wa, c_r, c_k, c_v) = refs
    else:
        (wa_ref, r_ref, k_ref, v_ref, mu_wa, mu_r, mu_k, mu_v, w0_ref, w2_ref, a0_ref, a2_ref,
         kk_ref, ka_ref, e_ref,
         r_out, ld_out, kh_out, v_out, kn_out, b_out, c_wa, c_r, c_k, c_v) = refs

    @pl.when(pl.program_id(1) == 0)
    def _():
        for c in (c_wa, c_r, c_k, c_v):
            c[...] = jnp.zeros_like(c)

    def mix(x_ref, c_ref, mu_ref):
        x = x_ref[...]
        rolled = pltpu.roll(x, 1, 0)
        rid = lax.broadcasted_iota(jnp.int32, (SUBLANES, x.shape[1]), 0)
        prev_last = jnp.broadcast_to(c_ref[SUBLANES - 1:SUBLANES, :], (SUBLANES, x.shape[1]))
        top = jnp.where(rid == 0, prev_last, rolled[0:SUBLANES])
        xprev = jnp.concatenate([top, rolled[SUBLANES:]], axis=0)
        c_ref[...] = x[tm - SUBLANES:tm]
        return x + mu_ref[...] * (xprev - x)

    xwa = mix(wa_ref, c_wa, mu_wa)
    r = mix(r_ref, c_r, mu_r)
    k = mix(k_ref, c_k, mu_k)
    v = mix(v_ref, c_v, mu_v)
    zw = w0_ref[...] + _mm(jnp.tanh(xwa), w2_ref[...], RW_PRE_PASSES)
    za = a0_ref[...] + _mm(xwa, a2_ref[...], RW_PRE_PASSES)
    log_w = _log_sigmoid(zw) - 0.5
    ld_out[...] = -jnp.exp(log_w)
    a = _sigmoid(za)
    if has_vres:
        gate = _sigmoid(v0_ref[...] + _mm(_mm(v, v1_ref[...], RW_PRE_PASSES), v2_ref[...],
                                          RW_PRE_PASSES))
        v = v + (vf_ref[...] - v) * gate
    kk = k * kk_ref[...]
    ss = _mm(kk * kk, e_ref[...])
    kn = kk / jnp.maximum(jnp.sqrt(ss), 1e-12)
    r_out[...] = r
    kh_out[...] = k * (1.0 + (a - 1.0) * ka_ref[...])
    v_out[...] = v
    kn_out[...] = kn
    b_out[...] = kn * a


RW_GROUP = 2


def _rwkv_scan_kernel(r_ref, ld_ref, kh_ref, v_ref, kn_ref, b_ref, z_ref, rk_ref, g_ref, bias_ref,
                      e_ref, out_ref, m_ref, ac_ref, gc_ref, qs_ref, y_ref, *, tt, chunk):
    @pl.when(pl.program_id(1) == 0)
    def _():
        m_ref[...] = jnp.zeros_like(m_ref)

    L = chunk
    L2 = 2 * L
    nchunk = tt // L
    npair = RW_WIDTH // LANES
    rows = lax.broadcasted_iota(jnp.int32, (L, L), 0)
    cols = lax.broadcasted_iota(jnp.int32, (L, L), 1)
    tri_b = jnp.where(rows >= cols, 1.0, 0.0).astype(BF16)
    lane = lax.broadcasted_iota(jnp.int32, (1, LANES), 1)
    m0 = jnp.where(lane < RW_HEAD_DIM, 1.0, 0.0)
    m1 = 1.0 - m0
    r2 = lax.broadcasted_iota(jnp.int32, (L2, L2), 0)
    c2 = lax.broadcasted_iota(jnp.int32, (L2, L2), 1)
    same_blk = (r2 < L) == (c2 < L)
    t2 = jnp.where(r2 < L, r2, r2 - L)
    s2 = jnp.where(c2 < L, c2, c2 - L)
    mask_strict = jnp.logical_and(same_blk, t2 > s2)
    mask_incl = jnp.logical_and(same_blk, t2 >= s2)
    eye2 = lax.broadcasted_iota(jnp.int32, (LANES, LANES), 0) == lax.broadcasted_iota(
        jnp.int32, (LANES, LANES), 1)
    n_double = max(1, (L - 1).bit_length())

    def stack(x):
        return jnp.concatenate([x * m0, x * m1], axis=0)

    def prepare(p, c):
        sl = pl.ds(pl.multiple_of(c * L, L), L)
        cs = slice(p * LANES, (p + 1) * LANES)
        ld = ld_ref[sl, cs]
        r, kh, v, kn, b = r_ref[sl, cs], kh_ref[sl, cs], v_ref[sl, cs], kn_ref[sl, cs], b_ref[sl, cs]
        ld_hi, ld_lo = _split_bf16(ld)
        logp = (jnp.dot(tri_b, ld_hi, preferred_element_type=F32)
                + jnp.dot(tri_b, ld_lo, preferred_element_type=F32))
        cmid = logp[L // 2 - 1:L // 2, :]
        e = logp - cmid
        e_last = e[L - 1:L, :]
        p_mid = jnp.exp(cmid)
        at_s = stack(-kn * jnp.exp(e - ld))
        rt_s = stack(r * jnp.exp(e))
        inv = jnp.exp(-e)
        tail = jnp.exp(e_last - e)
        lhs = jnp.concatenate([at_s, rt_s], axis=0).astype(BF16)
        rhs = jnp.concatenate([stack(b * inv), stack(kh * inv)], axis=0).astype(BF16)
        hat_t = jnp.concatenate([stack(b * tail), stack(kh * tail)], axis=0).T.astype(BF16)
        return dict(sl=sl, cs=cs, idx=p * nchunk + c, lhs=lhs, rhs=rhs, hat_t=hat_t,
                    at_true=at_s * p_mid, rt_true=rt_s * p_mid, v_s=stack(v).astype(BF16),
                    p_last=jnp.exp(e_last + cmid))

    def coeff_body(cg, carry):
        insts = [prepare(p, cg * RW_GROUP + g) for p in range(npair) for g in range(RW_GROUP)]
        for s in insts:
            aa = lax.dot_general(s["lhs"], s["rhs"], _NT, preferred_element_type=F32)
            s["nmat"] = jnp.where(mask_strict, aa[0:L2, 0:L2], 0.0).astype(BF16)
            s["a_ak"] = jnp.where(mask_strict, aa[0:L2, L2:], 0.0).astype(BF16)
            s["a_r"] = jnp.concatenate([jnp.where(mask_incl, aa[L2:, 0:L2], 0.0),
                                        jnp.where(mask_incl, aa[L2:, L2:], 0.0)],
                                       axis=1).astype(BF16)
        for s in insts:
            s["x"] = jnp.concatenate(
                [s["at_true"], jnp.dot(s["a_ak"], s["v_s"], preferred_element_type=F32)],
                axis=1)
        for it in range(n_double):
            for s in insts:
                s["x"] = s["x"] + jnp.dot(s["nmat"], s["x"].astype(BF16),
                                          preferred_element_type=F32)
            if it + 1 < n_double:
                for s in insts:
                    s["nmat"] = jnp.dot(s["nmat"], s["nmat"],
                                        preferred_element_type=F32).astype(BF16)
        for s in insts:
            big_l = jnp.concatenate([s["a_r"], s["hat_t"]], axis=0)
            big_r = jnp.concatenate(
                [s["x"].astype(BF16),
                 jnp.concatenate([jnp.zeros((L2, LANES), BF16), s["v_s"]], axis=1)], axis=0)
            res = jnp.dot(big_l, big_r, preferred_element_type=F32)
            q_s = s["rt_true"] + res[0:L2, :LANES]
            y0_s = res[0:L2, LANES:]
            qs_ref[s["sl"], s["cs"]] = q_s[0:L] + q_s[L:]
            y_ref[s["sl"], s["cs"]] = y0_s[0:L] + y0_s[L:]
            ac_ref[s["idx"]] = (jnp.where(eye2, jnp.broadcast_to(s["p_last"], (LANES, LANES)), 0.0)
                                + res[L2:, :LANES])
            gc_ref[s["idx"]] = res[L2:, LANES:]
        return carry

    lax.fori_loop(0, nchunk // RW_GROUP, coeff_body, 0)

    def chain_body(c, carry):
        sl = pl.ds(pl.multiple_of(c * L, L), L)
        for p in range(npair):
            cs = slice(p * LANES, (p + 1) * LANES)
            idx = p * nchunk + c
            mp = m_ref[p]
            y_ref[sl, cs] = _mm(qs_ref[sl, cs], mp) + y_ref[sl, cs]
            m_ref[p] = _mm(ac_ref[idx], mp) + gc_ref[idx]
        return carry

    lax.fori_loop(0, nchunk, chain_body, 0)

    e_b = e_ref[...]
    inv_n = 1.0 / RW_HEAD_DIM

    def head_sum(x):
        return jnp.dot(x.astype(BF16), e_b, preferred_element_type=F32)

    for p in range(npair):
        cs = slice(p * LANES, (p + 1) * LANES)
        y = y_ref[:, cs]
        mu = head_sum(y) * inv_n
        d = y - mu
        var = head_sum(d * d) * inv_n
        yn = d * lax.rsqrt(var + RW_GN_EPS) * g_ref[:, cs] + bias_ref[:, cs]
        v_all = v_ref[:, cs]
        bonus = head_sum(r_ref[:, cs] * kh_ref[:, cs] * rk_ref[:, cs])
        out_ref[:, cs] = (_silu(z_ref[:, cs]) * (yn + bonus * v_all)).astype(BF16)


def _rwkv_kernel(*refs, has_vres, tt, chunk):
    n_mix = 19 if has_vres else 15
    mix_in = refs[:n_mix]
    z_ref, rk_ref, g_ref, bias_ref, e_ref = refs[n_mix:n_mix + 5]
    n_out = 1 if has_vres else 2
    outs = refs[n_mix + 5:n_mix + 5 + n_out]
    (c_wa, c_r, c_k, c_v, r_s, ld_s, kh_s, v_s, kn_s, b_s,
     m_ref, ac_ref, gc_ref, qs_ref, y_ref) = refs[n_mix + 5 + n_out:]
    _rwkv_pre_kernel(*mix_in, r_s, ld_s, kh_s, v_s, kn_s, b_s, c_wa, c_r, c_k, c_v,
                     has_vres=has_vres, tm=tt)
    if not has_vres:
        outs[1][...] = v_s[...]
    _rwkv_scan_kernel(r_s, ld_s, kh_s, v_s, kn_s, b_s, z_ref, rk_ref, g_ref, bias_ref, e_ref,
                      outs[0], m_ref, ac_ref, gc_ref, qs_ref, y_ref, tt=tt, chunk=chunk)


def _rwkv(proj, v_first, mu_wa, mu_r, mu_k, mu_v, w0, w2p, a0, a2p, k_k, k_a, e_head,
          v0, v1p, v2p, r_k3, ln_g3, ln_b3, e_pair, layer, batch, seq):
    m = proj.shape[0]
    tt = min(RW_TT, seq)
    nt = seq // tt
    chunk = min(RW_L, tt)
    has_vres = layer > 0
    w = RW_WIDTH
    npair = w // LANES

    def colblk(off, width):
        return pl.BlockSpec((tt, width), lambda b, t: (b * nt + t, off // width))

    def lay(shape):
        return pl.BlockSpec((None,) + shape, lambda b, t: (layer,) + (0,) * len(shape))

    def lay1(shape):
        return pl.BlockSpec((None,) + shape, lambda b, t: (layer - 1,) + (0,) * len(shape))

    row = pl.BlockSpec((tt, w), lambda b, t: (b * nt + t, 0))
    in_specs = [colblk(OFF_WA, LANES), colblk(OFF_RWR, w), colblk(OFF_RWK, w), colblk(OFF_RWV, w),
                lay((1, LANES)), lay((1, w)), lay((1, w)), lay((1, w)),
                lay((1, w)), lay((LANES, w)), lay((1, w)), lay((LANES, w)),
                lay((1, w)), lay((1, w)),
                pl.BlockSpec((w, w), lambda b, t: (0, 0))]
    args = [proj, proj, proj, proj, mu_wa, mu_r, mu_k, mu_v, w0, w2p, a0, a2p, k_k, k_a, e_head]
    if has_vres:
        in_specs += [row, lay1((1, w)), lay1((w, LANES)), lay1((LANES, w))]
        args += [v_first, v0, v1p, v2p]
    in_specs += [colblk(OFF_RWZ, w), lay((1, w)), lay((1, w)), lay((1, w)),
                 pl.BlockSpec((LANES, LANES), lambda b, t: (0, 0))]
    args += [proj, r_k3, ln_g3, ln_b3, e_pair]
    out_specs = [row]
    out_shape = [jax.ShapeDtypeStruct((m, w), BF16)]
    if not has_vres:
        out_specs.append(row)
        out_shape.append(jax.ShapeDtypeStruct((m, w), F32))
    scratch = [pltpu.VMEM((SUBLANES, LANES), F32)] + [pltpu.VMEM((SUBLANES, w), F32)] * 3
    scratch += [pltpu.VMEM((tt, w), F32)] * 6
    scratch += [
        pltpu.VMEM((npair, LANES, LANES), F32),
        pltpu.VMEM((npair * (tt // chunk), LANES, LANES), F32),
        pltpu.VMEM((npair * (tt // chunk), LANES, LANES), F32),
        pltpu.VMEM((tt, w), F32),
        pltpu.VMEM((tt, w), F32),
    ]
    return pl.pallas_call(
        functools.partial(_rwkv_kernel, has_vres=has_vres, tt=tt, chunk=chunk),
        grid=(batch, nt),
        in_specs=in_specs,
        out_specs=out_specs,
        out_shape=out_shape,
        scratch_shapes=scratch,
        compiler_params=_params("parallel", "arbitrary"),
        name="rwkv",
    )(*args)


def _permute_w_in(w_in):
    def s(a, n):
        return w_in[:, :, a:a + n]

    pad = jnp.zeros(w_in.shape[:2] + (LANES - MLA_ROPE - 2 * ML_HEADS,), w_in.dtype)
    parts = [s(0, 1024), s(1024, 512), s(1544, 512), s(2056, 512), s(2568, 512), s(3080, 256),
             s(3336, 64), s(1536, 8), pad, s(5960, 128), s(4424, 512), s(4936, 512), s(5448, 512),
             s(6088, 512), s(3400, 1024)]
    return jnp.concatenate([p.astype(BF16) for p in parts], axis=-1)


def _permute_w_uq(w_uq):
    d = w_uq.shape[0]
    w = w_uq.reshape(d, MLA_Q_RANK, MLA_HEADS, MLA_NOPE + MLA_ROPE)
    w = jnp.pad(w, ((0, 0), (0, 0), (0, 0), (0, MLA_QK_PAD - MLA_NOPE - MLA_ROPE)))
    return w.reshape(d, MLA_Q_RANK, MLA_HEADS * MLA_QK_PAD).astype(BF16)


def _split_w_ukv(w_ukv):
    d = w_ukv.shape[0]
    w = w_ukv.reshape(d, MLA_KV_RANK, MLA_HEADS, MLA_NOPE + MLA_V)
    wk = w[..., :MLA_NOPE].reshape(d, MLA_KV_RANK, MLA_HEADS * MLA_NOPE)
    wv = w[..., MLA_NOPE:].reshape(d, MLA_KV_RANK, MLA_HEADS * MLA_V)
    return wk.astype(BF16), wv.astype(BF16)


def _rope_tables(positions):
    inv_freq = jnp.power(ROPE_BASE, -jnp.arange(0, MLA_ROPE, 2, dtype=F32) / MLA_ROPE)
    ang = positions.astype(F32).reshape(-1, 1) * inv_freq
    cos, sin = jnp.cos(ang), jnp.sin(ang)
    z32 = jnp.zeros_like(cos)
    z64 = jnp.zeros((ang.shape[0], LANES - MLA_ROPE), F32)
    cosf = jnp.concatenate([cos, cos, z64], axis=-1)
    s1 = jnp.concatenate([-sin, z32, z64], axis=-1)
    s2 = jnp.concatenate([z32, sin, z64], axis=-1)
    return cosf, s1, s2


def kernel(x, positions, norm_g, w_in, ml_conv_w, ml_conv_b, ml_i_bias, ml_f_bias, ml_norm_g,
           mla_q_norm_g, mla_w_uq, mla_kv_norm_g, mla_w_ukv, rw_mu, rw_w0, rw_w2, rw_a0, rw_a2,
           rw_v0, rw_v1, rw_v2, rw_k_k, rw_k_a, rw_r_k, rw_ln_g, rw_ln_b, w_out, final_norm_g):
    batch, seq, _ = x.shape
    depth = w_in.shape[0]
    m = batch * seq
    x2 = x.reshape(m, D_MODEL)

    w_in_p = _permute_w_in(w_in)
    w_uq_p = _permute_w_uq(mla_w_uq)
    w_uk_p, w_uv_p = _split_w_ukv(mla_w_ukv)
    w_out_b = w_out.astype(BF16)
    cosf, s1, s2 = _rope_tables(positions)

    def row3(a):
        return a.reshape(a.shape[0], 1, a.shape[1])

    norm_g3 = row3(norm_g)
    conv_b3 = row3(ml_conv_b)
    ml_norm_g3 = row3(ml_norm_g)
    qg3, kvg3 = row3(mla_q_norm_g), row3(mla_kv_norm_g)
    w = RW_WIDTH
    mu_r, mu_k, mu_v = row3(rw_mu[:, 0:w]), row3(rw_mu[:, w:2 * w]), row3(rw_mu[:, 2 * w:3 * w])
    mu_wa = row3(rw_mu[:, 3 * w:])
    w0, a0, k_k, k_a = row3(rw_w0), row3(rw_a0), row3(rw_k_k), row3(rw_k_a)
    w2p = jnp.pad(rw_w2, ((0, 0), (0, LANES - RW_DECAY_RANK), (0, 0)))
    a2p = jnp.pad(rw_a2, ((0, 0), (LANES - RW_AAA_RANK, 0), (0, 0)))
    v0 = row3(rw_v0)
    v1p = jnp.pad(rw_v1, ((0, 0), (0, 0), (0, LANES - RW_MV_RANK)))
    v2p = jnp.pad(rw_v2, ((0, 0), (0, LANES - RW_MV_RANK), (0, 0)))
    r_k3 = rw_r_k.reshape(depth, 1, w)
    ln_g3, ln_b3 = row3(rw_ln_g), row3(rw_ln_b)
    hid = jnp.arange(w) // RW_HEAD_DIM
    e_head = (hid[:, None] == hid[None, :]).astype(F32)
    e_pair = e_head[:LANES, :LANES].astype(BF16)
    final_g3 = final_norm_g.reshape(1, 1, D_MODEL)

    v_first = None
    h = _rmsnorm(x2, norm_g3, 0)
    for layer in range(depth):
        proj = _inproj(h, w_in_p, layer)
        gates_t = proj[:, OFF_KRG + MLA_ROPE:OFF_KRG + MLA_ROPE + 2 * ML_HEADS]
        gates_t = gates_t.reshape(batch, seq, 2 * ML_HEADS).transpose(0, 2, 1)
        gates_t = gates_t.reshape(batch, 2 * ML_HEADS, 1, seq)
        y_ml = _mlstm(proj, gates_t, ml_i_bias, ml_f_bias, ml_conv_w, conv_b3, ml_norm_g3,
                      layer, batch, seq)
        q, k, v = _mla_proj(proj, cosf, s1, s2, qg3, w_uq_p, kvg3, w_uk_p, w_uv_p, layer)
        y_mla = _attention(q, k, v, proj, batch, seq)
        rw_out = _rwkv(proj, v_first, mu_wa, mu_r, mu_k, mu_v, w0, w2p, a0, a2p, k_k, k_a, e_head,
                       v0, v1p, v2p, r_k3, ln_g3, ln_b3, e_pair, layer, batch, seq)
        y_rw = rw_out[0]
        if layer == 0:
            v_first = rw_out[1]
        if layer == depth - 1:
            (x2,) = _outproj(x2, y_ml, y_mla, y_rw, w_out_b, final_g3, layer, True)
        else:
            x2, h = _outproj(x2, y_ml, y_mla, y_rw, w_out_b, norm_g3, layer, False)
    return x2.reshape(batch, seq, D_MODEL)
```

```python
import functools

import jax
import jax.numpy as jnp
from jax import lax
from jax.experimental import pallas as pl
from jax.experimental.pallas import tpu as pltpu

F32 = jnp.float32
BF16 = jnp.bfloat16

D_MODEL = 2048
NORM_EPS = 1e-6
ML_HEADS = 4
ML_HEAD_DIM = 128
ML_WIDTH = 512
ML_CONV = 4
MLA_HEADS = 8
MLA_NOPE = 128
MLA_ROPE = 64
MLA_V = 128
MLA_WIDTH = 1024
MLA_Q_RANK = 512
MLA_KV_RANK = 256
ROPE_BASE = 10000.0
RW_HEAD_DIM = 64
RW_WIDTH = 512
RW_HEADS = 8
RW_DECAY_RANK = 64
RW_AAA_RANK = 64
RW_MV_RANK = 32
RW_GN_EPS = 64e-5

LANES = 128
SUBLANES = 8
MLA_QK_PAD = 256
VMEM_LIMIT = 48 * 1024 * 1024
NEG_BIG = -1e30
LOG2E = 1.4426950408889634

OFF_QK = 0
OFF_MLV = 1024
OFF_MLO = 1536
OFF_MLZ = 2048
OFF_CQ = 2560
OFF_CKV = 3072
OFF_KRG = 3328
OFF_WA = 3456
OFF_RWR = 3584
OFF_RWK = 4096
OFF_RWV = 4608
OFF_RWZ = 5120
OFF_MLAZ = 5632
D_IN_PAD = 6656

TM_PROJ = 1024
TN_PROJ = 1664
TM_OUT = 512
TQ = 512
ML_TT = 1024
ML_L = 256
RW_TT = 512
RW_L = 64


def _split_bf16(x):
    hi = x.astype(BF16)
    lo = (x - hi.astype(F32)).astype(BF16)
    return hi, lo


def _mm(a, b, passes=1):
    if passes == 1:
        return jnp.dot(a.astype(BF16), b.astype(BF16), preferred_element_type=F32)
    ah, al = _split_bf16(a)
    bh, bl = _split_bf16(b)
    return (jnp.dot(ah, bh, preferred_element_type=F32)
            + jnp.dot(ah, bl, preferred_element_type=F32)
            + jnp.dot(al, bh, preferred_element_type=F32))


_NT = (((1,), (1,)), ((), ()))


def _mm_nt(a, b, passes=1):
    if passes == 1:
        return lax.dot_general(a.astype(BF16), b.astype(BF16), _NT, preferred_element_type=F32)
    ah, al = _split_bf16(a)
    bh, bl = _split_bf16(b)
    return (lax.dot_general(ah, bh, _NT, preferred_element_type=F32)
            + lax.dot_general(ah, bl, _NT, preferred_element_type=F32)
            + lax.dot_general(al, bh, _NT, preferred_element_type=F32))


def _sigmoid(x):
    return 1.0 / (1.0 + jnp.exp(-x))


def _silu(x):
    return x * _sigmoid(x)


def _log_sigmoid(x):
    return jnp.minimum(x, 0.0) - jnp.log(1.0 + jnp.exp(-jnp.abs(x)))


def _params(*sem):
    return pltpu.CompilerParams(dimension_semantics=sem, vmem_limit_bytes=VMEM_LIMIT)


def _rms_scale(x, g):
    return x * lax.rsqrt(jnp.mean(x * x, axis=-1, keepdims=True) + NORM_EPS) * g


def _rmsnorm_kernel(x_ref, g_ref, h_ref):
    h_ref[...] = _rms_scale(x_ref[...], g_ref[...]).astype(BF16)


def _rmsnorm(x2, norm_g3, layer):
    m = x2.shape[0]
    return pl.pallas_call(
        _rmsnorm_kernel,
        grid=(m // TM_OUT,),
        in_specs=[pl.BlockSpec((TM_OUT, D_MODEL), lambda i: (i, 0)),
                  pl.BlockSpec((None, 1, D_MODEL), lambda i: (layer, 0, 0))],
        out_specs=pl.BlockSpec((TM_OUT, D_MODEL), lambda i: (i, 0)),
        out_shape=jax.ShapeDtypeStruct((m, D_MODEL), BF16),
        compiler_params=_params("parallel"),
        name="rmsnorm",
    )(x2, norm_g3)


def _inproj_kernel(h_ref, w_ref, o_ref):
    o_ref[...] = jnp.dot(h_ref[...], w_ref[...], preferred_element_type=F32)


def _inproj(h, w_in_p, layer):
    m = h.shape[0]
    return pl.pallas_call(
        _inproj_kernel,
        grid=(m // TM_PROJ, D_IN_PAD // TN_PROJ),
        in_specs=[
            pl.BlockSpec((TM_PROJ, D_MODEL), lambda i, j: (i, 0)),
            pl.BlockSpec((None, D_MODEL, TN_PROJ), lambda i, j: (layer, 0, j)),
        ],
        out_specs=pl.BlockSpec((TM_PROJ, TN_PROJ), lambda i, j: (i, j)),
        out_shape=jax.ShapeDtypeStruct((m, D_IN_PAD), F32),
        compiler_params=_params("parallel", "arbitrary"),
        name="inproj",
    )(h, w_in_p)


def _outproj_kernel(x_ref, yml_ref, ymla_ref, yrw_ref, w_ref, g_ref, *out_refs, final):
    acc = x_ref[...]
    acc = acc + jnp.dot(yml_ref[...], w_ref[0:ML_WIDTH, :], preferred_element_type=F32)
    acc = acc + jnp.dot(ymla_ref[...], w_ref[ML_WIDTH:ML_WIDTH + MLA_WIDTH, :],
                        preferred_element_type=F32)
    acc = acc + jnp.dot(yrw_ref[...], w_ref[ML_WIDTH + MLA_WIDTH:, :], preferred_element_type=F32)
    normed = _rms_scale(acc, g_ref[...])
    if final:
        out_refs[0][...] = normed
    else:
        out_refs[0][...] = acc
        out_refs[1][...] = normed.astype(BF16)


def _outproj(x2, y_ml, y_mla, y_rw, w_out_b, gains3, layer, final):
    m = x2.shape[0]
    g_index = 0 if final else layer + 1
    row = pl.BlockSpec((TM_OUT, D_MODEL), lambda i: (i, 0))
    out_specs = [row] if final else [row, row]
    out_shape = [jax.ShapeDtypeStruct((m, D_MODEL), F32)]
    if not final:
        out_shape.append(jax.ShapeDtypeStruct((m, D_MODEL), BF16))
    return pl.pallas_call(
        functools.partial(_outproj_kernel, final=final),
        grid=(m // TM_OUT,),
        in_specs=[
            row,
            pl.BlockSpec((TM_OUT, ML_WIDTH), lambda i: (i, 0)),
            pl.BlockSpec((TM_OUT, MLA_WIDTH), lambda i: (i, 0)),
            pl.BlockSpec((TM_OUT, RW_WIDTH), lambda i: (i, 0)),
            pl.BlockSpec((None, D_MODEL, D_MODEL), lambda i: (layer, 0, 0)),
            pl.BlockSpec((None, 1, D_MODEL), lambda i: (g_index, 0, 0)),
        ],
        out_specs=out_specs,
        out_shape=out_shape,
        compiler_params=_params("parallel"),
        name="outproj",
    )(x2, y_ml, y_mla, y_rw, w_out_b, gains3)


def _mla_proj_kernel(cq_ref, ckv_ref, krg_ref, cos_ref, s1_ref, s2_ref, qg_ref, wq_ref, kvg_ref,
                     wk_ref, wv_ref, q_out, k_out, v_out):
    cq = cq_ref[...]
    cqn = (cq * lax.rsqrt(jnp.mean(cq * cq, axis=-1, keepdims=True) + NORM_EPS)
           * qg_ref[...]).astype(BF16)
    q = jnp.dot(cqn, wq_ref[...], preferred_element_type=F32)
    ckv = ckv_ref[...]
    ckvn = (ckv * lax.rsqrt(jnp.mean(ckv * ckv, axis=-1, keepdims=True) + NORM_EPS)
            * kvg_ref[...]).astype(BF16)
    kn = jnp.dot(ckvn, wk_ref[...], preferred_element_type=F32)
    v = jnp.dot(ckvn, wv_ref[...], preferred_element_type=F32)
    cosf, s1, s2 = cos_ref[...], s1_ref[...], s2_ref[...]

    def rope(xb):
        return (xb * cosf + pltpu.roll(xb, LANES - MLA_ROPE // 2, 1) * s1
                + pltpu.roll(xb, MLA_ROPE // 2, 1) * s2)

    scale = (MLA_NOPE + MLA_ROPE) ** -0.5 * LOG2E
    kr = rope(krg_ref[...]).astype(BF16)
    for h in range(MLA_HEADS):
        c0 = h * MLA_QK_PAD
        q_out[:, c0:c0 + MLA_NOPE] = (q[:, c0:c0 + MLA_NOPE] * scale).astype(BF16)
        q_out[:, c0 + MLA_NOPE:c0 + MLA_QK_PAD] = (
            rope(q[:, c0 + MLA_NOPE:c0 + MLA_QK_PAD]) * scale).astype(BF16)
        k_out[:, c0:c0 + MLA_NOPE] = kn[:, h * MLA_NOPE:(h + 1) * MLA_NOPE].astype(BF16)
        k_out[:, c0 + MLA_NOPE:c0 + MLA_QK_PAD] = kr
    v_out[...] = v.T.astype(BF16)


def _mla_proj(proj, cosf, s1, s2, qg3, wq_p, kvg3, wk_p, wv_p, layer):
    m = proj.shape[0]
    tm = TQ
    row = lambda i: (i, 0)
    return pl.pallas_call(
        _mla_proj_kernel,
        grid=(m // tm,),
        in_specs=[
            pl.BlockSpec((tm, MLA_Q_RANK), lambda i: (i, OFF_CQ // MLA_Q_RANK)),
            pl.BlockSpec((tm, MLA_KV_RANK), lambda i: (i, OFF_CKV // MLA_KV_RANK)),
            pl.BlockSpec((tm, LANES), lambda i: (i, OFF_KRG // LANES)),
            pl.BlockSpec((tm, LANES), row),
            pl.BlockSpec((tm, LANES), row),
            pl.BlockSpec((tm, LANES), row),
            pl.BlockSpec((None, 1, MLA_Q_RANK), lambda i: (layer, 0, 0)),
            pl.BlockSpec((None, MLA_Q_RANK, MLA_HEADS * MLA_QK_PAD), lambda i: (layer, 0, 0)),
            pl.BlockSpec((None, 1, MLA_KV_RANK), lambda i: (layer, 0, 0)),
            pl.BlockSpec((None, MLA_KV_RANK, MLA_HEADS * MLA_NOPE), lambda i: (layer, 0, 0)),
            pl.BlockSpec((None, MLA_KV_RANK, MLA_HEADS * MLA_V), lambda i: (layer, 0, 0)),
        ],
        out_specs=[
            pl.BlockSpec((tm, MLA_HEADS * MLA_QK_PAD), row),
            pl.BlockSpec((tm, MLA_HEADS * MLA_QK_PAD), row),
            pl.BlockSpec((None, MLA_HEADS * MLA_V, tm), lambda i: (i, 0, 0)),
        ],
        out_shape=[
            jax.ShapeDtypeStruct((m, MLA_HEADS * MLA_QK_PAD), BF16),
            jax.ShapeDtypeStruct((m, MLA_HEADS * MLA_QK_PAD), BF16),
            jax.ShapeDtypeStruct((m // tm, MLA_HEADS * MLA_V, tm), BF16),
        ],
        compiler_params=_params("parallel"),
        name="mla_proj",
    )(proj, proj, proj, cosf, s1, s2, qg3, wq_p, kvg3, wk_p, wv_p)


ATTN_HEADS_PER_STEP = 4


def _attn_kernel(q_ref, k_ref, vt_ref, z_ref, o_ref, *, tq):
    i = pl.program_id(2)
    nh = ATTN_HEADS_PER_STEP
    th = tq // 2
    chains = [(h, u) for h in range(nh) for u in range(2)]

    def head_cols(h):
        return slice(h * MLA_QK_PAD, (h + 1) * MLA_QK_PAD)

    def softmax_pv(carry_c, s, vt):
        m, l, acc = carry_c
        m_new = jnp.maximum(m, jnp.max(s, axis=0, keepdims=True))
        alpha = jnp.exp2(m - m_new)
        p = jnp.exp2(s - m_new)
        l = alpha * l + jnp.sum(p, axis=0, keepdims=True)
        acc = alpha * acc + jnp.dot(vt, p.astype(BF16), preferred_element_type=F32)
        return m_new, l, acc

    def step(j, carry):
        off = pl.multiple_of(j * tq, tq)
        ss = []
        for h in range(nh):
            s = lax.dot_general(k_ref[pl.ds(off, tq), head_cols(h)], q_ref[:, head_cols(h)], _NT,
                                preferred_element_type=F32)
            ss += [s[:, :th], s[:, th:]]
        return tuple(softmax_pv(carry[n], ss[n], vt_ref[j, chains[n][0]])
                     for n in range(len(chains)))

    def diagonal_step(carry):
        off = pl.multiple_of(i * tq, tq)
        ss, vts = [], []
        for h, u in chains:
            nk = th if u == 0 else tq
            s = lax.dot_general(k_ref[pl.ds(off, nk), head_cols(h)],
                                q_ref[u * th:(u + 1) * th, head_cols(h)], _NT,
                                preferred_element_type=F32)
            kidx = lax.broadcasted_iota(jnp.int32, (nk, th), 0)
            qidx = lax.broadcasted_iota(jnp.int32, (nk, th), 1) + u * th
            ss.append(jnp.where(kidx <= qidx, s, NEG_BIG))
            vts.append(vt_ref[i, h][:, :nk])
        return tuple(softmax_pv(carry[n], ss[n], vts[n]) for n in range(len(chains)))

    init = tuple((jnp.full((1, th), NEG_BIG, F32), jnp.zeros((1, th), F32),
                  jnp.zeros((MLA_V, th), F32)) for _ in chains)
    carry = lax.fori_loop(0, i, step, init)
    carry = diagonal_step(carry)
    for n, (h, u) in enumerate(chains):
        _, l, acc = carry[n]
        y = (acc / l).T
        rs = slice(u * th, (u + 1) * th)
        cs = slice(h * MLA_V, (h + 1) * MLA_V)
        o_ref[rs, cs] = (_silu(z_ref[rs, cs]) * y).astype(BF16)


def _attention(q, k, v_t, proj, batch, seq):
    m = q.shape[0]
    nq = seq // TQ
    nh = ATTN_HEADS_PER_STEP
    v_t5 = v_t.reshape(batch, nq, MLA_HEADS, MLA_V, TQ)
    return pl.pallas_call(
        functools.partial(_attn_kernel, tq=TQ),
        grid=(batch, MLA_HEADS // nh, nq),
        in_specs=[
            pl.BlockSpec((TQ, nh * MLA_QK_PAD), lambda b, h, i: (b * nq + i, h)),
            pl.BlockSpec((seq, nh * MLA_QK_PAD), lambda b, h, i: (b, h)),
            pl.BlockSpec((None, nq, nh, MLA_V, TQ), lambda b, h, i: (b, 0, h, 0, 0)),
            pl.BlockSpec((TQ, nh * MLA_V),
                         lambda b, h, i: (b * nq + i, OFF_MLAZ // (nh * MLA_V) + h)),
        ],
        out_specs=pl.BlockSpec((TQ, nh * MLA_V), lambda b, h, i: (b * nq + i, h)),
        out_shape=jax.ShapeDtypeStruct((m, MLA_WIDTH), BF16),
        compiler_params=_params("parallel", "parallel", "arbitrary"),
        name="mla_attn",
    )(q, k, v_t5, proj)


def _shift_rows(x, prev8, s):
    rolled = pltpu.roll(x, s, 0)
    prev_rolled = pltpu.roll(prev8, s, 0)
    rid = lax.broadcasted_iota(jnp.int32, (SUBLANES, x.shape[1]), 0)
    top = jnp.where(rid < s, prev_rolled, rolled[0:SUBLANES])
    return jnp.concatenate([top, rolled[SUBLANES:]], axis=0)


def _mlstm_kernel(ib_ref, fb_ref, q_ref, k_ref, v_ref, o_ref, z_ref, g_ref, w_ref, b_ref, ng_ref,
                  out_ref, ct_ref, n_ref, m_ref, qp_ref, kp_ref, *, layer, tt, chunk):
    @pl.when(pl.program_id(1) == 0)
    def _():
        for ref in (ct_ref, n_ref, m_ref, qp_ref, kp_ref):
            ref[...] = jnp.zeros_like(ref)

    hd = ML_HEAD_DIM
    w_all = w_ref[...]
    b_all = b_ref[...]

    def conv_silu(x_ref, prev_ref, w, b):
        x = x_ref[...]
        prev8 = prev_ref[...]
        y = b + w[ML_CONV - 1:ML_CONV] * x
        for s in range(1, ML_CONV):
            y = y + w[ML_CONV - 1 - s:ML_CONV - s] * _shift_rows(x, prev8, s)
        prev_ref[...] = x[tt - SUBLANES:tt]
        return _silu(y)

    q_all = conv_silu(q_ref, qp_ref, w_all[:, :ML_WIDTH], b_all[:, :ML_WIDTH]) * (hd ** -0.5)
    k_all = conv_silu(k_ref, kp_ref, w_all[:, ML_WIDTH:], b_all[:, ML_WIDTH:])
    li_rows = [g_ref[h] + ib_ref[layer, h] for h in range(ML_HEADS)]
    lf_rows = [_log_sigmoid(g_ref[ML_HEADS + h] + fb_ref[layer, h]) for h in range(ML_HEADS)]

    L = chunk
    rows = lax.broadcasted_iota(jnp.int32, (L, L), 0)
    cols = lax.broadcasted_iota(jnp.int32, (L, L), 1)
    tri = rows >= cols
    eye = rows == cols
    for c in range(tt // L):
        sl = slice(c * L, (c + 1) * L)
        st = []
        for h in range(ML_HEADS):
            cs = slice(h * hd, (h + 1) * hd)
            li_row = li_rows[h][:, sl]
            lf_b = jnp.broadcast_to(lf_rows[h][:, sl], (L, L))
            li_b = jnp.broadcast_to(li_row, (L, L))
            b_col = jnp.sum(jnp.where(tri, lf_b, 0.0), axis=-1, keepdims=True)
            lf_col = jnp.sum(jnp.where(eye, lf_b, 0.0), axis=-1, keepdims=True)
            li_col = jnp.sum(jnp.where(eye, li_b, 0.0), axis=-1, keepdims=True)
            b_row = jnp.sum(jnp.where(rows <= cols, jnp.broadcast_to(lf_col, (L, L)), 0.0),
                            axis=0, keepdims=True)
            m_prev = m_ref[h][:, 0:1]
            dmat = jnp.where(tri, b_col - b_row + li_row, NEG_BIG)
            m_inter = b_col + m_prev
            m_t = jnp.maximum(m_inter, jnp.max(dmat, axis=-1, keepdims=True))
            st.append(dict(cs=cs, qc=q_all[sl, cs].astype(BF16), kc=k_all[sl, cs],
                           vc=v_ref[sl, cs].astype(BF16), pm=jnp.exp(dmat - m_t),
                           inter=jnp.exp(m_inter - m_t), m_t=m_t, b_col=b_col, li_col=li_col,
                           m_prev=m_prev))
        for s in st:
            s["smat"] = lax.dot_general(s["qc"], s["kc"].astype(BF16), _NT,
                                        preferred_element_type=F32) * s["pm"]
        for h, s in enumerate(st):
            ct = ct_ref[h]
            n_row = n_ref[h]
            num = (s["inter"] * jnp.dot(s["qc"], ct.astype(BF16), preferred_element_type=F32)
                   + jnp.dot(s["smat"].astype(BF16), s["vc"], preferred_element_type=F32))
            den = (s["inter"] * jnp.sum(s["qc"].astype(F32) * n_row, axis=-1, keepdims=True)
                   + jnp.sum(s["smat"], axis=-1, keepdims=True))
            s["hh"] = num / jnp.maximum(jnp.abs(den), jnp.exp(-s["m_t"]))
            b_last = s["b_col"][L - 1:L, :]
            g_col = b_last - s["b_col"] + s["li_col"]
            m_new = jnp.maximum(b_last + s["m_prev"], jnp.max(g_col, axis=0, keepdims=True))
            decay = jnp.exp(b_last + s["m_prev"] - m_new)
            kw = s["kc"] * jnp.exp(g_col - m_new)
            ct_ref[h] = decay * ct + jnp.dot(kw.T.astype(BF16), s["vc"],
                                             preferred_element_type=F32)
            n_ref[h] = decay * n_row + jnp.sum(kw, axis=0, keepdims=True)
            m_ref[h] = jnp.broadcast_to(m_new, (1, hd))
        for s in st:
            cs = s["cs"]
            hh = s["hh"]
            mu = jnp.mean(hh, axis=-1, keepdims=True)
            dd = hh - mu
            var = jnp.mean(dd * dd, axis=-1, keepdims=True)
            yn = dd * lax.rsqrt(var + NORM_EPS) * ng_ref[:, cs]
            out_ref[sl, cs] = (_silu(z_ref[sl, cs]) * (_sigmoid(o_ref[sl, cs]) * yn)).astype(BF16)


def _mlstm(proj, gates_t, i_bias, f_bias, conv_w, conv_b3, norm_g3, layer, batch, seq):
    m = proj.shape[0]
    tt = min(ML_TT, seq)
    nt = seq // tt
    hd = ML_HEAD_DIM
    w = ML_WIDTH

    def col(off):
        return pl.BlockSpec((tt, w), lambda b, t: (b * nt + t, off // w))

    smem = pl.BlockSpec(memory_space=pltpu.SMEM)
    return pl.pallas_call(
        functools.partial(_mlstm_kernel, layer=layer, tt=tt, chunk=min(ML_L, tt)),
        grid=(batch, nt),
        in_specs=[
            smem, smem,
            col(OFF_QK), col(OFF_QK + w), col(OFF_MLV), col(OFF_MLO), col(OFF_MLZ),
            pl.BlockSpec((None, 2 * ML_HEADS, 1, tt), lambda b, t: (b, 0, 0, t)),
            pl.BlockSpec((None, ML_CONV, 2 * w), lambda b, t: (layer, 0, 0)),
            pl.BlockSpec((None, 1, 2 * w), lambda b, t: (layer, 0, 0)),
            pl.BlockSpec((None, 1, w), lambda b, t: (layer, 0, 0)),
        ],
        out_specs=pl.BlockSpec((tt, w), lambda b, t: (b * nt + t, 0)),
        out_shape=jax.ShapeDtypeStruct((m, w), BF16),
        scratch_shapes=[
            pltpu.VMEM((ML_HEADS, hd, hd), F32),
            pltpu.VMEM((ML_HEADS, 1, hd), F32),
            pltpu.VMEM((ML_HEADS, 1, hd), F32),
            pltpu.VMEM((SUBLANES, w), F32),
            pltpu.VMEM((SUBLANES, w), F32),
        ],
        compiler_params=_params("parallel", "arbitrary"),
        name="mlstm",
    )(i_bias, f_bias, proj, proj, proj, proj, proj, gates_t, conv_w, conv_b3, norm_g3)


RW_PRE_PASSES = 3


def _rwkv_pre_kernel(*refs, has_vres, tm):
    if has_vres:
        (wa_ref, r_ref, k_ref, v_ref, mu_wa, mu_r, mu_k, mu_v, w0_ref, w2_ref, a0_ref, a2_ref,
         kk_ref, ka_ref, e_ref, vf_ref, v0_ref, v1_ref, v2_ref,
         r_out, ld_out, kh_out, v_out, kn_out, b_out, c_wa, c_r, c_k, c_v) = refs
    else:
        (wa_ref, r_ref, k_ref, v_ref, mu_wa, mu_r, mu_k, mu_v, w0_ref, w2_ref, a0_ref, a2_ref,
         kk_ref, ka_ref, e_ref,
         r_out, ld_out, kh_out, v_out, kn_out, b_out, c_wa, c_r, c_k, c_v) = refs

    @pl.when(pl.program_id(1) == 0)
    def _():
        for c in (c_wa, c_r, c_k, c_v):
            c[...] = jnp.zeros_like(c)

    def mix(x_ref, c_ref, mu_ref):
        x = x_ref[...]
        rolled = pltpu.roll(x, 1, 0)
        rid = lax.broadcasted_iota(jnp.int32, (SUBLANES, x.shape[1]), 0)
        prev_last = jnp.broadcast_to(c_ref[SUBLANES - 1:SUBLANES, :], (SUBLANES, x.shape[1]))
        top = jnp.where(rid == 0, prev_last, rolled[0:SUBLANES])
        xprev = jnp.concatenate([top, rolled[SUBLANES:]], axis=0)
        c_ref[...] = x[tm - SUBLANES:tm]
        return x + mu_ref[...] * (xprev - x)

    xwa = mix(wa_ref, c_wa, mu_wa)
    r = mix(r_ref, c_r, mu_r)
    k = mix(k_ref, c_k, mu_k)
    v = mix(v_ref, c_v, mu_v)
    zw = w0_ref[...] + _mm(jnp.tanh(xwa), w2_ref[...], RW_PRE_PASSES)
    za = a0_ref[...] + _mm(xwa, a2_ref[...], RW_PRE_PASSES)
    log_w = _log_sigmoid(zw) - 0.5
    ld_out[...] = -jnp.exp(log_w)
    a = _sigmoid(za)
    if has_vres:
        gate = _sigmoid(v0_ref[...] + _mm(_mm(v, v1_ref[...], RW_PRE_PASSES), v2_ref[...],
                                          RW_PRE_PASSES))
        v = v + (vf_ref[...] - v) * gate
    kk = k * kk_ref[...]
    ss = _mm(kk * kk, e_ref[...])
    kn = kk / jnp.maximum(jnp.sqrt(ss), 1e-12)
    r_out[...] = r
    kh_out[...] = k * (1.0 + (a - 1.0) * ka_ref[...])
    v_out[...] = v
    kn_out[...] = kn
    b_out[...] = kn * a


RW_GROUP = 2


def _rwkv_scan_kernel(r_ref, ld_ref, kh_ref, v_ref, kn_ref, b_ref, z_ref, rk_ref, g_ref, bias_ref,
                      e_ref, out_ref, m_ref, ac_ref, gc_ref, qs_ref, y_ref, *, tt, chunk):
    @pl.when(pl.program_id(1) == 0)
    def _():
        m_ref[...] = jnp.zeros_like(m_ref)

    L = chunk
    L2 = 2 * L
    nchunk = tt // L
    npair = RW_WIDTH // LANES
    rows = lax.broadcasted_iota(jnp.int32, (L, L), 0)
    cols = lax.broadcasted_iota(jnp.int32, (L, L), 1)
    tri_b = jnp.where(rows >= cols, 1.0, 0.0).astype(BF16)
    lane = lax.broadcasted_iota(jnp.int32, (1, LANES), 1)
    m0 = jnp.where(lane < RW_HEAD_DIM, 1.0, 0.0)
    m1 = 1.0 - m0
    r2 = lax.broadcasted_iota(jnp.int32, (L2, L2), 0)
    c2 = lax.broadcasted_iota(jnp.int32, (L2, L2), 1)
    same_blk = (r2 < L) == (c2 < L)
    t2 = jnp.where(r2 < L, r2, r2 - L)
    s2 = jnp.where(c2 < L, c2, c2 - L)
    mask_strict = jnp.logical_and(same_blk, t2 > s2)
    mask_incl = jnp.logical_and(same_blk, t2 >= s2)
    eye2 = lax.broadcasted_iota(jnp.int32, (LANES, LANES), 0) == lax.broadcasted_iota(
        jnp.int32, (LANES, LANES), 1)
    n_double = max(1, (L - 1).bit_length())

    def stack(x):
        return jnp.concatenate([x * m0, x * m1], axis=0)

    def prepare(p, c):
        sl = pl.ds(pl.multiple_of(c * L, L), L)
        cs = slice(p * LANES, (p + 1) * LANES)
        ld = ld_ref[sl, cs]
        r, kh, v, kn, b = r_ref[sl, cs], kh_ref[sl, cs], v_ref[sl, cs], kn_ref[sl, cs], b_ref[sl, cs]
        ld_hi, ld_lo = _split_bf16(ld)
        logp = (jnp.dot(tri_b, ld_hi, preferred_element_type=F32)
                + jnp.dot(tri_b, ld_lo, preferred_element_type=F32))
        cmid = logp[L // 2 - 1:L // 2, :]
        e = logp - cmid
        e_last = e[L - 1:L, :]
        p_mid = jnp.exp(cmid)
        at_s = stack(-kn * jnp.exp(e - ld))
        rt_s = stack(r * jnp.exp(e))
        inv = jnp.exp(-e)
        tail = jnp.exp(e_last - e)
        lhs = jnp.concatenate([at_s, rt_s], axis=0).astype(BF16)
        rhs = jnp.concatenate([stack(b * inv), stack(kh * inv)], axis=0).astype(BF16)
        hat_t = jnp.concatenate([stack(b * tail), stack(kh * tail)], axis=0).T.astype(BF16)
        return dict(sl=sl, cs=cs, idx=p * nchunk + c, lhs=lhs, rhs=rhs, hat_t=hat_t,
                    at_true=at_s * p_mid, rt_true=rt_s * p_mid, v_s=stack(v).astype(BF16),
                    p_last=jnp.exp(e_last + cmid))

    def coeff_body(cg, carry):
        insts = [prepare(p, cg * RW_GROUP + g) for p in range(npair) for g in range(RW_GROUP)]
        for s in insts:
            aa = lax.dot_general(s["lhs"], s["rhs"], _NT, preferred_element_type=F32)
            s["nmat"] = jnp.where(mask_strict, aa[0:L2, 0:L2], 0.0).astype(BF16)
            s["a_ak"] = jnp.where(mask_strict, aa[0:L2, L2:], 0.0).astype(BF16)
            s["a_r"] = jnp.concatenate([jnp.where(mask_incl, aa[L2:, 0:L2], 0.0),
                                        jnp.where(mask_incl, aa[L2:, L2:], 0.0)],
                                       axis=1).astype(BF16)
        for s in insts:
            s["x"] = jnp.concatenate(
                [s["at_true"], jnp.dot(s["a_ak"], s["v_s"], preferred_element_type=F32)],
                axis=1)
        for it in range(n_double):
            for s in insts:
                s["x"] = s["x"] + jnp.dot(s["nmat"], s["x"].astype(BF16),
                                          preferred_element_type=F32)
            if it + 1 < n_double:
                for s in insts:
                    s["nmat"] = jnp.dot(s["nmat"], s["nmat"],
                                        preferred_element_type=F32).astype(BF16)
        for s in insts:
            big_l = jnp.concatenate([s["a_r"], s["hat_t"]], axis=0)
            big_r = jnp.concatenate(
                [s["x"].astype(BF16),
                 jnp.concatenate([jnp.zeros((L2, LANES), BF16), s["v_s"]], axis=1)], axis=0)
            res = jnp.dot(big_l, big_r, preferred_element_type=F32)
            q_s = s["rt_true"] + res[0:L2, :LANES]
            y0_s = res[0:L2, LANES:]
            qs_ref[s["sl"], s["cs"]] = q_s[0:L] + q_s[L:]
            y_ref[s["sl"], s["cs"]] = y0_s[0:L] + y0_s[L:]
            ac_ref[s["idx"]] = (jnp.where(eye2, jnp.broadcast_to(s["p_last"], (LANES, LANES)), 0.0)
                                + res[L2:, :LANES])
            gc_ref[s["idx"]] = res[L2:, LANES:]
        return carry

    lax.fori_loop(0, nchunk // RW_GROUP, coeff_body, 0)

    def chain_body(c, carry):
        sl = pl.ds(pl.multiple_of(c * L, L), L)
        for p in range(npair):
            cs = slice(p * LANES, (p + 1) * LANES)
            idx = p * nchunk + c
            mp = m_ref[p]
            y_ref[sl, cs] = _mm(qs_ref[sl, cs], mp) + y_ref[sl, cs]
            m_ref[p] = _mm(ac_ref[idx], mp) + gc_ref[idx]
        return carry

    lax.fori_loop(0, nchunk, chain_body, 0)

    e_b = e_ref[...]
    inv_n = 1.0 / RW_HEAD_DIM

    def head_sum(x):
        return jnp.dot(x.astype(BF16), e_b, preferred_element_type=F32)

    for p in range(npair):
        cs = slice(p * LANES, (p + 1) * LANES)
        y = y_ref[:, cs]
        mu = head_sum(y) * inv_n
        d = y - mu
        var = head_sum(d * d) * inv_n
        yn = d * lax.rsqrt(var + RW_GN_EPS) * g_ref[:, cs] + bias_ref[:, cs]
        v_all = v_ref[:, cs]
        bonus = head_sum(r_ref[:, cs] * kh_ref[:, cs] * rk_ref[:, cs])
        out_ref[:, cs] = (_silu(z_ref[:, cs]) * (yn + bonus * v_all)).astype(BF16)


def _rwkv_kernel(*refs, has_vres, tt, chunk):
    n_mix = 19 if has_vres else 15
    mix_in = refs[:n_mix]
    z_ref, rk_ref, g_ref, bias_ref, e_ref = refs[n_mix:n_mix + 5]
    n_out = 1 if has_vres else 2
    outs = refs[n_mix + 5:n_mix + 5 + n_out]
    (c_wa, c_r, c_k, c_v, r_s, ld_s, kh_s, v_s, kn_s, b_s,
     m_ref, ac_ref, gc_ref, qs_ref, y_ref) = refs[n_mix + 5 + n_out:]
    _rwkv_pre_kernel(*mix_in, r_s, ld_s, kh_s, v_s, kn_s, b_s, c_wa, c_r, c_k, c_v,
                     has_vres=has_vres, tm=tt)
    if not has_vres:
        outs[1][...] = v_s[...]
    _rwkv_scan_kernel(r_s, ld_s, kh_s, v_s, kn_s, b_s, z_ref, rk_ref, g_ref, bias_ref, e_ref,
                      outs[0], m_ref, ac_ref, gc_ref, qs_ref, y_ref, tt=tt, chunk=chunk)


def _rwkv(proj, v_first, mu_wa, mu_r, mu_k, mu_v, w0, w2p, a0, a2p, k_k, k_a, e_head,
          v0, v1p, v2p, r_k3, ln_g3, ln_b3, e_pair, layer, batch, seq):
    m = proj.shape[0]
    tt = min(RW_TT, seq)
    nt = seq // tt
    chunk = min(RW_L, tt)
    has_vres = layer > 0
    w = RW_WIDTH
    npair = w // LANES

    def colblk(off, width):
        return pl.BlockSpec((tt, width), lambda b, t: (b * nt + t, off // width))

    def lay(shape):
        return pl.BlockSpec((None,) + shape, lambda b, t: (layer,) + (0,) * len(shape))

    def lay1(shape):
        return pl.BlockSpec((None,) + shape, lambda b, t: (layer - 1,) + (0,) * len(shape))

    row = pl.BlockSpec((tt, w), lambda b, t: (b * nt + t, 0))
    in_specs = [colblk(OFF_WA, LANES), colblk(OFF_RWR, w), colblk(OFF_RWK, w), colblk(OFF_RWV, w),
                lay((1, LANES)), lay((1, w)), lay((1, w)), lay((1, w)),
                lay((1, w)), lay((LANES, w)), lay((1, w)), lay((LANES, w)),
                lay((1, w)), lay((1, w)),
                pl.BlockSpec((w, w), lambda b, t: (0, 0))]
    args = [proj, proj, proj, proj, mu_wa, mu_r, mu_k, mu_v, w0, w2p, a0, a2p, k_k, k_a, e_head]
    if has_vres:
        in_specs += [row, lay1((1, w)), lay1((w, LANES)), lay1((LANES, w))]
        args += [v_first, v0, v1p, v2p]
    in_specs += [colblk(OFF_RWZ, w), lay((1, w)), lay((1, w)), lay((1, w)),
                 pl.BlockSpec((LANES, LANES), lambda b, t: (0, 0))]
    args += [proj, r_k3, ln_g3, ln_b3, e_pair]
    out_specs = [row]
    out_shape = [jax.ShapeDtypeStruct((m, w), BF16)]
    if not has_vres:
        out_specs.append(row)
        out_shape.append(jax.ShapeDtypeStruct((m, w), F32))
    scratch = [pltpu.VMEM((SUBLANES, LANES), F32)] + [pltpu.VMEM((SUBLANES, w), F32)] * 3
    scratch += [pltpu.VMEM((tt, w), F32)] * 6
    scratch += [
        pltpu.VMEM((npair, LANES, LANES), F32),
        pltpu.VMEM((npair * (tt // chunk), LANES, LANES), F32),
        pltpu.VMEM((npair * (tt // chunk), LANES, LANES), F32),
        pltpu.VMEM((tt, w), F32),
        pltpu.VMEM((tt, w), F32),
    ]
    return pl.pallas_call(
        functools.partial(_rwkv_kernel, has_vres=has_vres, tt=tt, chunk=chunk),
        grid=(batch, nt),
        in_specs=in_specs,
        out_specs=out_specs,
        out_shape=out_shape,
        scratch_shapes=scratch,
        compiler_params=_params("parallel", "arbitrary"),
        name="rwkv",
    )(*args)


def _permute_w_in(w_in):
    w_in = w_in.astype(BF16)

    def s(a, n):
        return w_in[:, :, a:a + n]

    pad = jnp.zeros(w_in.shape[:2] + (LANES - MLA_ROPE - 2 * ML_HEADS,), w_in.dtype)
    parts = [s(0, 1024), s(1024, 512), s(1544, 512), s(2056, 512), s(2568, 512), s(3080, 256),
             s(3336, 64), s(1536, 8), pad, s(5960, 128), s(4424, 512), s(4936, 512), s(5448, 512),
             s(6088, 512), s(3400, 1024)]
    return jnp.concatenate(parts, axis=-1)


def _permute_w_uq(w_uq):
    d = w_uq.shape[0]
    w = w_uq.reshape(d, MLA_Q_RANK, MLA_HEADS, MLA_NOPE + MLA_ROPE)
    w = jnp.pad(w, ((0, 0), (0, 0), (0, 0), (0, MLA_QK_PAD - MLA_NOPE - MLA_ROPE)))
    return w.reshape(d, MLA_Q_RANK, MLA_HEADS * MLA_QK_PAD).astype(BF16)


def _split_w_ukv(w_ukv):
    d = w_ukv.shape[0]
    w = w_ukv.reshape(d, MLA_KV_RANK, MLA_HEADS, MLA_NOPE + MLA_V)
    wk = w[..., :MLA_NOPE].reshape(d, MLA_KV_RANK, MLA_HEADS * MLA_NOPE)
    wv = w[..., MLA_NOPE:].reshape(d, MLA_KV_RANK, MLA_HEADS * MLA_V)
    return wk.astype(BF16), wv.astype(BF16)


def _rope_tables(positions):
    inv_freq = jnp.power(ROPE_BASE, -jnp.arange(0, MLA_ROPE, 2, dtype=F32) / MLA_ROPE)
    ang = positions.astype(F32).reshape(-1, 1) * inv_freq
    cos, sin = jnp.cos(ang), jnp.sin(ang)
    z32 = jnp.zeros_like(cos)
    z64 = jnp.zeros((ang.shape[0], LANES - MLA_ROPE), F32)
    cosf = jnp.concatenate([cos, cos, z64], axis=-1)
    s1 = jnp.concatenate([-sin, z32, z64], axis=-1)
    s2 = jnp.concatenate([z32, sin, z64], axis=-1)
    return cosf, s1, s2


def kernel(x, positions, norm_g, w_in, ml_conv_w, ml_conv_b, ml_i_bias, ml_f_bias, ml_norm_g,
           mla_q_norm_g, mla_w_uq, mla_kv_norm_g, mla_w_ukv, rw_mu, rw_w0, rw_w2, rw_a0, rw_a2,
           rw_v0, rw_v1, rw_v2, rw_k_k, rw_k_a, rw_r_k, rw_ln_g, rw_ln_b, w_out, final_norm_g):
    batch, seq, _ = x.shape
    depth = w_in.shape[0]
    m = batch * seq
    x2 = x.reshape(m, D_MODEL)

    w_in_p = _permute_w_in(w_in)
    w_uq_p = _permute_w_uq(mla_w_uq)
    w_uk_p, w_uv_p = _split_w_ukv(mla_w_ukv)
    w_out_b = w_out.astype(BF16)
    cosf, s1, s2 = _rope_tables(positions)

    def row3(a):
        return a.reshape(a.shape[0], 1, a.shape[1])

    norm_g3 = row3(norm_g)
    conv_b3 = row3(ml_conv_b)
    ml_norm_g3 = row3(ml_norm_g)
    qg3, kvg3 = row3(mla_q_norm_g), row3(mla_kv_norm_g)
    w = RW_WIDTH
    mu_r, mu_k, mu_v = row3(rw_mu[:, 0:w]), row3(rw_mu[:, w:2 * w]), row3(rw_mu[:, 2 * w:3 * w])
    mu_wa = row3(rw_mu[:, 3 * w:])
    w0, a0, k_k, k_a = row3(rw_w0), row3(rw_a0), row3(rw_k_k), row3(rw_k_a)
    w2p = jnp.pad(rw_w2, ((0, 0), (0, LANES - RW_DECAY_RANK), (0, 0)))
    a2p = jnp.pad(rw_a2, ((0, 0), (LANES - RW_AAA_RANK, 0), (0, 0)))
    v0 = row3(rw_v0)
    v1p = jnp.pad(rw_v1, ((0, 0), (0, 0), (0, LANES - RW_MV_RANK)))
    v2p = jnp.pad(rw_v2, ((0, 0), (0, LANES - RW_MV_RANK), (0, 0)))
    r_k3 = rw_r_k.reshape(depth, 1, w)
    ln_g3, ln_b3 = row3(rw_ln_g), row3(rw_ln_b)
    hid = jnp.arange(w) // RW_HEAD_DIM
    e_head = (hid[:, None] == hid[None, :]).astype(F32)
    e_pair = e_head[:LANES, :LANES].astype(BF16)
    final_g3 = final_norm_g.reshape(1, 1, D_MODEL)

    v_first = None
    h = _rmsnorm(x2, norm_g3, 0)
    for layer in range(depth):
        proj = _inproj(h, w_in_p, layer)
        gates_t = proj[:, OFF_KRG + MLA_ROPE:OFF_KRG + MLA_ROPE + 2 * ML_HEADS]
        gates_t = gates_t.reshape(batch, seq, 2 * ML_HEADS).transpose(0, 2, 1)
        gates_t = gates_t.reshape(batch, 2 * ML_HEADS, 1, seq)
        y_ml = _mlstm(proj, gates_t, ml_i_bias, ml_f_bias, ml_conv_w, conv_b3, ml_norm_g3,
                      layer, batch, seq)
        q, k, v = _mla_proj(proj, cosf, s1, s2, qg3, w_uq_p, kvg3, w_uk_p, w_uv_p, layer)
        y_mla = _attention(q, k, v, proj, batch, seq)
        rw_out = _rwkv(proj, v_first, mu_wa, mu_r, mu_k, mu_v, w0, w2p, a0, a2p, k_k, k_a, e_head,
                       v0, v1p, v2p, r_k3, ln_g3, ln_b3, e_pair, layer, batch, seq)
        y_rw = rw_out[0]
        if layer == 0:
            v_first = rw_out[1]
        if layer == depth - 1:
            (x2,) = _outproj(x2, y_ml, y_mla, y_rw, w_out_b, final_g3, layer, True)
        else:
            x2, h = _outproj(x2, y_ml, y_mla, y_rw, w_out_b, norm_g3, layer, False)
    return x2.reshape(batch, seq, D_MODEL)
```

```python
import functools

import jax
import jax.numpy as jnp
from jax import lax
from jax.experimental import pallas as pl
from jax.experimental.pallas import tpu as pltpu

F32 = jnp.float32
BF16 = jnp.bfloat16

D_MODEL = 2048
NORM_EPS = 1e-6
ML_HEADS = 4
ML_HEAD_DIM = 128
ML_WIDTH = 512
ML_CONV = 4
MLA_HEADS = 8
MLA_NOPE = 128
MLA_ROPE = 64
MLA_V = 128
MLA_WIDTH = 1024
MLA_Q_RANK = 512
MLA_KV_RANK = 256
ROPE_BASE = 10000.0
RW_HEAD_DIM = 64
RW_WIDTH = 512
RW_HEADS = 8
RW_DECAY_RANK = 64
RW_AAA_RANK = 64
RW_MV_RANK = 32
RW_GN_EPS = 64e-5

LANES = 128
SUBLANES = 8
MLA_QK_PAD = 256
VMEM_LIMIT = 48 * 1024 * 1024
NEG_BIG = -1e30
LOG2E = 1.4426950408889634

OFF_QK = 0
OFF_MLV = 1024
OFF_MLO = 1536
OFF_MLZ = 2048
OFF_CQ = 2560
OFF_CKV = 3072
OFF_KRG = 3328
OFF_WA = 3456
OFF_RWR = 3584
OFF_RWK = 4096
OFF_RWV = 4608
OFF_RWZ = 5120
OFF_MLAZ = 5632
D_IN_PAD = 6656

TM_PROJ = 1024
TN_PROJ = 1664
TM_OUT = 512
TQ = 512
ML_TT = 1024
ML_L = 256
RW_TT = 512
RW_L = 64


def _split_bf16(x):
    hi = x.astype(BF16)
    lo = (x - hi.astype(F32)).astype(BF16)
    return hi, lo


def _mm(a, b, passes=1):
    if passes == 1:
        return jnp.dot(a.astype(BF16), b.astype(BF16), preferred_element_type=F32)
    ah, al = _split_bf16(a)
    bh, bl = _split_bf16(b)
    return (jnp.dot(ah, bh, preferred_element_type=F32)
            + jnp.dot(ah, bl, preferred_element_type=F32)
            + jnp.dot(al, bh, preferred_element_type=F32))


_NT = (((1,), (1,)), ((), ()))


def _mm_nt(a, b, passes=1):
    if passes == 1:
        return lax.dot_general(a.astype(BF16), b.astype(BF16), _NT, preferred_element_type=F32)
    ah, al = _split_bf16(a)
    bh, bl = _split_bf16(b)
    return (lax.dot_general(ah, bh, _NT, preferred_element_type=F32)
            + lax.dot_general(ah, bl, _NT, preferred_element_type=F32)
            + lax.dot_general(al, bh, _NT, preferred_element_type=F32))


def _sigmoid(x):
    return 0.5 * jnp.tanh(0.5 * x) + 0.5


def _silu(x):
    h = 0.5 * x
    return h * jnp.tanh(h) + h


def _log_sigmoid(x):
    return jnp.minimum(x, 0.0) - jnp.log(1.0 + jnp.exp(-jnp.abs(x)))


def _params(*sem):
    return pltpu.CompilerParams(dimension_semantics=sem, vmem_limit_bytes=VMEM_LIMIT)


def _rms_scale(x, g):
    return x * lax.rsqrt(jnp.mean(x * x, axis=-1, keepdims=True) + NORM_EPS) * g


def _rmsnorm_kernel(x_ref, g_ref, h_ref):
    h_ref[...] = _rms_scale(x_ref[...], g_ref[...]).astype(BF16)


def _rmsnorm(x2, norm_g3, layer):
    m = x2.shape[0]
    return pl.pallas_call(
        _rmsnorm_kernel,
        grid=(m // TM_OUT,),
        in_specs=[pl.BlockSpec((TM_OUT, D_MODEL), lambda i: (i, 0)),
                  pl.BlockSpec((None, 1, D_MODEL), lambda i: (layer, 0, 0))],
        out_specs=pl.BlockSpec((TM_OUT, D_MODEL), lambda i: (i, 0)),
        out_shape=jax.ShapeDtypeStruct((m, D_MODEL), BF16),
        compiler_params=_params("parallel"),
        name="rmsnorm",
    )(x2, norm_g3)


def _inproj_kernel(h_ref, w_ref, o_ref):
    o_ref[...] = jnp.dot(h_ref[...], w_ref[...], preferred_element_type=F32)


def _inproj(h, w_in_p, layer):
    m = h.shape[0]
    return pl.pallas_call(
        _inproj_kernel,
        grid=(m // TM_PROJ, D_IN_PAD // TN_PROJ),
        in_specs=[
            pl.BlockSpec((TM_PROJ, D_MODEL), lambda i, j: (i, 0)),
            pl.BlockSpec((None, D_MODEL, TN_PROJ), lambda i, j: (layer, 0, j)),
        ],
        out_specs=pl.BlockSpec((TM_PROJ, TN_PROJ), lambda i, j: (i, j)),
        out_shape=jax.ShapeDtypeStruct((m, D_IN_PAD), F32),
        compiler_params=_params("parallel", "arbitrary"),
        name="inproj",
    )(h, w_in_p)


def _outproj_kernel(x_ref, yml_ref, ymla_ref, yrw_ref, w_ref, g_ref, *out_refs, final):
    acc = x_ref[...]
    acc = acc + jnp.dot(yml_ref[...], w_ref[0:ML_WIDTH, :], preferred_element_type=F32)
    acc = acc + jnp.dot(ymla_ref[...], w_ref[ML_WIDTH:ML_WIDTH + MLA_WIDTH, :],
                        preferred_element_type=F32)
    acc = acc + jnp.dot(yrw_ref[...], w_ref[ML_WIDTH + MLA_WIDTH:, :], preferred_element_type=F32)
    normed = _rms_scale(acc, g_ref[...])
    if final:
        out_refs[0][...] = normed
    else:
        out_refs[0][...] = acc
        out_refs[1][...] = normed.astype(BF16)


def _outproj(x2, y_ml, y_mla, y_rw, w_out_b, gains3, layer, final):
    m = x2.shape[0]
    g_index = 0 if final else layer + 1
    row = pl.BlockSpec((TM_OUT, D_MODEL), lambda i: (i, 0))
    out_specs = [row] if final else [row, row]
    out_shape = [jax.ShapeDtypeStruct((m, D_MODEL), F32)]
    if not final:
        out_shape.append(jax.ShapeDtypeStruct((m, D_MODEL), BF16))
    return pl.pallas_call(
        functools.partial(_outproj_kernel, final=final),
        grid=(m // TM_OUT,),
        in_specs=[
            row,
            pl.BlockSpec((TM_OUT, ML_WIDTH), lambda i: (i, 0)),
            pl.BlockSpec((TM_OUT, MLA_WIDTH), lambda i: (i, 0)),
            pl.BlockSpec((TM_OUT, RW_WIDTH), lambda i: (i, 0)),
            pl.BlockSpec((None, D_MODEL, D_MODEL), lambda i: (layer, 0, 0)),
            pl.BlockSpec((None, 1, D_MODEL), lambda i: (g_index, 0, 0)),
        ],
        out_specs=out_specs,
        out_shape=out_shape,
        compiler_params=_params("parallel"),
        name="outproj",
    )(x2, y_ml, y_mla, y_rw, w_out_b, gains3)


def _mla_proj_kernel(cq_ref, ckv_ref, krg_ref, cos_ref, s1_ref, s2_ref, qg_ref, wq_ref, kvg_ref,
                     wk_ref, wv_ref, q_out, k_out, v_out):
    cq = cq_ref[...]
    cqn = (cq * lax.rsqrt(jnp.mean(cq * cq, axis=-1, keepdims=True) + NORM_EPS)
           * qg_ref[...]).astype(BF16)
    q = jnp.dot(cqn, wq_ref[...], preferred_element_type=F32)
    ckv = ckv_ref[...]
    ckvn = (ckv * lax.rsqrt(jnp.mean(ckv * ckv, axis=-1, keepdims=True) + NORM_EPS)
            * kvg_ref[...]).astype(BF16)
    kn = jnp.dot(ckvn, wk_ref[...], preferred_element_type=F32)
    v = jnp.dot(ckvn, wv_ref[...], preferred_element_type=F32)
    cosf, s1, s2 = cos_ref[...], s1_ref[...], s2_ref[...]

    def rope(xb):
        return (xb * cosf + pltpu.roll(xb, LANES - MLA_ROPE // 2, 1) * s1
                + pltpu.roll(xb, MLA_ROPE // 2, 1) * s2)

    scale = (MLA_NOPE + MLA_ROPE) ** -0.5 * LOG2E
    kr = rope(krg_ref[...]).astype(BF16)
    for h in range(MLA_HEADS):
        c0 = h * MLA_QK_PAD
        q_out[:, c0:c0 + MLA_NOPE] = (q[:, c0:c0 + MLA_NOPE] * scale).astype(BF16)
        q_out[:, c0 + MLA_NOPE:c0 + MLA_QK_PAD] = (
            rope(q[:, c0 + MLA_NOPE:c0 + MLA_QK_PAD]) * scale).astype(BF16)
        k_out[:, c0:c0 + MLA_NOPE] = kn[:, h * MLA_NOPE:(h + 1) * MLA_NOPE].astype(BF16)
        k_out[:, c0 + MLA_NOPE:c0 + MLA_QK_PAD] = kr
    v_out[...] = v.T.astype(BF16)


def _mla_proj(proj, cosf, s1, s2, qg3, wq_p, kvg3, wk_p, wv_p, layer):
    m = proj.shape[0]
    tm = TQ
    row = lambda i: (i, 0)
    return pl.pallas_call(
        _mla_proj_kernel,
        grid=(m // tm,),
        in_specs=[
            pl.BlockSpec((tm, MLA_Q_RANK), lambda i: (i, OFF_CQ // MLA_Q_RANK)),
            pl.BlockSpec((tm, MLA_KV_RANK), lambda i: (i, OFF_CKV // MLA_KV_RANK)),
            pl.BlockSpec((tm, LANES), lambda i: (i, OFF_KRG // LANES)),
            pl.BlockSpec((tm, LANES), row),
            pl.BlockSpec((tm, LANES), row),
            pl.BlockSpec((tm, LANES), row),
            pl.BlockSpec((None, 1, MLA_Q_RANK), lambda i: (layer, 0, 0)),
            pl.BlockSpec((None, MLA_Q_RANK, MLA_HEADS * MLA_QK_PAD), lambda i: (layer, 0, 0)),
            pl.BlockSpec((None, 1, MLA_KV_RANK), lambda i: (layer, 0, 0)),
            pl.BlockSpec((None, MLA_KV_RANK, MLA_HEADS * MLA_NOPE), lambda i: (layer, 0, 0)),
            pl.BlockSpec((None, MLA_KV_RANK, MLA_HEADS * MLA_V), lambda i: (layer, 0, 0)),
        ],
        out_specs=[
            pl.BlockSpec((tm, MLA_HEADS * MLA_QK_PAD), row),
            pl.BlockSpec((tm, MLA_HEADS * MLA_QK_PAD), row),
            pl.BlockSpec((None, MLA_HEADS * MLA_V, tm), lambda i: (i, 0, 0)),
        ],
        out_shape=[
            jax.ShapeDtypeStruct((m, MLA_HEADS * MLA_QK_PAD), BF16),
            jax.ShapeDtypeStruct((m, MLA_HEADS * MLA_QK_PAD), BF16),
            jax.ShapeDtypeStruct((m // tm, MLA_HEADS * MLA_V, tm), BF16),
        ],
        compiler_params=_params("parallel"),
        name="mla_proj",
    )(proj, proj, proj, cosf, s1, s2, qg3, wq_p, kvg3, wk_p, wv_p)


ATTN_HEADS_PER_STEP = 4


def _attn_kernel(q_ref, k_ref, vt_ref, z_ref, o_ref, *, tq):
    i = pl.program_id(2)
    nh = ATTN_HEADS_PER_STEP
    th = tq // 2
    chains = [(h, u) for h in range(nh) for u in range(2)]

    def head_cols(h):
        return slice(h * MLA_QK_PAD, (h + 1) * MLA_QK_PAD)

    def softmax_pv(carry_c, s, vt):
        m, l, acc = carry_c
        m_new = jnp.maximum(m, jnp.max(s, axis=0, keepdims=True))
        alpha = jnp.exp2(m - m_new)
        p = jnp.exp2(s - m_new)
        l = alpha * l + jnp.sum(p, axis=0, keepdims=True)
        acc = alpha * acc + jnp.dot(vt, p.astype(BF16), preferred_element_type=F32)
        return m_new, l, acc

    def step(j, carry):
        off = pl.multiple_of(j * tq, tq)
        ss = []
        for h in range(nh):
            s = lax.dot_general(k_ref[pl.ds(off, tq), head_cols(h)], q_ref[:, head_cols(h)], _NT,
                                preferred_element_type=F32)
            ss += [s[:, :th], s[:, th:]]
        return tuple(softmax_pv(carry[n], ss[n], vt_ref[j, chains[n][0]])
                     for n in range(len(chains)))

    def diagonal_step(carry):
        off = pl.multiple_of(i * tq, tq)
        ss, vts = [], []
        for h, u in chains:
            nk = th if u == 0 else tq
            s = lax.dot_general(k_ref[pl.ds(off, nk), head_cols(h)],
                                q_ref[u * th:(u + 1) * th, head_cols(h)], _NT,
                                preferred_element_type=F32)
            kidx = lax.broadcasted_iota(jnp.int32, (nk, th), 0)
            qidx = lax.broadcasted_iota(jnp.int32, (nk, th), 1) + u * th
            ss.append(jnp.where(kidx <= qidx, s, NEG_BIG))
            vts.append(vt_ref[i, h][:, :nk])
        return tuple(softmax_pv(carry[n], ss[n], vts[n]) for n in range(len(chains)))

    init = tuple((jnp.full((1, th), NEG_BIG, F32), jnp.zeros((1, th), F32),
                  jnp.zeros((MLA_V, th), F32)) for _ in chains)
    carry = lax.fori_loop(0, i, step, init)
    carry = diagonal_step(carry)
    for n, (h, u) in enumerate(chains):
        _, l, acc = carry[n]
        y = (acc / l).T
        rs = slice(u * th, (u + 1) * th)
        cs = slice(h * MLA_V, (h + 1) * MLA_V)
        o_ref[rs, cs] = (_silu(z_ref[rs, cs]) * y).astype(BF16)


def _attention(q, k, v_t, proj, batch, seq):
    m = q.shape[0]
    nq = seq // TQ
    nh = ATTN_HEADS_PER_STEP
    v_t5 = v_t.reshape(batch, nq, MLA_HEADS, MLA_V, TQ)
    return pl.pallas_call(
        functools.partial(_attn_kernel, tq=TQ),
        grid=(batch, MLA_HEADS // nh, nq),
        in_specs=[
            pl.BlockSpec((TQ, nh * MLA_QK_PAD), lambda b, h, i: (b * nq + i, h)),
            pl.BlockSpec((seq, nh * MLA_QK_PAD), lambda b, h, i: (b, h)),
            pl.BlockSpec((None, nq, nh, MLA_V, TQ), lambda b, h, i: (b, 0, h, 0, 0)),
            pl.BlockSpec((TQ, nh * MLA_V),
                         lambda b, h, i: (b * nq + i, OFF_MLAZ // (nh * MLA_V) + h)),
        ],
        out_specs=pl.BlockSpec((TQ, nh * MLA_V), lambda b, h, i: (b * nq + i, h)),
        out_shape=jax.ShapeDtypeStruct((m, MLA_WIDTH), BF16),
        compiler_params=_params("parallel", "parallel", "arbitrary"),
        name="mla_attn",
    )(q, k, v_t5, proj)


def _shift_rows(x, prev8, s):
    rolled = pltpu.roll(x, s, 0)
    prev_rolled = pltpu.roll(prev8, s, 0)
    rid = lax.broadcasted_iota(jnp.int32, (SUBLANES, x.shape[1]), 0)
    top = jnp.where(rid < s, prev_rolled, rolled[0:SUBLANES])
    return jnp.concatenate([top, rolled[SUBLANES:]], axis=0)


def _mlstm_kernel(ib_ref, fb_ref, q_ref, k_ref, v_ref, o_ref, z_ref, g_ref, w_ref, b_ref, ng_ref,
                  out_ref, ct_ref, n_ref, m_ref, qp_ref, kp_ref, *, layer, tt, chunk):
    @pl.when(pl.program_id(1) == 0)
    def _():
        for ref in (ct_ref, n_ref, m_ref, qp_ref, kp_ref):
            ref[...] = jnp.zeros_like(ref)

    hd = ML_HEAD_DIM
    w_all = w_ref[...]
    b_all = b_ref[...]

    def conv_silu(x_ref, prev_ref, w, b):
        x = x_ref[...]
        prev8 = prev_ref[...]
        y = b + w[ML_CONV - 1:ML_CONV] * x
        for s in range(1, ML_CONV):
            y = y + w[ML_CONV - 1 - s:ML_CONV - s] * _shift_rows(x, prev8, s)
        prev_ref[...] = x[tt - SUBLANES:tt]
        return _silu(y)

    q_all = conv_silu(q_ref, qp_ref, w_all[:, :ML_WIDTH], b_all[:, :ML_WIDTH]) * (hd ** -0.5)
    k_all = conv_silu(k_ref, kp_ref, w_all[:, ML_WIDTH:], b_all[:, ML_WIDTH:])
    li_rows = [g_ref[h] + ib_ref[layer, h] for h in range(ML_HEADS)]
    lf_rows = [_log_sigmoid(g_ref[ML_HEADS + h] + fb_ref[layer, h]) for h in range(ML_HEADS)]

    L = chunk
    rows = lax.broadcasted_iota(jnp.int32, (L, L), 0)
    cols = lax.broadcasted_iota(jnp.int32, (L, L), 1)
    tri = rows >= cols
    eye = rows == cols
    for c in range(tt // L):
        sl = slice(c * L, (c + 1) * L)
        st = []
        for h in range(ML_HEADS):
            cs = slice(h * hd, (h + 1) * hd)
            li_row = li_rows[h][:, sl]
            lf_b = jnp.broadcast_to(lf_rows[h][:, sl], (L, L))
            li_b = jnp.broadcast_to(li_row, (L, L))
            b_col = jnp.sum(jnp.where(tri, lf_b, 0.0), axis=-1, keepdims=True)
            lf_col = jnp.sum(jnp.where(eye, lf_b, 0.0), axis=-1, keepdims=True)
            li_col = jnp.sum(jnp.where(eye, li_b, 0.0), axis=-1, keepdims=True)
            b_row = jnp.sum(jnp.where(rows <= cols, jnp.broadcast_to(lf_col, (L, L)), 0.0),
                            axis=0, keepdims=True)
            m_prev = m_ref[h][:, 0:1]
            dmat = jnp.where(tri, b_col - b_row + li_row, NEG_BIG)
            m_inter = b_col + m_prev
            m_t = jnp.maximum(m_inter, jnp.max(dmat, axis=-1, keepdims=True))
            st.append(dict(cs=cs, qc=q_all[sl, cs].astype(BF16), kc=k_all[sl, cs],
                           vc=v_ref[sl, cs].astype(BF16), pm=jnp.exp(dmat - m_t),
                           inter=jnp.exp(m_inter - m_t), m_t=m_t, b_col=b_col, li_col=li_col,
                           m_prev=m_prev))
        for s in st:
            s["smat"] = lax.dot_general(s["qc"], s["kc"].astype(BF16), _NT,
                                        preferred_element_type=F32) * s["pm"]
        for h, s in enumerate(st):
            ct = ct_ref[h]
            n_row = n_ref[h]
            num = (s["inter"] * jnp.dot(s["qc"], ct.astype(BF16), preferred_element_type=F32)
                   + jnp.dot(s["smat"].astype(BF16), s["vc"], preferred_element_type=F32))
            den = (s["inter"] * jnp.sum(s["qc"].astype(F32) * n_row, axis=-1, keepdims=True)
                   + jnp.sum(s["smat"], axis=-1, keepdims=True))
            s["hh"] = num / jnp.maximum(jnp.abs(den), jnp.exp(-s["m_t"]))
            b_last = s["b_col"][L - 1:L, :]
            g_col = b_last - s["b_col"] + s["li_col"]
            m_new = jnp.maximum(b_last + s["m_prev"], jnp.max(g_col, axis=0, keepdims=True))
            decay = jnp.exp(b_last + s["m_prev"] - m_new)
            kw = s["kc"] * jnp.exp(g_col - m_new)
            ct_ref[h] = decay * ct + jnp.dot(kw.T.astype(BF16), s["vc"],
                                             preferred_element_type=F32)
            n_ref[h] = decay * n_row + jnp.sum(kw, axis=0, keepdims=True)
            m_ref[h] = jnp.broadcast_to(m_new, (1, hd))
        for s in st:
            cs = s["cs"]
            hh = s["hh"]
            mu = jnp.mean(hh, axis=-1, keepdims=True)
            dd = hh - mu
            var = jnp.mean(dd * dd, axis=-1, keepdims=True)
            yn = dd * lax.rsqrt(var + NORM_EPS) * ng_ref[:, cs]
            out_ref[sl, cs] = (_silu(z_ref[sl, cs]) * (_sigmoid(o_ref[sl, cs]) * yn)).astype(BF16)


def _mlstm(proj, gates_t, i_bias, f_bias, conv_w, conv_b3, norm_g3, layer, batch, seq):
    m = proj.shape[0]
    tt = min(ML_TT, seq)
    nt = seq // tt
    hd = ML_HEAD_DIM
    w = ML_WIDTH

    def col(off):
        return pl.BlockSpec((tt, w), lambda b, t: (b * nt + t, off // w))

    smem = pl.BlockSpec(memory_space=pltpu.SMEM)
    return pl.pallas_call(
        functools.partial(_mlstm_kernel, layer=layer, tt=tt, chunk=min(ML_L, tt)),
        grid=(batch, nt),
        in_specs=[
            smem, smem,
            col(OFF_QK), col(OFF_QK + w), col(OFF_MLV), col(OFF_MLO), col(OFF_MLZ),
            pl.BlockSpec((None, 2 * ML_HEADS, 1, tt), lambda b, t: (b, 0, 0, t)),
            pl.BlockSpec((None, ML_CONV, 2 * w), lambda b, t: (layer, 0, 0)),
            pl.BlockSpec((None, 1, 2 * w), lambda b, t: (layer, 0, 0)),
            pl.BlockSpec((None, 1, w), lambda b, t: (layer, 0, 0)),
        ],
        out_specs=pl.BlockSpec((tt, w), lambda b, t: (b * nt + t, 0)),
        out_shape=jax.ShapeDtypeStruct((m, w), BF16),
        scratch_shapes=[
            pltpu.VMEM((ML_HEADS, hd, hd), F32),
            pltpu.VMEM((ML_HEADS, 1, hd), F32),
            pltpu.VMEM((ML_HEADS, 1, hd), F32),
            pltpu.VMEM((SUBLANES, w), F32),
            pltpu.VMEM((SUBLANES, w), F32),
        ],
        compiler_params=_params("parallel", "arbitrary"),
        name="mlstm",
    )(i_bias, f_bias, proj, proj, proj, proj, proj, gates_t, conv_w, conv_b3, norm_g3)


RW_PRE_PASSES = 3


def _rwkv_pre_kernel(*refs, has_vres, tm):
    if has_vres:
        (wa_ref, r_ref, k_ref, v_ref, mu_wa, mu_r, mu_k, mu_v, w0_ref, w2_ref, a0_ref, a2_ref,
         kk_ref, ka_ref, e_ref, vf_ref, v0_ref, v1_ref, v2_ref,
         r_out, ld_out, kh_out, v_out, kn_out, b_out, c_wa, c_r, c_k, c_v) = refs
    else:
        (wa_ref, r_ref, k_ref, v_ref, mu_wa, mu_r, mu_k, mu_v, w0_ref, w2_ref, a0_ref, a2_ref,
         kk_ref, ka_ref, e_ref,
         r_out, ld_out, kh_out, v_out, kn_out, b_out, c_wa, c_r, c_k, c_v) = refs

    @pl.when(pl.program_id(1) == 0)
    def _():
        for c in (c_wa, c_r, c_k, c_v):
            c[...] = jnp.zeros_like(c)

    def mix(x_ref, c_ref, mu_ref):
        x = x_ref[...]
        rolled = pltpu.roll(x, 1, 0)
        rid = lax.broadcasted_iota(jnp.int32, (SUBLANES, x.shape[1]), 0)
        prev_last = jnp.broadcast_to(c_ref[SUBLANES - 1:SUBLANES, :], (SUBLANES, x.shape[1]))
        top = jnp.where(rid == 0, prev_last, rolled[0:SUBLANES])
        xprev = jnp.concatenate([top, rolled[SUBLANES:]], axis=0)
        c_ref[...] = x[tm - SUBLANES:tm]
        return x + mu_ref[...] * (xprev - x)

    xwa = mix(wa_ref, c_wa, mu_wa)
    r = mix(r_ref, c_r, mu_r)
    k = mix(k_ref, c_k, mu_k)
    v = mix(v_ref, c_v, mu_v)
    zw = w0_ref[...] + _mm(jnp.tanh(xwa), w2_ref[...], RW_PRE_PASSES)
    za = a0_ref[...] + _mm(xwa, a2_ref[...], RW_PRE_PASSES)
    log_w = _log_sigmoid(zw) - 0.5
    ld_out[...] = -jnp.exp(log_w)
    a = _sigmoid(za)
    if has_vres:
        gate = _sigmoid(v0_ref[...] + _mm(_mm(v, v1_ref[...], RW_PRE_PASSES), v2_ref[...],
                                          RW_PRE_PASSES))
        v = v + (vf_ref[...] - v) * gate
    kk = k * kk_ref[...]
    ss = _mm(kk * kk, e_ref[...])
    kn = kk / jnp.maximum(jnp.sqrt(ss), 1e-12)
    r_out[...] = r
    kh_out[...] = k * (1.0 + (a - 1.0) * ka_ref[...])
    v_out[...] = v
    kn_out[...] = kn
    b_out[...] = kn * a


RW_GROUP = 2


def _rwkv_scan_kernel(r_ref, ld_ref, kh_ref, v_ref, kn_ref, b_ref, z_ref, rk_ref, g_ref, bias_ref,
                      e_ref, out_ref, m_ref, ac_ref, gc_ref, qs_ref, y_ref, *, tt, chunk):
    @pl.when(pl.program_id(1) == 0)
    def _():
        m_ref[...] = jnp.zeros_like(m_ref)

    L = chunk
    L2 = 2 * L
    nchunk = tt // L
    npair = RW_WIDTH // LANES
    rows = lax.broadcasted_iota(jnp.int32, (L, L), 0)
    cols = lax.broadcasted_iota(jnp.int32, (L, L), 1)
    tri_b = jnp.where(rows >= cols, 1.0, 0.0).astype(BF16)
    lane = lax.broadcasted_iota(jnp.int32, (1, LANES), 1)
    m0 = jnp.where(lane < RW_HEAD_DIM, 1.0, 0.0)
    m1 = 1.0 - m0
    r2 = lax.broadcasted_iota(jnp.int32, (L2, L2), 0)
    c2 = lax.broadcasted_iota(jnp.int32, (L2, L2), 1)
    same_blk = (r2 < L) == (c2 < L)
    t2 = jnp.where(r2 < L, r2, r2 - L)
    s2 = jnp.where(c2 < L, c2, c2 - L)
    mask_strict = jnp.logical_and(same_blk, t2 > s2)
    mask_incl = jnp.logical_and(same_blk, t2 >= s2)
    eye2 = lax.broadcasted_iota(jnp.int32, (LANES, LANES), 0) == lax.broadcasted_iota(
        jnp.int32, (LANES, LANES), 1)
    n_double = max(1, (L - 1).bit_length())

    def stack(x):
        return jnp.concatenate([x * m0, x * m1], axis=0)

    def prepare(p, c):
        sl = pl.ds(pl.multiple_of(c * L, L), L)
        cs = slice(p * LANES, (p + 1) * LANES)
        ld = ld_ref[sl, cs]
        r, kh, v, kn, b = r_ref[sl, cs], kh_ref[sl, cs], v_ref[sl, cs], kn_ref[sl, cs], b_ref[sl, cs]
        ld_hi, ld_lo = _split_bf16(ld)
        logp = (jnp.dot(tri_b, ld_hi, preferred_element_type=F32)
                + jnp.dot(tri_b, ld_lo, preferred_element_type=F32))
        cmid = logp[L // 2 - 1:L // 2, :]
        e = logp - cmid
        e_last = e[L - 1:L, :]
        p_mid = jnp.exp(cmid)
        at_s = stack(-kn * jnp.exp(e - ld))
        rt_s = stack(r * jnp.exp(e))
        inv = jnp.exp(-e)
        tail = jnp.exp(e_last - e)
        lhs = jnp.concatenate([at_s, rt_s], axis=0).astype(BF16)
        rhs = jnp.concatenate([stack(b * inv), stack(kh * inv)], axis=0).astype(BF16)
        hat_t = jnp.concatenate([stack(b * tail), stack(kh * tail)], axis=0).T.astype(BF16)
        return dict(sl=sl, cs=cs, idx=p * nchunk + c, lhs=lhs, rhs=rhs, hat_t=hat_t,
                    at_true=at_s * p_mid, rt_true=rt_s * p_mid, v_s=stack(v).astype(BF16),
                    p_last=jnp.exp(e_last + cmid))

    def coeff_body(cg, carry):
        insts = [prepare(p, cg * RW_GROUP + g) for p in range(npair) for g in range(RW_GROUP)]
        for s in insts:
            aa = lax.dot_general(s["lhs"], s["rhs"], _NT, preferred_element_type=F32)
            s["nmat"] = jnp.where(mask_strict, aa[0:L2, 0:L2], 0.0).astype(BF16)
            s["a_ak"] = jnp.where(mask_strict, aa[0:L2, L2:], 0.0).astype(BF16)
            s["a_r"] = jnp.concatenate([jnp.where(mask_incl, aa[L2:, 0:L2], 0.0),
                                        jnp.where(mask_incl, aa[L2:, L2:], 0.0)],
                                       axis=1).astype(BF16)
        for s in insts:
            s["x"] = jnp.concatenate(
                [s["at_true"], jnp.dot(s["a_ak"], s["v_s"], preferred_element_type=F32)],
                axis=1)
        for it in range(n_double):
            for s in insts:
                s["x"] = s["x"] + jnp.dot(s["nmat"], s["x"].astype(BF16),
                                          preferred_element_type=F32)
            if it + 1 < n_double:
                for s in insts:
                    s["nmat"] = jnp.dot(s["nmat"], s["nmat"],
                                        preferred_element_type=F32).astype(BF16)
        for s in insts:
            big_l = jnp.concatenate([s["a_r"], s["hat_t"]], axis=0)
            big_r = jnp.concatenate(
                [s["x"].astype(BF16),
                 jnp.concatenate([jnp.zeros((L2, LANES), BF16), s["v_s"]], axis=1)], axis=0)
            res = jnp.dot(big_l, big_r, preferred_element_type=F32)
            q_s = s["rt_true"] + res[0:L2, :LANES]
            y0_s = res[0:L2, LANES:]
            qs_ref[s["sl"], s["cs"]] = q_s[0:L] + q_s[L:]
            y_ref[s["sl"], s["cs"]] = y0_s[0:L] + y0_s[L:]
            ac_ref[s["idx"]] = (jnp.where(eye2, jnp.broadcast_to(s["p_last"], (LANES, LANES)), 0.0)
                                + res[L2:, :LANES])
            gc_ref[s["idx"]] = res[L2:, LANES:]
        return carry

    lax.fori_loop(0, nchunk // RW_GROUP, coeff_body, 0)

    def chain_body(c, carry):
        sl = pl.ds(pl.multiple_of(c * L, L), L)
        for p in range(npair):
            cs = slice(p * LANES, (p + 1) * LANES)
            idx = p * nchunk + c
            mp = m_ref[p]
            y_ref[sl, cs] = _mm(qs_ref[sl, cs], mp) + y_ref[sl, cs]
            m_ref[p] = _mm(ac_ref[idx], mp) + gc_ref[idx]
        return carry

    lax.fori_loop(0, nchunk, chain_body, 0)

    e_b = e_ref[...]
    inv_n = 1.0 / RW_HEAD_DIM

    def head_sum(x):
        return jnp.dot(x.astype(BF16), e_b, preferred_element_type=F32)

    for p in range(npair):
        cs = slice(p * LANES, (p + 1) * LANES)
        y = y_ref[:, cs]
        mu = head_sum(y) * inv_n
        d = y - mu
        var = head_sum(d * d) * inv_n
        yn = d * lax.rsqrt(var + RW_GN_EPS) * g_ref[:, cs] + bias_ref[:, cs]
        v_all = v_ref[:, cs]
        bonus = head_sum(r_ref[:, cs] * kh_ref[:, cs] * rk_ref[:, cs])
        out_ref[:, cs] = (_silu(z_ref[:, cs]) * (yn + bonus * v_all)).astype(BF16)


def _rwkv_kernel(*refs, has_vres, tt, chunk):
    n_mix = 19 if has_vres else 15
    mix_in = refs[:n_mix]
    z_ref, rk_ref, g_ref, bias_ref, e_ref = refs[n_mix:n_mix + 5]
    n_out = 1 if has_vres else 2
    outs = refs[n_mix + 5:n_mix + 5 + n_out]
    (c_wa, c_r, c_k, c_v, r_s, ld_s, kh_s, v_s, kn_s, b_s,
     m_ref, ac_ref, gc_ref, qs_ref, y_ref) = refs[n_mix + 5 + n_out:]
    _rwkv_pre_kernel(*mix_in, r_s, ld_s, kh_s, v_s, kn_s, b_s, c_wa, c_r, c_k, c_v,
                     has_vres=has_vres, tm=tt)
    if not has_vres:
        outs[1][...] = v_s[...]
    _rwkv_scan_kernel(r_s, ld_s, kh_s, v_s, kn_s, b_s, z_ref, rk_ref, g_ref, bias_ref, e_ref,
                      outs[0], m_ref, ac_ref, gc_ref, qs_ref, y_ref, tt=tt, chunk=chunk)


def _rwkv(proj, v_first, mu_wa, mu_r, mu_k, mu_v, w0, w2p, a0, a2p, k_k, k_a, e_head,
          v0, v1p, v2p, r_k3, ln_g3, ln_b3, e_pair, layer, batch, seq):
    m = proj.shape[0]
    tt = min(RW_TT, seq)
    nt = seq // tt
    chunk = min(RW_L, tt)
    has_vres = layer > 0
    w = RW_WIDTH
    npair = w // LANES

    def colblk(off, width):
        return pl.BlockSpec((tt, width), lambda b, t: (b * nt + t, off // width))

    def lay(shape):
        return pl.BlockSpec((None,) + shape, lambda b, t: (layer,) + (0,) * len(shape))

    def lay1(shape):
        return pl.BlockSpec((None,) + shape, lambda b, t: (layer - 1,) + (0,) * len(shape))

    row = pl.BlockSpec((tt, w), lambda b, t: (b * nt + t, 0))
    in_specs = [colblk(OFF_WA, LANES), colblk(OFF_RWR, w), colblk(OFF_RWK, w), colblk(OFF_RWV, w),
                lay((1, LANES)), lay((1, w)), lay((1, w)), lay((1, w)),
                lay((1, w)), lay((LANES, w)), lay((1, w)), lay((LANES, w)),
                lay((1, w)), lay((1, w)),
                pl.BlockSpec((w, w), lambda b, t: (0, 0))]
    args = [proj, proj, proj, proj, mu_wa, mu_r, mu_k, mu_v, w0, w2p, a0, a2p, k_k, k_a, e_head]
    if has_vres:
        in_specs += [row, lay1((1, w)), lay1((w, LANES)), lay1((LANES, w))]
        args += [v_first, v0, v1p, v2p]
    in_specs += [colblk(OFF_RWZ, w), lay((1, w)), lay((1, w)), lay((1, w)),
                 pl.BlockSpec((LANES, LANES), lambda b, t: (0, 0))]
    args += [proj, r_k3, ln_g3, ln_b3, e_pair]
    out_specs = [row]
    out_shape = [jax.ShapeDtypeStruct((m, w), BF16)]
    if not has_vres:
        out_specs.append(row)
        out_shape.append(jax.ShapeDtypeStruct((m, w), F32))
    scratch = [pltpu.VMEM((SUBLANES, LANES), F32)] + [pltpu.VMEM((SUBLANES, w), F32)] * 3
    scratch += [pltpu.VMEM((tt, w), F32)] * 6
    scratch += [
        pltpu.VMEM((npair, LANES, LANES), F32),
        pltpu.VMEM((npair * (tt // chunk), LANES, LANES), F32),
        pltpu.VMEM((npair * (tt // chunk), LANES, LANES), F32),
        pltpu.VMEM((tt, w), F32),
        pltpu.VMEM((tt, w), F32),
    ]
    return pl.pallas_call(
        functools.partial(_rwkv_kernel, has_vres=has_vres, tt=tt, chunk=chunk),
        grid=(batch, nt),
        in_specs=in_specs,
        out_specs=out_specs,
        out_shape=out_shape,
        scratch_shapes=scratch,
        compiler_params=_params("parallel", "arbitrary"),
        name="rwkv",
    )(*args)


_W_IN_PARTS = ((0, 1024), (1024, 512), (1544, 512), (2056, 512), (2568, 512), (3080, 256),
               (3336, 64), (1536, 8), (None, LANES - MLA_ROPE - 2 * ML_HEADS), (5960, 128),
               (4424, 512), (4936, 512), (5448, 512), (6088, 512), (3400, 1024))
TK_PREP = 256


def _permute_w_in_kernel(src_ref, dst_ref):
    off = 0
    for start, width in _W_IN_PARTS:
        if start is None:
            dst_ref[:, off:off + width] = jnp.zeros((dst_ref.shape[0], width), BF16)
        else:
            dst_ref[:, off:off + width] = src_ref[:, start:start + width].astype(BF16)
        off += width


def _permute_w_in(w_in):
    depth, d_model, d_in = w_in.shape
    return pl.pallas_call(
        _permute_w_in_kernel,
        grid=(depth, d_model // TK_PREP),
        in_specs=[pl.BlockSpec((None, TK_PREP, d_in), lambda l, i: (l, i, 0))],
        out_specs=pl.BlockSpec((None, TK_PREP, D_IN_PAD), lambda l, i: (l, i, 0)),
        out_shape=jax.ShapeDtypeStruct((depth, d_model, D_IN_PAD), BF16),
        compiler_params=_params("parallel", "parallel"),
        name="permute_w_in",
    )(w_in)


def _permute_w_uq(w_uq):
    d = w_uq.shape[0]
    w = w_uq.reshape(d, MLA_Q_RANK, MLA_HEADS, MLA_NOPE + MLA_ROPE)
    w = jnp.pad(w, ((0, 0), (0, 0), (0, 0), (0, MLA_QK_PAD - MLA_NOPE - MLA_ROPE)))
    return w.reshape(d, MLA_Q_RANK, MLA_HEADS * MLA_QK_PAD).astype(BF16)


def _split_w_ukv(w_ukv):
    d = w_ukv.shape[0]
    w = w_ukv.reshape(d, MLA_KV_RANK, MLA_HEADS, MLA_NOPE + MLA_V)
    wk = w[..., :MLA_NOPE].reshape(d, MLA_KV_RANK, MLA_HEADS * MLA_NOPE)
    wv = w[..., MLA_NOPE:].reshape(d, MLA_KV_RANK, MLA_HEADS * MLA_V)
    return wk.astype(BF16), wv.astype(BF16)


def _rope_tables(positions):
    inv_freq = jnp.power(ROPE_BASE, -jnp.arange(0, MLA_ROPE, 2, dtype=F32) / MLA_ROPE)
    ang = positions.astype(F32).reshape(-1, 1) * inv_freq
    cos, sin = jnp.cos(ang), jnp.sin(ang)
    z32 = jnp.zeros_like(cos)
    z64 = jnp.zeros((ang.shape[0], LANES - MLA_ROPE), F32)
    cosf = jnp.concatenate([cos, cos, z64], axis=-1)
    s1 = jnp.concatenate([-sin, z32, z64], axis=-1)
    s2 = jnp.concatenate([z32, sin, z64], axis=-1)
    return cosf, s1, s2


def kernel(x, positions, norm_g, w_in, ml_conv_w, ml_conv_b, ml_i_bias, ml_f_bias, ml_norm_g,
           mla_q_norm_g, mla_w_uq, mla_kv_norm_g, mla_w_ukv, rw_mu, rw_w0, rw_w2, rw_a0, rw_a2,
           rw_v0, rw_v1, rw_v2, rw_k_k, rw_k_a, rw_r_k, rw_ln_g, rw_ln_b, w_out, final_norm_g):
    batch, seq, _ = x.shape
    depth = w_in.shape[0]
    m = batch * seq
    x2 = x.reshape(m, D_MODEL)

    w_in_p = _permute_w_in(w_in)
    w_uq_p = _permute_w_uq(mla_w_uq)
    w_uk_p, w_uv_p = _split_w_ukv(mla_w_ukv)
    w_out_b = w_out.astype(BF16)
    cosf, s1, s2 = _rope_tables(positions)

    def row3(a):
        return a.reshape(a.shape[0], 1, a.shape[1])

    norm_g3 = row3(norm_g)
    conv_b3 = row3(ml_conv_b)
    ml_norm_g3 = row3(ml_norm_g)
    qg3, kvg3 = row3(mla_q_norm_g), row3(mla_kv_norm_g)
    w = RW_WIDTH
    mu_r, mu_k, mu_v = row3(rw_mu[:, 0:w]), row3(rw_mu[:, w:2 * w]), row3(rw_mu[:, 2 * w:3 * w])
    mu_wa = row3(rw_mu[:, 3 * w:])
    w0, a0, k_k, k_a = row3(rw_w0), row3(rw_a0), row3(rw_k_k), row3(rw_k_a)
    w2p = jnp.pad(rw_w2, ((0, 0), (0, LANES - RW_DECAY_RANK), (0, 0)))
    a2p = jnp.pad(rw_a2, ((0, 0), (LANES - RW_AAA_RANK, 0), (0, 0)))
    v0 = row3(rw_v0)
    v1p = jnp.pad(rw_v1, ((0, 0), (0, 0), (0, LANES - RW_MV_RANK)))
    v2p = jnp.pad(rw_v2, ((0, 0), (0, LANES - RW_MV_RANK), (0, 0)))
    r_k3 = rw_r_k.reshape(depth, 1, w)
    ln_g3, ln_b3 = row3(rw_ln_g), row3(rw_ln_b)
    hid = jnp.arange(w) // RW_HEAD_DIM
    e_head = (hid[:, None] == hid[None, :]).astype(F32)
    e_pair = e_head[:LANES, :LANES].astype(BF16)
    final_g3 = final_norm_g.reshape(1, 1, D_MODEL)

    v_first = None
    h = _rmsnorm(x2, norm_g3, 0)
    for layer in range(depth):
        proj = _inproj(h, w_in_p, layer)
        gates_t = proj[:, OFF_KRG + MLA_ROPE:OFF_KRG + MLA_ROPE + 2 * ML_HEADS]
        gates_t = gates_t.reshape(batch, seq, 2 * ML_HEADS).transpose(0, 2, 1)
        gates_t = gates_t.reshape(batch, 2 * ML_HEADS, 1, seq)
        y_ml = _mlstm(proj, gates_t, ml_i_bias, ml_f_bias, ml_conv_w, conv_b3, ml_norm_g3,
                      layer, batch, seq)
        q, k, v = _mla_proj(proj, cosf, s1, s2, qg3, w_uq_p, kvg3, w_uk_p, w_uv_p, layer)
        y_mla = _attention(q, k, v, proj, batch, seq)
        rw_out = _rwkv(proj, v_first, mu_wa, mu_r, mu_k, mu_v, w0, w2p, a0, a2p, k_k, k_a, e_head,
                       v0, v1p, v2p, r_k3, ln_g3, ln_b3, e_pair, layer, batch, seq)
        y_rw = rw_out[0]
        if layer == 0:
            v_first = rw_out[1]
        if layer == depth - 1:
            (x2,) = _outproj(x2, y_ml, y_mla, y_rw, w_out_b, final_g3, layer, True)
        else:
            x2, h = _outproj(x2, y_ml, y_mla, y_rw, w_out_b, norm_g3, layer, False)
    return x2.reshape(batch, seq, D_MODEL)
```

```python
import functools

import jax
import jax.numpy as jnp
from jax import lax
from jax.experimental import pallas as pl
from jax.experimental.pallas import tpu as pltpu

F32 = jnp.float32
BF16 = jnp.bfloat16

D_MODEL = 2048
NORM_EPS = 1e-6
ML_HEADS = 4
ML_HEAD_DIM = 128
ML_WIDTH = 512
ML_CONV = 4
MLA_HEADS = 8
MLA_NOPE = 128
MLA_ROPE = 64
MLA_V = 128
MLA_WIDTH = 1024
MLA_Q_RANK = 512
MLA_KV_RANK = 256
ROPE_BASE = 10000.0
RW_HEAD_DIM = 64
RW_WIDTH = 512
RW_HEADS = 8
RW_DECAY_RANK = 64
RW_AAA_RANK = 64
RW_MV_RANK = 32
RW_GN_EPS = 64e-5

LANES = 128
SUBLANES = 8
MLA_QK_PAD = 256
VMEM_LIMIT = 48 * 1024 * 1024
NEG_BIG = -1e30
LOG2E = 1.4426950408889634

OFF_QK = 0
OFF_MLV = 1024
OFF_MLO = 1536
OFF_MLZ = 2048
OFF_CQ = 2560
OFF_CKV = 3072
OFF_KRG = 3328
OFF_WA = 3456
OFF_RWR = 3584
OFF_RWK = 4096
OFF_RWV = 4608
OFF_RWZ = 5120
OFF_MLAZ = 5632
D_IN_PAD = 6656

TM_PROJ = 1024
TN_PROJ = 1664
TM_OUT = 512
TQ = 512
ML_TT = 1024
ML_L = 256
RW_TT = 512
RW_L = 64


def _split_bf16(x):
    hi = x.astype(BF16)
    lo = (x - hi.astype(F32)).astype(BF16)
    return hi, lo


def _mm(a, b, passes=1):
    if passes == 1:
        return jnp.dot(a.astype(BF16), b.astype(BF16), preferred_element_type=F32)
    ah, al = _split_bf16(a)
    bh, bl = _split_bf16(b)
    return (jnp.dot(ah, bh, preferred_element_type=F32)
            + jnp.dot(ah, bl, preferred_element_type=F32)
            + jnp.dot(al, bh, preferred_element_type=F32))


_NT = (((1,), (1,)), ((), ()))


def _mm_nt(a, b, passes=1):
    if passes == 1:
        return lax.dot_general(a.astype(BF16), b.astype(BF16), _NT, preferred_element_type=F32)
    ah, al = _split_bf16(a)
    bh, bl = _split_bf16(b)
    return (lax.dot_general(ah, bh, _NT, preferred_element_type=F32)
            + lax.dot_general(ah, bl, _NT, preferred_element_type=F32)
            + lax.dot_general(al, bh, _NT, preferred_element_type=F32))


def _sigmoid(x):
    return 0.5 * jnp.tanh(0.5 * x) + 0.5


def _silu(x):
    h = 0.5 * x
    return h * jnp.tanh(h) + h


def _log_sigmoid(x):
    return jnp.minimum(x, 0.0) - jnp.log(1.0 + jnp.exp(-jnp.abs(x)))


def _params(*sem):
    return pltpu.CompilerParams(dimension_semantics=sem, vmem_limit_bytes=VMEM_LIMIT)


def _rms_scale(x, g):
    return x * lax.rsqrt(jnp.mean(x * x, axis=-1, keepdims=True) + NORM_EPS) * g


def _rmsnorm_kernel(x_ref, g_ref, h_ref):
    h_ref[...] = _rms_scale(x_ref[...], g_ref[...]).astype(BF16)


def _rmsnorm(x2, norm_g3, layer):
    m = x2.shape[0]
    return pl.pallas_call(
        _rmsnorm_kernel,
        grid=(m // TM_OUT,),
        in_specs=[pl.BlockSpec((TM_OUT, D_MODEL), lambda i: (i, 0)),
                  pl.BlockSpec((None, 1, D_MODEL), lambda i: (layer, 0, 0))],
        out_specs=pl.BlockSpec((TM_OUT, D_MODEL), lambda i: (i, 0)),
        out_shape=jax.ShapeDtypeStruct((m, D_MODEL), BF16),
        compiler_params=_params("parallel"),
        name="rmsnorm",
    )(x2, norm_g3)


def _inproj_kernel(h_ref, w_ref, o_ref):
    o_ref[...] = jnp.dot(h_ref[...], w_ref[...], preferred_element_type=F32)


def _inproj(h, w_in_p, layer):
    m = h.shape[0]
    return pl.pallas_call(
        _inproj_kernel,
        grid=(m // TM_PROJ, D_IN_PAD // TN_PROJ),
        in_specs=[
            pl.BlockSpec((TM_PROJ, D_MODEL), lambda i, j: (i, 0)),
            pl.BlockSpec((None, D_MODEL, TN_PROJ), lambda i, j: (layer, 0, j)),
        ],
        out_specs=pl.BlockSpec((TM_PROJ, TN_PROJ), lambda i, j: (i, j)),
        out_shape=jax.ShapeDtypeStruct((m, D_IN_PAD), F32),
        compiler_params=_params("parallel", "arbitrary"),
        name="inproj",
    )(h, w_in_p)


def _outproj_kernel(x_ref, yml_ref, ymla_ref, yrw_ref, w_ref, g_ref, *out_refs, final):
    acc = x_ref[...]
    acc = acc + jnp.dot(yml_ref[...], w_ref[0:ML_WIDTH, :], preferred_element_type=F32)
    acc = acc + jnp.dot(ymla_ref[...], w_ref[ML_WIDTH:ML_WIDTH + MLA_WIDTH, :],
                        preferred_element_type=F32)
    acc = acc + jnp.dot(yrw_ref[...], w_ref[ML_WIDTH + MLA_WIDTH:, :], preferred_element_type=F32)
    normed = _rms_scale(acc, g_ref[...])
    if final:
        out_refs[0][...] = normed
    else:
        out_refs[0][...] = acc
        out_refs[1][...] = normed.astype(BF16)


def _outproj(x2, y_ml, y_mla, y_rw, w_out_b, gains3, layer, final):
    m = x2.shape[0]
    g_index = 0 if final else layer + 1
    row = pl.BlockSpec((TM_OUT, D_MODEL), lambda i: (i, 0))
    out_specs = [row] if final else [row, row]
    out_shape = [jax.ShapeDtypeStruct((m, D_MODEL), F32)]
    if not final:
        out_shape.append(jax.ShapeDtypeStruct((m, D_MODEL), BF16))
    return pl.pallas_call(
        functools.partial(_outproj_kernel, final=final),
        grid=(m // TM_OUT,),
        in_specs=[
            row,
            pl.BlockSpec((TM_OUT, ML_WIDTH), lambda i: (i, 0)),
            pl.BlockSpec((TM_OUT, MLA_WIDTH), lambda i: (i, 0)),
            pl.BlockSpec((TM_OUT, RW_WIDTH), lambda i: (i, 0)),
            pl.BlockSpec((None, D_MODEL, D_MODEL), lambda i: (layer, 0, 0)),
            pl.BlockSpec((None, 1, D_MODEL), lambda i: (g_index, 0, 0)),
        ],
        out_specs=out_specs,
        out_shape=out_shape,
        compiler_params=_params("parallel"),
        name="outproj",
    )(x2, y_ml, y_mla, y_rw, w_out_b, gains3)


def _mla_proj_kernel(cq_ref, ckv_ref, krg_ref, cos_ref, s1_ref, s2_ref, qg_ref, wq_ref, kvg_ref,
                     wk_ref, wv_ref, q_out, k_out, v_out):
    cq = cq_ref[...]
    cqn = (cq * lax.rsqrt(jnp.mean(cq * cq, axis=-1, keepdims=True) + NORM_EPS)
           * qg_ref[...]).astype(BF16)
    q = jnp.dot(cqn, wq_ref[...], preferred_element_type=F32)
    ckv = ckv_ref[...]
    ckvn = (ckv * lax.rsqrt(jnp.mean(ckv * ckv, axis=-1, keepdims=True) + NORM_EPS)
            * kvg_ref[...]).astype(BF16)
    kn = jnp.dot(ckvn, wk_ref[...], preferred_element_type=F32)
    v = jnp.dot(ckvn, wv_ref[...], preferred_element_type=F32)
    cosf, s1, s2 = cos_ref[...], s1_ref[...], s2_ref[...]

    def rope(xb):
        return (xb * cosf + pltpu.roll(xb, LANES - MLA_ROPE // 2, 1) * s1
                + pltpu.roll(xb, MLA_ROPE // 2, 1) * s2)

    scale = (MLA_NOPE + MLA_ROPE) ** -0.5 * LOG2E
    kr = rope(krg_ref[...]).astype(BF16)
    for h in range(MLA_HEADS):
        c0 = h * MLA_QK_PAD
        q_out[:, c0:c0 + MLA_NOPE] = (q[:, c0:c0 + MLA_NOPE] * scale).astype(BF16)
        q_out[:, c0 + MLA_NOPE:c0 + MLA_QK_PAD] = (
            rope(q[:, c0 + MLA_NOPE:c0 + MLA_QK_PAD]) * scale).astype(BF16)
        k_out[:, c0:c0 + MLA_NOPE] = kn[:, h * MLA_NOPE:(h + 1) * MLA_NOPE].astype(BF16)
        k_out[:, c0 + MLA_NOPE:c0 + MLA_QK_PAD] = kr
    v_out[...] = v.T.astype(BF16)


def _mla_proj(proj, cosf, s1, s2, qg3, wq_p, kvg3, wk_p, wv_p, layer):
    m = proj.shape[0]
    tm = TQ
    row = lambda i: (i, 0)
    return pl.pallas_call(
        _mla_proj_kernel,
        grid=(m // tm,),
        in_specs=[
            pl.BlockSpec((tm, MLA_Q_RANK), lambda i: (i, OFF_CQ // MLA_Q_RANK)),
            pl.BlockSpec((tm, MLA_KV_RANK), lambda i: (i, OFF_CKV // MLA_KV_RANK)),
            pl.BlockSpec((tm, LANES), lambda i: (i, OFF_KRG // LANES)),
            pl.BlockSpec((tm, LANES), row),
            pl.BlockSpec((tm, LANES), row),
            pl.BlockSpec((tm, LANES), row),
            pl.BlockSpec((None, 1, MLA_Q_RANK), lambda i: (layer, 0, 0)),
            pl.BlockSpec((None, MLA_Q_RANK, MLA_HEADS * MLA_QK_PAD), lambda i: (layer, 0, 0)),
            pl.BlockSpec((None, 1, MLA_KV_RANK), lambda i: (layer, 0, 0)),
            pl.BlockSpec((None, MLA_KV_RANK, MLA_HEADS * MLA_NOPE), lambda i: (layer, 0, 0)),
            pl.BlockSpec((None, MLA_KV_RANK, MLA_HEADS * MLA_V), lambda i: (layer, 0, 0)),
        ],
        out_specs=[
            pl.BlockSpec((tm, MLA_HEADS * MLA_QK_PAD), row),
            pl.BlockSpec((tm, MLA_HEADS * MLA_QK_PAD), row),
            pl.BlockSpec((None, MLA_HEADS * MLA_V, tm), lambda i: (i, 0, 0)),
        ],
        out_shape=[
            jax.ShapeDtypeStruct((m, MLA_HEADS * MLA_QK_PAD), BF16),
            jax.ShapeDtypeStruct((m, MLA_HEADS * MLA_QK_PAD), BF16),
            jax.ShapeDtypeStruct((m // tm, MLA_HEADS * MLA_V, tm), BF16),
        ],
        compiler_params=_params("parallel"),
        name="mla_proj",
    )(proj, proj, proj, cosf, s1, s2, qg3, wq_p, kvg3, wk_p, wv_p)


ATTN_HEADS_PER_STEP = 4


def _attn_kernel(q_ref, k_ref, vt_ref, z_ref, o_ref, *, tq):
    i = pl.program_id(2)
    nh = ATTN_HEADS_PER_STEP
    th = tq // 2
    chains = [(h, u) for h in range(nh) for u in range(2)]

    def head_cols(h):
        return slice(h * MLA_QK_PAD, (h + 1) * MLA_QK_PAD)

    def softmax_pv(carry_c, s, vt):
        m, l, acc = carry_c
        m_new = jnp.maximum(m, jnp.max(s, axis=0, keepdims=True))
        alpha = jnp.exp2(m - m_new)
        p = jnp.exp2(s - m_new)
        l = alpha * l + jnp.sum(p, axis=0, keepdims=True)
        acc = alpha * acc + jnp.dot(vt, p.astype(BF16), preferred_element_type=F32)
        return m_new, l, acc

    def step(j, carry):
        off = pl.multiple_of(j * tq, tq)
        ss = []
        for h in range(nh):
            s = lax.dot_general(k_ref[pl.ds(off, tq), head_cols(h)], q_ref[:, head_cols(h)], _NT,
                                preferred_element_type=F32)
            ss += [s[:, :th], s[:, th:]]
        return tuple(softmax_pv(carry[n], ss[n], vt_ref[j, chains[n][0]])
                     for n in range(len(chains)))

    def diagonal_step(carry):
        off = pl.multiple_of(i * tq, tq)
        ss, vts = [], []
        for h, u in chains:
            nk = th if u == 0 else tq
            s = lax.dot_general(k_ref[pl.ds(off, nk), head_cols(h)],
                                q_ref[u * th:(u + 1) * th, head_cols(h)], _NT,
                                preferred_element_type=F32)
            kidx = lax.broadcasted_iota(jnp.int32, (nk, th), 0)
            qidx = lax.broadcasted_iota(jnp.int32, (nk, th), 1) + u * th
            ss.append(jnp.where(kidx <= qidx, s, NEG_BIG))
            vts.append(vt_ref[i, h][:, :nk])
        return tuple(softmax_pv(carry[n], ss[n], vts[n]) for n in range(len(chains)))

    init = tuple((jnp.full((1, th), NEG_BIG, F32), jnp.zeros((1, th), F32),
                  jnp.zeros((MLA_V, th), F32)) for _ in chains)
    carry = lax.fori_loop(0, i, step, init)
    carry = diagonal_step(carry)
    for n, (h, u) in enumerate(chains):
        _, l, acc = carry[n]
        y = (acc / l).T
        rs = slice(u * th, (u + 1) * th)
        cs = slice(h * MLA_V, (h + 1) * MLA_V)
        o_ref[rs, cs] = (_silu(z_ref[rs, cs]) * y).astype(BF16)


def _attention(q, k, v_t, proj, batch, seq):
    m = q.shape[0]
    nq = seq // TQ
    nh = ATTN_HEADS_PER_STEP
    v_t5 = v_t.reshape(batch, nq, MLA_HEADS, MLA_V, TQ)
    return pl.pallas_call(
        functools.partial(_attn_kernel, tq=TQ),
        grid=(batch, MLA_HEADS // nh, nq),
        in_specs=[
            pl.BlockSpec((TQ, nh * MLA_QK_PAD), lambda b, h, i: (b * nq + i, h)),
            pl.BlockSpec((seq, nh * MLA_QK_PAD), lambda b, h, i: (b, h)),
            pl.BlockSpec((None, nq, nh, MLA_V, TQ), lambda b, h, i: (b, 0, h, 0, 0)),
            pl.BlockSpec((TQ, nh * MLA_V),
                         lambda b, h, i: (b * nq + i, OFF_MLAZ // (nh * MLA_V) + h)),
        ],
        out_specs=pl.BlockSpec((TQ, nh * MLA_V), lambda b, h, i: (b * nq + i, h)),
        out_shape=jax.ShapeDtypeStruct((m, MLA_WIDTH), BF16),
        compiler_params=_params("parallel", "parallel", "arbitrary"),
        name="mla_attn",
    )(q, k, v_t5, proj)


def _shift_rows(x, prev8, s):
    rolled = pltpu.roll(x, s, 0)
    prev_rolled = pltpu.roll(prev8, s, 0)
    rid = lax.broadcasted_iota(jnp.int32, (SUBLANES, x.shape[1]), 0)
    top = jnp.where(rid < s, prev_rolled, rolled[0:SUBLANES])
    return jnp.concatenate([top, rolled[SUBLANES:]], axis=0)


def _mlstm_kernel(ib_ref, fb_ref, q_ref, k_ref, v_ref, o_ref, z_ref, g_ref, w_ref, b_ref, ng_ref,
                  out_ref, ct_ref, n_ref, m_ref, qp_ref, kp_ref, *, layer, tt, chunk):
    @pl.when(pl.program_id(1) == 0)
    def _():
        for ref in (ct_ref, n_ref, m_ref, qp_ref, kp_ref):
            ref[...] = jnp.zeros_like(ref)

    hd = ML_HEAD_DIM
    w_all = w_ref[...]
    b_all = b_ref[...]

    def conv_silu(x_ref, prev_ref, w, b):
        x = x_ref[...]
        prev8 = prev_ref[...]
        y = b + w[ML_CONV - 1:ML_CONV] * x
        for s in range(1, ML_CONV):
            y = y + w[ML_CONV - 1 - s:ML_CONV - s] * _shift_rows(x, prev8, s)
        prev_ref[...] = x[tt - SUBLANES:tt]
        return _silu(y)

    q_all = conv_silu(q_ref, qp_ref, w_all[:, :ML_WIDTH], b_all[:, :ML_WIDTH]) * (hd ** -0.5)
    k_all = conv_silu(k_ref, kp_ref, w_all[:, ML_WIDTH:], b_all[:, ML_WIDTH:])
    li_rows = [g_ref[h] + ib_ref[layer, h] for h in range(ML_HEADS)]
    lf_rows = [_log_sigmoid(g_ref[ML_HEADS + h] + fb_ref[layer, h]) for h in range(ML_HEADS)]

    L = chunk
    rows = lax.broadcasted_iota(jnp.int32, (L, L), 0)
    cols = lax.broadcasted_iota(jnp.int32, (L, L), 1)
    tri = rows >= cols
    eye = rows == cols
    for c in range(tt // L):
        sl = slice(c * L, (c + 1) * L)
        st = []
        for h in range(ML_HEADS):
            cs = slice(h * hd, (h + 1) * hd)
            li_row = li_rows[h][:, sl]
            lf_b = jnp.broadcast_to(lf_rows[h][:, sl], (L, L))
            li_b = jnp.broadcast_to(li_row, (L, L))
            b_col = jnp.sum(jnp.where(tri, lf_b, 0.0), axis=-1, keepdims=True)
            lf_col = jnp.sum(jnp.where(eye, lf_b, 0.0), axis=-1, keepdims=True)
            li_col = jnp.sum(jnp.where(eye, li_b, 0.0), axis=-1, keepdims=True)
            b_row = jnp.sum(jnp.where(rows <= cols, jnp.broadcast_to(lf_col, (L, L)), 0.0),
                            axis=0, keepdims=True)
            m_prev = m_ref[h][:, 0:1]
            dmat = jnp.where(tri, b_col - b_row + li_row, NEG_BIG)
            m_inter = b_col + m_prev
            m_t = jnp.maximum(m_inter, jnp.max(dmat, axis=-1, keepdims=True))
            st.append(dict(cs=cs, qc=q_all[sl, cs].astype(BF16), kc=k_all[sl, cs],
                           vc=v_ref[sl, cs].astype(BF16), pm=jnp.exp(dmat - m_t),
                           inter=jnp.exp(m_inter - m_t), m_t=m_t, b_col=b_col, li_col=li_col,
                           m_prev=m_prev))
        for s in st:
            s["smat"] = lax.dot_general(s["qc"], s["kc"].astype(BF16), _NT,
                                        preferred_element_type=F32) * s["pm"]
        for h, s in enumerate(st):
            ct = ct_ref[h]
            n_row = n_ref[h]
            num = (s["inter"] * jnp.dot(s["qc"], ct.astype(BF16), preferred_element_type=F32)
                   + jnp.dot(s["smat"].astype(BF16), s["vc"], preferred_element_type=F32))
            den = (s["inter"] * jnp.sum(s["qc"].astype(F32) * n_row, axis=-1, keepdims=True)
                   + jnp.sum(s["smat"], axis=-1, keepdims=True))
            s["hh"] = num / jnp.maximum(jnp.abs(den), jnp.exp(-s["m_t"]))
            b_last = s["b_col"][L - 1:L, :]
            g_col = b_last - s["b_col"] + s["li_col"]
            m_new = jnp.maximum(b_last + s["m_prev"], jnp.max(g_col, axis=0, keepdims=True))
            decay = jnp.exp(b_last + s["m_prev"] - m_new)
            kw = s["kc"] * jnp.exp(g_col - m_new)
            ct_ref[h] = decay * ct + jnp.dot(kw.T.astype(BF16), s["vc"],
                                             preferred_element_type=F32)
            n_ref[h] = decay * n_row + jnp.sum(kw, axis=0, keepdims=True)
            m_ref[h] = jnp.broadcast_to(m_new, (1, hd))
        for s in st:
            cs = s["cs"]
            hh = s["hh"]
            mu = jnp.mean(hh, axis=-1, keepdims=True)
            dd = hh - mu
            var = jnp.mean(dd * dd, axis=-1, keepdims=True)
            yn = dd * lax.rsqrt(var + NORM_EPS) * ng_ref[:, cs]
            out_ref[sl, cs] = (_silu(z_ref[sl, cs]) * (_sigmoid(o_ref[sl, cs]) * yn)).astype(BF16)


def _mlstm(proj, gates_t, i_bias, f_bias, conv_w, conv_b3, norm_g3, layer, batch, seq):
    m = proj.shape[0]
    tt = min(ML_TT, seq)
    nt = seq // tt
    hd = ML_HEAD_DIM
    w = ML_WIDTH

    def col(off):
        return pl.BlockSpec((tt, w), lambda b, t: (b * nt + t, off // w))

    smem = pl.BlockSpec(memory_space=pltpu.SMEM)
    return pl.pallas_call(
        functools.partial(_mlstm_kernel, layer=layer, tt=tt, chunk=min(ML_L, tt)),
        grid=(batch, nt),
        in_specs=[
            smem, smem,
            col(OFF_QK), col(OFF_QK + w), col(OFF_MLV), col(OFF_MLO), col(OFF_MLZ),
            pl.BlockSpec((None, 2 * ML_HEADS, 1, tt), lambda b, t: (b, 0, 0, t)),
            pl.BlockSpec((None, ML_CONV, 2 * w), lambda b, t: (layer, 0, 0)),
            pl.BlockSpec((None, 1, 2 * w), lambda b, t: (layer, 0, 0)),
            pl.BlockSpec((None, 1, w), lambda b, t: (layer, 0, 0)),
        ],
        out_specs=pl.BlockSpec((tt, w), lambda b, t: (b * nt + t, 0)),
        out_shape=jax.ShapeDtypeStruct((m, w), BF16),
        scratch_shapes=[
            pltpu.VMEM((ML_HEADS, hd, hd), F32),
            pltpu.VMEM((ML_HEADS, 1, hd), F32),
            pltpu.VMEM((ML_HEADS, 1, hd), F32),
            pltpu.VMEM((SUBLANES, w), F32),
            pltpu.VMEM((SUBLANES, w), F32),
        ],
        compiler_params=_params("parallel", "arbitrary"),
        name="mlstm",
    )(i_bias, f_bias, proj, proj, proj, proj, proj, gates_t, conv_w, conv_b3, norm_g3)


RW_PRE_PASSES = 3


def _rwkv_pre_kernel(*refs, has_vres, tm):
    if has_vres:
        (wa_ref, r_ref, k_ref, v_ref, mu_wa, mu_r, mu_k, mu_v, w0_ref, w2_ref, a0_ref, a2_ref,
         kk_ref, ka_ref, e_ref, vf_ref, v0_ref, v1_ref, v2_ref,
         r_out, ld_out, kh_out, v_out, kn_out, b_out, c_wa, c_r, c_k, c_v) = refs
    else:
        (wa_ref, r_ref, k_ref, v_ref, mu_wa, mu_r, mu_k, mu_v, w0_ref, w2_ref, a0_ref, a2_ref,
         kk_ref, ka_ref, e_ref,
         r_out, ld_out, kh_out, v_out, kn_out, b_out, c_wa, c_r, c_k, c_v) = refs

    @pl.when(pl.program_id(1) == 0)
    def _():
        for c in (c_wa, c_r, c_k, c_v):
            c[...] = jnp.zeros_like(c)

    def mix(x_ref, c_ref, mu_ref):
        x = x_ref[...]
        rolled = pltpu.roll(x, 1, 0)
        rid = lax.broadcasted_iota(jnp.int32, (SUBLANES, x.shape[1]), 0)
        prev_last = jnp.broadcast_to(c_ref[SUBLANES - 1:SUBLANES, :], (SUBLANES, x.shape[1]))
        top = jnp.where(rid == 0, prev_last, rolled[0:SUBLANES])
        xprev = jnp.concatenate([top, rolled[SUBLANES:]], axis=0)
        c_ref[...] = x[tm - SUBLANES:tm]
        return x + mu_ref[...] * (xprev - x)

    xwa = mix(wa_ref, c_wa, mu_wa)
    r = mix(r_ref, c_r, mu_r)
    k = mix(k_ref, c_k, mu_k)
    v = mix(v_ref, c_v, mu_v)
    zw = w0_ref[...] + _mm(jnp.tanh(xwa), w2_ref[...], RW_PRE_PASSES)
    za = a0_ref[...] + _mm(xwa, a2_ref[...], RW_PRE_PASSES)
    log_w = _log_sigmoid(zw) - 0.5
    ld_out[...] = -jnp.exp(log_w)
    a = _sigmoid(za)
    if has_vres:
        gate = _sigmoid(v0_ref[...] + _mm(_mm(v, v1_ref[...], RW_PRE_PASSES), v2_ref[...],
                                          RW_PRE_PASSES))
        v = v + (vf_ref[...] - v) * gate
    kk = k * kk_ref[...]
    ss = _mm(kk * kk, e_ref[...])
    kn = kk / jnp.maximum(jnp.sqrt(ss), 1e-12)
    r_out[...] = r
    kh_out[...] = k * (1.0 + (a - 1.0) * ka_ref[...])
    v_out[...] = v
    kn_out[...] = kn
    b_out[...] = kn * a


RW_GROUP = 4


def _rwkv_scan_kernel(r_ref, ld_ref, kh_ref, v_ref, kn_ref, b_ref, z_ref, rk_ref, g_ref, bias_ref,
                      e_ref, out_ref, m_ref, ac_ref, gc_ref, qs_ref, y_ref, *, tt, chunk):
    @pl.when(pl.program_id(1) == 0)
    def _():
        m_ref[...] = jnp.zeros_like(m_ref)

    L = chunk
    L2 = 2 * L
    nchunk = tt // L
    npair = RW_WIDTH // LANES
    rows = lax.broadcasted_iota(jnp.int32, (L, L), 0)
    cols = lax.broadcasted_iota(jnp.int32, (L, L), 1)
    tri_b = jnp.where(rows >= cols, 1.0, 0.0).astype(BF16)
    lane = lax.broadcasted_iota(jnp.int32, (1, LANES), 1)
    m0 = jnp.where(lane < RW_HEAD_DIM, 1.0, 0.0)
    m1 = 1.0 - m0
    r2 = lax.broadcasted_iota(jnp.int32, (L2, L2), 0)
    c2 = lax.broadcasted_iota(jnp.int32, (L2, L2), 1)
    same_blk = (r2 < L) == (c2 < L)
    t2 = jnp.where(r2 < L, r2, r2 - L)
    s2 = jnp.where(c2 < L, c2, c2 - L)
    mask_strict = jnp.logical_and(same_blk, t2 > s2)
    mask_incl = jnp.logical_and(same_blk, t2 >= s2)
    eye2 = lax.broadcasted_iota(jnp.int32, (LANES, LANES), 0) == lax.broadcasted_iota(
        jnp.int32, (LANES, LANES), 1)
    n_double = max(1, (L - 1).bit_length())

    def stack(x):
        return jnp.concatenate([x * m0, x * m1], axis=0)

    def prepare(p, c):
        sl = slice(c * L, (c + 1) * L)
        cs = slice(p * LANES, (p + 1) * LANES)
        ld = ld_ref[sl, cs]
        r, kh, v, kn, b = r_ref[sl, cs], kh_ref[sl, cs], v_ref[sl, cs], kn_ref[sl, cs], b_ref[sl, cs]
        ld_hi, ld_lo = _split_bf16(ld)
        logp = (jnp.dot(tri_b, ld_hi, preferred_element_type=F32)
                + jnp.dot(tri_b, ld_lo, preferred_element_type=F32))
        cmid = logp[L // 2 - 1:L // 2, :]
        e = logp - cmid
        e_last = e[L - 1:L, :]
        p_mid = jnp.exp(cmid)
        at_s = stack(-kn * jnp.exp(e - ld))
        rt_s = stack(r * jnp.exp(e))
        inv = jnp.exp(-e)
        tail = jnp.exp(e_last - e)
        lhs = jnp.concatenate([at_s, rt_s], axis=0).astype(BF16)
        rhs = jnp.concatenate([stack(b * inv), stack(kh * inv)], axis=0).astype(BF16)
        hat_t = jnp.concatenate([stack(b * tail), stack(kh * tail)], axis=0).T.astype(BF16)
        return dict(sl=sl, cs=cs, idx=p * nchunk + c, lhs=lhs, rhs=rhs, hat_t=hat_t,
                    at_true=at_s * p_mid, rt_true=rt_s * p_mid, v_s=stack(v).astype(BF16),
                    p_last=jnp.exp(e_last + cmid))

    def coeff_stages(cg):
        insts = [prepare(p, cg * RW_GROUP + g) for p in range(npair) for g in range(RW_GROUP)]
        for s in insts:
            aa = lax.dot_general(s["lhs"], s["rhs"], _NT, preferred_element_type=F32)
            s["nmat"] = jnp.where(mask_strict, aa[0:L2, 0:L2], 0.0).astype(BF16)
            s["a_ak"] = jnp.where(mask_strict, aa[0:L2, L2:], 0.0).astype(BF16)
            s["a_r"] = jnp.concatenate([jnp.where(mask_incl, aa[L2:, 0:L2], 0.0),
                                        jnp.where(mask_incl, aa[L2:, L2:], 0.0)],
                                       axis=1).astype(BF16)
        yield
        for s in insts:
            s["x"] = jnp.concatenate(
                [s["at_true"], jnp.dot(s["a_ak"], s["v_s"], preferred_element_type=F32)],
                axis=1)
        yield
        for it in range(n_double):
            for s in insts:
                s["x"] = s["x"] + jnp.dot(s["nmat"], s["x"].astype(BF16),
                                          preferred_element_type=F32)
            if it + 1 < n_double:
                for s in insts:
                    s["nmat"] = jnp.dot(s["nmat"], s["nmat"],
                                        preferred_element_type=F32).astype(BF16)
            yield
        for s in insts:
            big_l = jnp.concatenate([s["a_r"], s["hat_t"]], axis=0)
            big_r = jnp.concatenate(
                [s["x"].astype(BF16),
                 jnp.concatenate([jnp.zeros((L2, LANES), BF16), s["v_s"]], axis=1)], axis=0)
            res = jnp.dot(big_l, big_r, preferred_element_type=F32)
            q_s = s["rt_true"] + res[0:L2, :LANES]
            y0_s = res[0:L2, LANES:]
            qs_ref[s["sl"], s["cs"]] = q_s[0:L] + q_s[L:]
            y_ref[s["sl"], s["cs"]] = y0_s[0:L] + y0_s[L:]
            ac_ref[s["idx"]] = (jnp.where(eye2, jnp.broadcast_to(s["p_last"], (LANES, LANES)), 0.0)
                                + res[L2:, :LANES])
            gc_ref[s["idx"]] = res[L2:, LANES:]

    def chain_step(c):
        sl = slice(c * L, (c + 1) * L)
        for p in range(npair):
            cs = slice(p * LANES, (p + 1) * LANES)
            idx = p * nchunk + c
            mp = m_ref[p]
            y_ref[sl, cs] = _mm(qs_ref[sl, cs], mp) + y_ref[sl, cs]
            m_ref[p] = _mm(ac_ref[idx], mp) + gc_ref[idx]

    pending = []
    for cg in range(nchunk // RW_GROUP):
        for _ in coeff_stages(cg):
            if pending:
                chain_step(pending.pop(0))
        for c in pending:
            chain_step(c)
        pending = [cg * RW_GROUP + g for g in range(RW_GROUP)]
    for c in pending:
        chain_step(c)

    e_b = e_ref[...]
    inv_n = 1.0 / RW_HEAD_DIM

    def head_sum(x):
        return jnp.dot(x.astype(BF16), e_b, preferred_element_type=F32)

    for p in range(npair):
        cs = slice(p * LANES, (p + 1) * LANES)
        y = y_ref[:, cs]
        mu = head_sum(y) * inv_n
        d = y - mu
        var = head_sum(d * d) * inv_n
        yn = d * lax.rsqrt(var + RW_GN_EPS) * g_ref[:, cs] + bias_ref[:, cs]
        v_all = v_ref[:, cs]
        bonus = head_sum(r_ref[:, cs] * kh_ref[:, cs] * rk_ref[:, cs])
        out_ref[:, cs] = (_silu(z_ref[:, cs]) * (yn + bonus * v_all)).astype(BF16)


def _rwkv_kernel(*refs, has_vres, tt, chunk):
    n_mix = 19 if has_vres else 15
    mix_in = refs[:n_mix]
    z_ref, rk_ref, g_ref, bias_ref, e_ref = refs[n_mix:n_mix + 5]
    n_out = 1 if has_vres else 2
    outs = refs[n_mix + 5:n_mix + 5 + n_out]
    (c_wa, c_r, c_k, c_v, r_s, ld_s, kh_s, v_s, kn_s, b_s,
     m_ref, ac_ref, gc_ref, qs_ref, y_ref) = refs[n_mix + 5 + n_out:]
    _rwkv_pre_kernel(*mix_in, r_s, ld_s, kh_s, v_s, kn_s, b_s, c_wa, c_r, c_k, c_v,
                     has_vres=has_vres, tm=tt)
    if not has_vres:
        outs[1][...] = v_s[...]
    _rwkv_scan_kernel(r_s, ld_s, kh_s, v_s, kn_s, b_s, z_ref, rk_ref, g_ref, bias_ref, e_ref,
                      outs[0], m_ref, ac_ref, gc_ref, qs_ref, y_ref, tt=tt, chunk=chunk)


def _rwkv(proj, v_first, mu_wa, mu_r, mu_k, mu_v, w0, w2p, a0, a2p, k_k, k_a, e_head,
          v0, v1p, v2p, r_k3, ln_g3, ln_b3, e_pair, layer, batch, seq):
    m = proj.shape[0]
    tt = min(RW_TT, seq)
    nt = seq // tt
    chunk = min(RW_L, tt)
    has_vres = layer > 0
    w = RW_WIDTH
    npair = w // LANES

    def colblk(off, width):
        return pl.BlockSpec((tt, width), lambda b, t: (b * nt + t, off // width))

    def lay(shape):
        return pl.BlockSpec((None,) + shape, lambda b, t: (layer,) + (0,) * len(shape))

    def lay1(shape):
        return pl.BlockSpec((None,) + shape, lambda b, t: (layer - 1,) + (0,) * len(shape))

    row = pl.BlockSpec((tt, w), lambda b, t: (b * nt + t, 0))
    in_specs = [colblk(OFF_WA, LANES), colblk(OFF_RWR, w), colblk(OFF_RWK, w), colblk(OFF_RWV, w),
                lay((1, LANES)), lay((1, w)), lay((1, w)), lay((1, w)),
                lay((1, w)), lay((LANES, w)), lay((1, w)), lay((LANES, w)),
                lay((1, w)), lay((1, w)),
                pl.BlockSpec((w, w), lambda b, t: (0, 0))]
    args = [proj, proj, proj, proj, mu_wa, mu_r, mu_k, mu_v, w0, w2p, a0, a2p, k_k, k_a, e_head]
    if has_vres:
        in_specs += [row, lay1((1, w)), lay1((w, LANES)), lay1((LANES, w))]
        args += [v_first, v0, v1p, v2p]
    in_specs += [colblk(OFF_RWZ, w), lay((1, w)), lay((1, w)), lay((1, w)),
                 pl.BlockSpec((LANES, LANES), lambda b, t: (0, 0))]
    args += [proj, r_k3, ln_g3, ln_b3, e_pair]
    out_specs = [row]
    out_shape = [jax.ShapeDtypeStruct((m, w), BF16)]
    if not has_vres:
        out_specs.append(row)
        out_shape.append(jax.ShapeDtypeStruct((m, w), F32))
    scratch = [pltpu.VMEM((SUBLANES, LANES), F32)] + [pltpu.VMEM((SUBLANES, w), F32)] * 3
    scratch += [pltpu.VMEM((tt, w), F32)] * 6
    scratch += [
        pltpu.VMEM((npair, LANES, LANES), F32),
        pltpu.VMEM((npair * (tt // chunk), LANES, LANES), F32),
        pltpu.VMEM((npair * (tt // chunk), LANES, LANES), F32),
        pltpu.VMEM((tt, w), F32),
        pltpu.VMEM((tt, w), F32),
    ]
    return pl.pallas_call(
        functools.partial(_rwkv_kernel, has_vres=has_vres, tt=tt, chunk=chunk),
        grid=(batch, nt),
        in_specs=in_specs,
        out_specs=out_specs,
        out_shape=out_shape,
        scratch_shapes=scratch,
        compiler_params=_params("parallel", "arbitrary"),
        name="rwkv",
    )(*args)


_W_IN_PARTS = ((0, 1024), (1024, 512), (1544, 512), (2056, 512), (2568, 512), (3080, 256),
               (3336, 64), (1536, 8), (None, LANES - MLA_ROPE - 2 * ML_HEADS), (5960, 128),
               (4424, 512), (4936, 512), (5448, 512), (6088, 512), (3400, 1024))
TK_PREP = 256


def _permute_w_in_kernel(src_ref, dst_ref):
    tk = src_ref.shape[1]
    off = 0
    pending = []
    for start, width in _W_IN_PARTS:
        if start is None:
            pending.append(jnp.zeros((width, tk), F32))
        elif width % LANES:
            pending.append(src_ref[start:start + width, :])
        else:
            dst_ref[:, off:off + width] = src_ref[start:start + width, :].T.astype(BF16)
            off += width
            continue
        rows = sum(p.shape[0] for p in pending)
        if rows == LANES:
            dst_ref[:, off:off + LANES] = jnp.concatenate(pending, axis=0).T.astype(BF16)
            off += LANES
            pending = []


def _permute_w_in(w_in):
    depth, d_model, d_in = w_in.shape
    w_t = jnp.swapaxes(w_in, 1, 2)
    return pl.pallas_call(
        _permute_w_in_kernel,
        grid=(depth, d_model // TK_PREP),
        in_specs=[pl.BlockSpec((None, d_in, TK_PREP), lambda l, i: (l, 0, i))],
        out_specs=pl.BlockSpec((None, TK_PREP, D_IN_PAD), lambda l, i: (l, i, 0)),
        out_shape=jax.ShapeDtypeStruct((depth, d_model, D_IN_PAD), BF16),
        compiler_params=_params("parallel", "parallel"),
        name="permute_w_in",
    )(w_t)


def _permute_w_uq(w_uq):
    d = w_uq.shape[0]
    w = w_uq.reshape(d, MLA_Q_RANK, MLA_HEADS, MLA_NOPE + MLA_ROPE)
    w = jnp.pad(w, ((0, 0), (0, 0), (0, 0), (0, MLA_QK_PAD - MLA_NOPE - MLA_ROPE)))
    return w.reshape(d, MLA_Q_RANK, MLA_HEADS * MLA_QK_PAD).astype(BF16)


def _split_w_ukv(w_ukv):
    d = w_ukv.shape[0]
    w = w_ukv.reshape(d, MLA_KV_RANK, MLA_HEADS, MLA_NOPE + MLA_V)
    wk = w[..., :MLA_NOPE].reshape(d, MLA_KV_RANK, MLA_HEADS * MLA_NOPE)
    wv = w[..., MLA_NOPE:].reshape(d, MLA_KV_RANK, MLA_HEADS * MLA_V)
    return wk.astype(BF16), wv.astype(BF16)


def _rope_tables(positions):
    inv_freq = jnp.power(ROPE_BASE, -jnp.arange(0, MLA_ROPE, 2, dtype=F32) / MLA_ROPE)
    ang = positions.astype(F32).reshape(-1, 1) * inv_freq
    cos, sin = jnp.cos(ang), jnp.sin(ang)
    z32 = jnp.zeros_like(cos)
    z64 = jnp.zeros((ang.shape[0], LANES - MLA_ROPE), F32)
    cosf = jnp.concatenate([cos, cos, z64], axis=-1)
    s1 = jnp.concatenate([-sin, z32, z64], axis=-1)
    s2 = jnp.concatenate([z32, sin, z64], axis=-1)
    return cosf, s1, s2


def kernel(x, positions, norm_g, w_in, ml_conv_w, ml_conv_b, ml_i_bias, ml_f_bias, ml_norm_g,
           mla_q_norm_g, mla_w_uq, mla_kv_norm_g, mla_w_ukv, rw_mu, rw_w0, rw_w2, rw_a0, rw_a2,
           rw_v0, rw_v1, rw_v2, rw_k_k, rw_k_a, rw_r_k, rw_ln_g, rw_ln_b, w_out, final_norm_g):
    batch, seq, _ = x.shape
    depth = w_in.shape[0]
    m = batch * seq
    x2 = x.reshape(m, D_MODEL)

    w_in_p = _permute_w_in(w_in)
    w_uq_p = _permute_w_uq(mla_w_uq)
    w_uk_p, w_uv_p = _split_w_ukv(mla_w_ukv)
    w_out_b = w_out.astype(BF16)
    cosf, s1, s2 = _rope_tables(positions)

    def row3(a):
        return a.reshape(a.shape[0], 1, a.shape[1])

    norm_g3 = row3(norm_g)
    conv_b3 = row3(ml_conv_b)
    ml_norm_g3 = row3(ml_norm_g)
    qg3, kvg3 = row3(mla_q_norm_g), row3(mla_kv_norm_g)
    w = RW_WIDTH
    mu_r, mu_k, mu_v = row3(rw_mu[:, 0:w]), row3(rw_mu[:, w:2 * w]), row3(rw_mu[:, 2 * w:3 * w])
    mu_wa = row3(rw_mu[:, 3 * w:])
    w0, a0, k_k, k_a = row3(rw_w0), row3(rw_a0), row3(rw_k_k), row3(rw_k_a)
    w2p = jnp.pad(rw_w2, ((0, 0), (0, LANES - RW_DECAY_RANK), (0, 0)))
    a2p = jnp.pad(rw_a2, ((0, 0), (LANES - RW_AAA_RANK, 0), (0, 0)))
    v0 = row3(rw_v0)
    v1p = jnp.pad(rw_v1, ((0, 0), (0, 0), (0, LANES - RW_MV_RANK)))
    v2p = jnp.pad(rw_v2, ((0, 0), (0, LANES - RW_MV_RANK), (0, 0)))
    r_k3 = rw_r_k.reshape(depth, 1, w)
    ln_g3, ln_b3 = row3(rw_ln_g), row3(rw_ln_b)
    hid = jnp.arange(w) // RW_HEAD_DIM
    e_head = (hid[:, None] == hid[None, :]).astype(F32)
    e_pair = e_head[:LANES, :LANES].astype(BF16)
    final_g3 = final_norm_g.reshape(1, 1, D_MODEL)

    v_first = None
    h = _rmsnorm(x2, norm_g3, 0)
    for layer in range(depth):
        proj = _inproj(h, w_in_p, layer)
        gates_t = proj[:, OFF_KRG + MLA_ROPE:OFF_KRG + MLA_ROPE + 2 * ML_HEADS]
        gates_t = gates_t.reshape(batch, seq, 2 * ML_HEADS).transpose(0, 2, 1)
        gates_t = gates_t.reshape(batch, 2 * ML_HEADS, 1, seq)
        y_ml = _mlstm(proj, gates_t, ml_i_bias, ml_f_bias, ml_conv_w, conv_b3, ml_norm_g3,
                      layer, batch, seq)
        q, k, v = _mla_proj(proj, cosf, s1, s2, qg3, w_uq_p, kvg3, w_uk_p, w_uv_p, layer)
        y_mla = _attention(q, k, v, proj, batch, seq)
        rw_out = _rwkv(proj, v_first, mu_wa, mu_r, mu_k, mu_v, w0, w2p, a0, a2p, k_k, k_a, e_head,
                       v0, v1p, v2p, r_k3, ln_g3, ln_b3, e_pair, layer, batch, seq)
        y_rw = rw_out[0]
        if layer == 0:
            v_first = rw_out[1]
        if layer == depth - 1:
            (x2,) = _outproj(x2, y_ml, y_mla, y_rw, w_out_b, final_g3, layer, True)
        else:
            x2, h = _outproj(x2, y_ml, y_mla, y_rw, w_out_b, norm_g3, layer, False)
    return x2.reshape(batch, seq, D_MODEL)
```

```python
import functools

import jax
import jax.numpy as jnp
from jax import lax
from jax.experimental import pallas as pl
from jax.experimental.pallas import tpu as pltpu

F32 = jnp.float32
BF16 = jnp.bfloat16

D_MODEL = 2048
NORM_EPS = 1e-6
ML_HEADS = 4
ML_HEAD_DIM = 128
ML_WIDTH = 512
ML_CONV = 4
MLA_HEADS = 8
MLA_NOPE = 128
MLA_ROPE = 64
MLA_V = 128
MLA_WIDTH = 1024
MLA_Q_RANK = 512
MLA_KV_RANK = 256
ROPE_BASE = 10000.0
RW_HEAD_DIM = 64
RW_WIDTH = 512
RW_HEADS = 8
RW_DECAY_RANK = 64
RW_AAA_RANK = 64
RW_MV_RANK = 32
RW_GN_EPS = 64e-5

LANES = 128
SUBLANES = 8
MLA_QK_PAD = 256
VMEM_LIMIT = 48 * 1024 * 1024
NEG_BIG = -1e30
LOG2E = 1.4426950408889634

OFF_QK = 0
OFF_MLV = 1024
OFF_MLO = 1536
OFF_MLZ = 2048
OFF_CQ = 2560
OFF_CKV = 3072
OFF_KRG = 3328
OFF_WA = 3456
D_IN_A = 3584
OFF_RWR = 0
OFF_RWK = 512
OFF_RWV = 1024
OFF_RWZ = 1536
OFF_MLAZ = 2048
D_IN_B = 3072

TM_PROJ = 1024
TN_PROJ_A = 1792
TN_PROJ_B = 1536
TM_OUT = 512
TQ = 512
ML_TT = 1024
ML_L = 256
RW_TT = 512
RW_L = 64


def _split_bf16(x):
    hi = x.astype(BF16)
    lo = (x - hi.astype(F32)).astype(BF16)
    return hi, lo


def _mm(a, b, passes=1):
    if passes == 1:
        return jnp.dot(a.astype(BF16), b.astype(BF16), preferred_element_type=F32)
    ah, al = _split_bf16(a)
    bh, bl = _split_bf16(b)
    return (jnp.dot(ah, bh, preferred_element_type=F32)
            + jnp.dot(ah, bl, preferred_element_type=F32)
            + jnp.dot(al, bh, preferred_element_type=F32))


_NT = (((1,), (1,)), ((), ()))


def _mm_nt(a, b, passes=1):
    if passes == 1:
        return lax.dot_general(a.astype(BF16), b.astype(BF16), _NT, preferred_element_type=F32)
    ah, al = _split_bf16(a)
    bh, bl = _split_bf16(b)
    return (lax.dot_general(ah, bh, _NT, preferred_element_type=F32)
            + lax.dot_general(ah, bl, _NT, preferred_element_type=F32)
            + lax.dot_general(al, bh, _NT, preferred_element_type=F32))


def _sigmoid(x):
    return 0.5 * jnp.tanh(0.5 * x) + 0.5


def _silu(x):
    h = 0.5 * x
    return h * jnp.tanh(h) + h


def _log_sigmoid(x):
    return jnp.minimum(x, 0.0) - jnp.log(1.0 + jnp.exp(-jnp.abs(x)))


def _params(*sem):
    return pltpu.CompilerParams(dimension_semantics=sem, vmem_limit_bytes=VMEM_LIMIT)


def _rms_scale(x, g):
    return x * lax.rsqrt(jnp.mean(x * x, axis=-1, keepdims=True) + NORM_EPS) * g


def _rmsnorm_kernel(x_ref, g_ref, h_ref):
    h_ref[...] = _rms_scale(x_ref[...], g_ref[...]).astype(BF16)


def _rmsnorm(x2, norm_g3, layer):
    m = x2.shape[0]
    return pl.pallas_call(
        _rmsnorm_kernel,
        grid=(m // TM_OUT,),
        in_specs=[pl.BlockSpec((TM_OUT, D_MODEL), lambda i: (i, 0)),
                  pl.BlockSpec((None, 1, D_MODEL), lambda i: (layer, 0, 0))],
        out_specs=pl.BlockSpec((TM_OUT, D_MODEL), lambda i: (i, 0)),
        out_shape=jax.ShapeDtypeStruct((m, D_MODEL), BF16),
        compiler_params=_params("parallel"),
        name="rmsnorm",
    )(x2, norm_g3)


def _inproj_kernel(h_ref, w_ref, o_ref):
    o_ref[...] = jnp.dot(h_ref[...], w_ref[...], preferred_element_type=F32)


def _inproj(h, w_in_p, layer, tn):
    m = h.shape[0]
    d_out = w_in_p.shape[2]
    return pl.pallas_call(
        _inproj_kernel,
        grid=(m // TM_PROJ, d_out // tn),
        in_specs=[
            pl.BlockSpec((TM_PROJ, D_MODEL), lambda i, j: (i, 0)),
            pl.BlockSpec((None, D_MODEL, tn), lambda i, j: (layer, 0, j)),
        ],
        out_specs=pl.BlockSpec((TM_PROJ, tn), lambda i, j: (i, j)),
        out_shape=jax.ShapeDtypeStruct((m, d_out), F32),
        compiler_params=_params("parallel", "arbitrary"),
        name="inproj",
    )(h, w_in_p)


def _outproj_kernel(x_ref, yml_ref, ymla_ref, yrw_ref, w_ref, g_ref, *out_refs, final):
    acc = x_ref[...]
    acc = acc + jnp.dot(yml_ref[...], w_ref[0:ML_WIDTH, :], preferred_element_type=F32)
    acc = acc + jnp.dot(ymla_ref[...], w_ref[ML_WIDTH:ML_WIDTH + MLA_WIDTH, :],
                        preferred_element_type=F32)
    acc = acc + jnp.dot(yrw_ref[...], w_ref[ML_WIDTH + MLA_WIDTH:, :], preferred_element_type=F32)
    normed = _rms_scale(acc, g_ref[...])
    if final:
        out_refs[0][...] = normed
    else:
        out_refs[0][...] = acc
        out_refs[1][...] = normed.astype(BF16)


def _outproj(x2, y_ml, y_mla, y_rw, w_out_b, gains3, layer, final):
    m = x2.shape[0]
    g_index = 0 if final else layer + 1
    row = pl.BlockSpec((TM_OUT, D_MODEL), lambda i: (i, 0))
    out_specs = [row] if final else [row, row]
    out_shape = [jax.ShapeDtypeStruct((m, D_MODEL), F32)]
    if not final:
        out_shape.append(jax.ShapeDtypeStruct((m, D_MODEL), BF16))
    return pl.pallas_call(
        functools.partial(_outproj_kernel, final=final),
        grid=(m // TM_OUT,),
        in_specs=[
            row,
            pl.BlockSpec((TM_OUT, ML_WIDTH), lambda i: (i, 0)),
            pl.BlockSpec((TM_OUT, MLA_WIDTH), lambda i: (i, 0)),
            pl.BlockSpec((TM_OUT, RW_WIDTH), lambda i: (i, 0)),
            pl.BlockSpec((None, D_MODEL, D_MODEL), lambda i: (layer, 0, 0)),
            pl.BlockSpec((None, 1, D_MODEL), lambda i: (g_index, 0, 0)),
        ],
        out_specs=out_specs,
        out_shape=out_shape,
        compiler_params=_params("parallel"),
        name="outproj",
    )(x2, y_ml, y_mla, y_rw, w_out_b, gains3)


def _mla_proj_kernel(cq_ref, ckv_ref, krg_ref, cos_ref, s1_ref, s2_ref, qg_ref, wq_ref, kvg_ref,
                     wk_ref, wv_ref, q_out, k_out, v_out):
    cq = cq_ref[...]
    cqn = (cq * lax.rsqrt(jnp.mean(cq * cq, axis=-1, keepdims=True) + NORM_EPS)
           * qg_ref[...]).astype(BF16)
    q = jnp.dot(cqn, wq_ref[...], preferred_element_type=F32)
    ckv = ckv_ref[...]
    ckvn = (ckv * lax.rsqrt(jnp.mean(ckv * ckv, axis=-1, keepdims=True) + NORM_EPS)
            * kvg_ref[...]).astype(BF16)
    kn = jnp.dot(ckvn, wk_ref[...], preferred_element_type=F32)
    v = jnp.dot(ckvn, wv_ref[...], preferred_element_type=F32)
    cosf, s1, s2 = cos_ref[...], s1_ref[...], s2_ref[...]

    def rope(xb):
        return (xb * cosf + pltpu.roll(xb, LANES - MLA_ROPE // 2, 1) * s1
                + pltpu.roll(xb, MLA_ROPE // 2, 1) * s2)

    scale = (MLA_NOPE + MLA_ROPE) ** -0.5 * LOG2E
    kr = rope(krg_ref[...]).astype(BF16)
    for h in range(MLA_HEADS):
        c0 = h * MLA_QK_PAD
        q_out[:, c0:c0 + MLA_NOPE] = (q[:, c0:c0 + MLA_NOPE] * scale).astype(BF16)
        q_out[:, c0 + MLA_NOPE:c0 + MLA_QK_PAD] = (
            rope(q[:, c0 + MLA_NOPE:c0 + MLA_QK_PAD]) * scale).astype(BF16)
        k_out[:, c0:c0 + MLA_NOPE] = kn[:, h * MLA_NOPE:(h + 1) * MLA_NOPE].astype(BF16)
        k_out[:, c0 + MLA_NOPE:c0 + MLA_QK_PAD] = kr
    v_out[...] = v.T.astype(BF16)


def _mla_proj(proj, cosf, s1, s2, qg3, wq_p, kvg3, wk_p, wv_p, layer):
    m = proj.shape[0]
    tm = TQ
    row = lambda i: (i, 0)
    return pl.pallas_call(
        _mla_proj_kernel,
        grid=(m // tm,),
        in_specs=[
            pl.BlockSpec((tm, MLA_Q_RANK), lambda i: (i, OFF_CQ // MLA_Q_RANK)),
            pl.BlockSpec((tm, MLA_KV_RANK), lambda i: (i, OFF_CKV // MLA_KV_RANK)),
            pl.BlockSpec((tm, LANES), lambda i: (i, OFF_KRG // LANES)),
            pl.BlockSpec((tm, LANES), row),
            pl.BlockSpec((tm, LANES), row),
            pl.BlockSpec((tm, LANES), row),
            pl.BlockSpec((None, 1, MLA_Q_RANK), lambda i: (layer, 0, 0)),
            pl.BlockSpec((None, MLA_Q_RANK, MLA_HEADS * MLA_QK_PAD), lambda i: (layer, 0, 0)),
            pl.BlockSpec((None, 1, MLA_KV_RANK), lambda i: (layer, 0, 0)),
            pl.BlockSpec((None, MLA_KV_RANK, MLA_HEADS * MLA_NOPE), lambda i: (layer, 0, 0)),
            pl.BlockSpec((None, MLA_KV_RANK, MLA_HEADS * MLA_V), lambda i: (layer, 0, 0)),
        ],
        out_specs=[
            pl.BlockSpec((tm, MLA_HEADS * MLA_QK_PAD), row),
            pl.BlockSpec((tm, MLA_HEADS * MLA_QK_PAD), row),
            pl.BlockSpec((None, MLA_HEADS * MLA_V, tm), lambda i: (i, 0, 0)),
        ],
        out_shape=[
            jax.ShapeDtypeStruct((m, MLA_HEADS * MLA_QK_PAD), BF16),
            jax.ShapeDtypeStruct((m, MLA_HEADS * MLA_QK_PAD), BF16),
            jax.ShapeDtypeStruct((m // tm, MLA_HEADS * MLA_V, tm), BF16),
        ],
        compiler_params=_params("parallel"),
        name="mla_proj",
    )(proj, proj, proj, cosf, s1, s2, qg3, wq_p, kvg3, wk_p, wv_p)


ATTN_HEADS_PER_STEP = 4


def _attn_kernel(q_ref, k_ref, vt_ref, z_ref, o_ref, *, tq):
    i = pl.program_id(2)
    nh = ATTN_HEADS_PER_STEP
    th = tq // 2
    chains = [(h, u) for h in range(nh) for u in range(2)]

    def head_cols(h):
        return slice(h * MLA_QK_PAD, (h + 1) * MLA_QK_PAD)

    def softmax_pv(carry_c, s, vt):
        m, l, acc = carry_c
        m_new = jnp.maximum(m, jnp.max(s, axis=0, keepdims=True))
        alpha = jnp.exp2(m - m_new)
        p = jnp.exp2(s - m_new)
        l = alpha * l + jnp.sum(p, axis=0, keepdims=True)
        acc = alpha * acc + jnp.dot(vt, p.astype(BF16), preferred_element_type=F32)
        return m_new, l, acc

    def step(j, carry):
        off = pl.multiple_of(j * tq, tq)
        ss = []
        for h in range(nh):
            s = lax.dot_general(k_ref[pl.ds(off, tq), head_cols(h)], q_ref[:, head_cols(h)], _NT,
                                preferred_element_type=F32)
            ss += [s[:, :th], s[:, th:]]
        return tuple(softmax_pv(carry[n], ss[n], vt_ref[j, chains[n][0]])
                     for n in range(len(chains)))

    def diagonal_step(carry):
        off = pl.multiple_of(i * tq, tq)
        ss, vts = [], []
        for h, u in chains:
            nk = th if u == 0 else tq
            s = lax.dot_general(k_ref[pl.ds(off, nk), head_cols(h)],
                                q_ref[u * th:(u + 1) * th, head_cols(h)], _NT,
                                preferred_element_type=F32)
            kidx = lax.broadcasted_iota(jnp.int32, (nk, th), 0)
            qidx = lax.broadcasted_iota(jnp.int32, (nk, th), 1) + u * th
            ss.append(jnp.where(kidx <= qidx, s, NEG_BIG))
            vts.append(vt_ref[i, h][:, :nk])
        return tuple(softmax_pv(carry[n], ss[n], vts[n]) for n in range(len(chains)))

    init = tuple((jnp.full((1, th), NEG_BIG, F32), jnp.zeros((1, th), F32),
                  jnp.zeros((MLA_V, th), F32)) for _ in chains)
    carry = lax.fori_loop(0, i, step, init)
    carry = diagonal_step(carry)
    for n, (h, u) in enumerate(chains):
        _, l, acc = carry[n]
        y = (acc / l).T
        rs = slice(u * th, (u + 1) * th)
        cs = slice(h * MLA_V, (h + 1) * MLA_V)
        o_ref[rs, cs] = (_silu(z_ref[rs, cs]) * y).astype(BF16)


def _attention(q, k, v_t, proj, batch, seq):
    m = q.shape[0]
    nq = seq // TQ
    nh = ATTN_HEADS_PER_STEP
    v_t5 = v_t.reshape(batch, nq, MLA_HEADS, MLA_V, TQ)
    return pl.pallas_call(
        functools.partial(_attn_kernel, tq=TQ),
        grid=(batch, MLA_HEADS // nh, nq),
        in_specs=[
            pl.BlockSpec((TQ, nh * MLA_QK_PAD), lambda b, h, i: (b * nq + i, h)),
            pl.BlockSpec((seq, nh * MLA_QK_PAD), lambda b, h, i: (b, h)),
            pl.BlockSpec((None, nq, nh, MLA_V, TQ), lambda b, h, i: (b, 0, h, 0, 0)),
            pl.BlockSpec((TQ, nh * MLA_V),
                         lambda b, h, i: (b * nq + i, OFF_MLAZ // (nh * MLA_V) + h)),
        ],
        out_specs=pl.BlockSpec((TQ, nh * MLA_V), lambda b, h, i: (b * nq + i, h)),
        out_shape=jax.ShapeDtypeStruct((m, MLA_WIDTH), BF16),
        compiler_params=_params("parallel", "parallel", "arbitrary"),
        name="mla_attn",
    )(q, k, v_t5, proj)


def _shift_rows(x, prev8, s):
    rolled = pltpu.roll(x, s, 0)
    prev_rolled = pltpu.roll(prev8, s, 0)
    rid = lax.broadcasted_iota(jnp.int32, (SUBLANES, x.shape[1]), 0)
    top = jnp.where(rid < s, prev_rolled, rolled[0:SUBLANES])
    return jnp.concatenate([top, rolled[SUBLANES:]], axis=0)


def _mlstm_kernel(ib_ref, fb_ref, q_ref, k_ref, v_ref, o_ref, z_ref, g_ref, w_ref, b_ref, ng_ref,
                  out_ref, ct_ref, n_ref, m_ref, qp_ref, kp_ref, *, layer, tt, chunk):
    @pl.when(pl.program_id(1) == 0)
    def _():
        for ref in (ct_ref, n_ref, m_ref, qp_ref, kp_ref):
            ref[...] = jnp.zeros_like(ref)

    hd = ML_HEAD_DIM
    w_all = w_ref[...]
    b_all = b_ref[...]

    def conv_silu(x_ref, prev_ref, w, b):
        x = x_ref[...]
        prev8 = prev_ref[...]
        y = b + w[ML_CONV - 1:ML_CONV] * x
        for s in range(1, ML_CONV):
            y = y + w[ML_CONV - 1 - s:ML_CONV - s] * _shift_rows(x, prev8, s)
        prev_ref[...] = x[tt - SUBLANES:tt]
        return _silu(y)

    q_all = conv_silu(q_ref, qp_ref, w_all[:, :ML_WIDTH], b_all[:, :ML_WIDTH]) * (hd ** -0.5)
    k_all = conv_silu(k_ref, kp_ref, w_all[:, ML_WIDTH:], b_all[:, ML_WIDTH:])
    li_rows = [g_ref[h] + ib_ref[layer, h] for h in range(ML_HEADS)]
    lf_rows = [_log_sigmoid(g_ref[ML_HEADS + h] + fb_ref[layer, h]) for h in range(ML_HEADS)]

    L = chunk
    rows = lax.broadcasted_iota(jnp.int32, (L, L), 0)
    cols = lax.broadcasted_iota(jnp.int32, (L, L), 1)
    tri = rows >= cols
    eye = rows == cols
    for c in range(tt // L):
        sl = slice(c * L, (c + 1) * L)
        st = []
        for h in range(ML_HEADS):
            cs = slice(h * hd, (h + 1) * hd)
            li_row = li_rows[h][:, sl]
            lf_b = jnp.broadcast_to(lf_rows[h][:, sl], (L, L))
            li_b = jnp.broadcast_to(li_row, (L, L))
            b_col = jnp.sum(jnp.where(tri, lf_b, 0.0), axis=-1, keepdims=True)
            lf_col = jnp.sum(jnp.where(eye, lf_b, 0.0), axis=-1, keepdims=True)
            li_col = jnp.sum(jnp.where(eye, li_b, 0.0), axis=-1, keepdims=True)
            b_row = jnp.sum(jnp.where(rows <= cols, jnp.broadcast_to(lf_col, (L, L)), 0.0),
                            axis=0, keepdims=True)
            m_prev = m_ref[h][:, 0:1]
            dmat = jnp.where(tri, b_col - b_row + li_row, NEG_BIG)
            m_inter = b_col + m_prev
            m_t = jnp.maximum(m_inter, jnp.max(dmat, axis=-1, keepdims=True))
            st.append(dict(cs=cs, qc=q_all[sl, cs].astype(BF16), kc=k_all[sl, cs],
                           vc=v_ref[sl, cs].astype(BF16), pm=jnp.exp(dmat - m_t),
                           inter=jnp.exp(m_inter - m_t), m_t=m_t, b_col=b_col, li_col=li_col,
                           m_prev=m_prev))
        for s in st:
            s["smat"] = lax.dot_general(s["qc"], s["kc"].astype(BF16), _NT,
                                        preferred_element_type=F32) * s["pm"]
        for h, s in enumerate(st):
            ct = ct_ref[h]
            n_row = n_ref[h]
            num = (s["inter"] * jnp.dot(s["qc"], ct.astype(BF16), preferred_element_type=F32)
                   + jnp.dot(s["smat"].astype(BF16), s["vc"], preferred_element_type=F32))
            den = (s["inter"] * jnp.sum(s["qc"].astype(F32) * n_row, axis=-1, keepdims=True)
                   + jnp.sum(s["smat"], axis=-1, keepdims=True))
            s["hh"] = num / jnp.maximum(jnp.abs(den), jnp.exp(-s["m_t"]))
            b_last = s["b_col"][L - 1:L, :]
            g_col = b_last - s["b_col"] + s["li_col"]
            m_new = jnp.maximum(b_last + s["m_prev"], jnp.max(g_col, axis=0, keepdims=True))
            decay = jnp.exp(b_last + s["m_prev"] - m_new)
            kw = s["kc"] * jnp.exp(g_col - m_new)
            ct_ref[h] = decay * ct + jnp.dot(kw.T.astype(BF16), s["vc"],
                                             preferred_element_type=F32)
            n_ref[h] = decay * n_row + jnp.sum(kw, axis=0, keepdims=True)
            m_ref[h] = jnp.broadcast_to(m_new, (1, hd))
        for s in st:
            cs = s["cs"]
            hh = s["hh"]
            mu = jnp.mean(hh, axis=-1, keepdims=True)
            dd = hh - mu
            var = jnp.mean(dd * dd, axis=-1, keepdims=True)
            yn = dd * lax.rsqrt(var + NORM_EPS) * ng_ref[:, cs]
            out_ref[sl, cs] = (_silu(z_ref[sl, cs]) * (_sigmoid(o_ref[sl, cs]) * yn)).astype(BF16)


def _mlstm(proj, gates_t, i_bias, f_bias, conv_w, conv_b3, norm_g3, layer, batch, seq):
    m = proj.shape[0]
    tt = min(ML_TT, seq)
    nt = seq // tt
    hd = ML_HEAD_DIM
    w = ML_WIDTH

    def col(off):
        return pl.BlockSpec((tt, w), lambda b, t: (b * nt + t, off // w))

    smem = pl.BlockSpec(memory_space=pltpu.SMEM)
    return pl.pallas_call(
        functools.partial(_mlstm_kernel, layer=layer, tt=tt, chunk=min(ML_L, tt)),
        grid=(batch, nt),
        in_specs=[
            smem, smem,
            col(OFF_QK), col(OFF_QK + w), col(OFF_MLV), col(OFF_MLO), col(OFF_MLZ),
            pl.BlockSpec((None, 2 * ML_HEADS, 1, tt), lambda b, t: (b, 0, 0, t)),
            pl.BlockSpec((None, ML_CONV, 2 * w), lambda b, t: (layer, 0, 0)),
            pl.BlockSpec((None, 1, 2 * w), lambda b, t: (layer, 0, 0)),
            pl.BlockSpec((None, 1, w), lambda b, t: (layer, 0, 0)),
        ],
        out_specs=pl.BlockSpec((tt, w), lambda b, t: (b * nt + t, 0)),
        out_shape=jax.ShapeDtypeStruct((m, w), BF16),
        scratch_shapes=[
            pltpu.VMEM((ML_HEADS, hd, hd), F32),
            pltpu.VMEM((ML_HEADS, 1, hd), F32),
            pltpu.VMEM((ML_HEADS, 1, hd), F32),
            pltpu.VMEM((SUBLANES, w), F32),
            pltpu.VMEM((SUBLANES, w), F32),
        ],
        compiler_params=_params("parallel", "arbitrary"),
        name="mlstm",
    )(i_bias, f_bias, proj, proj, proj, proj, proj, gates_t, conv_w, conv_b3, norm_g3)


RW_PRE_PASSES = 3


def _rwkv_pre_kernel(*refs, has_vres, tm):
    if has_vres:
        (wa_ref, r_ref, k_ref, v_ref, mu_wa, mu_r, mu_k, mu_v, w0_ref, w2_ref, a0_ref, a2_ref,
         kk_ref, ka_ref, e_ref, vf_ref, v0_ref, v1_ref, v2_ref,
         r_out, ld_out, kh_out, v_out, kn_out, b_out, c_wa, c_r, c_k, c_v) = refs
    else:
        (wa_ref, r_ref, k_ref, v_ref, mu_wa, mu_r, mu_k, mu_v, w0_ref, w2_ref, a0_ref, a2_ref,
         kk_ref, ka_ref, e_ref,
         r_out, ld_out, kh_out, v_out, kn_out, b_out, c_wa, c_r, c_k, c_v) = refs

    @pl.when(pl.program_id(1) == 0)
    def _():
        for c in (c_wa, c_r, c_k, c_v):
            c[...] = jnp.zeros_like(c)

    def mix(x_ref, c_ref, mu_ref):
        x = x_ref[...]
        rolled = pltpu.roll(x, 1, 0)
        rid = lax.broadcasted_iota(jnp.int32, (SUBLANES, x.shape[1]), 0)
        prev_last = jnp.broadcast_to(c_ref[SUBLANES - 1:SUBLANES, :], (SUBLANES, x.shape[1]))
        top = jnp.where(rid == 0, prev_last, rolled[0:SUBLANES])
        xprev = jnp.concatenate([top, rolled[SUBLANES:]], axis=0)
        c_ref[...] = x[tm - SUBLANES:tm]
        return x + mu_ref[...] * (xprev - x)

    xwa = mix(wa_ref, c_wa, mu_wa)
    r = mix(r_ref, c_r, mu_r)
    k = mix(k_ref, c_k, mu_k)
    v = mix(v_ref, c_v, mu_v)
    zw = w0_ref[...] + _mm(jnp.tanh(xwa), w2_ref[...], RW_PRE_PASSES)
    za = a0_ref[...] + _mm(xwa, a2_ref[...], RW_PRE_PASSES)
    log_w = _log_sigmoid(zw) - 0.5
    ld_out[...] = -jnp.exp(log_w)
    a = _sigmoid(za)
    if has_vres:
        gate = _sigmoid(v0_ref[...] + _mm(_mm(v, v1_ref[...], RW_PRE_PASSES), v2_ref[...],
                                          RW_PRE_PASSES))
        v = v + (vf_ref[...] - v) * gate
    kk = k * kk_ref[...]
    ss = _mm(kk * kk, e_ref[...])
    kn = kk / jnp.maximum(jnp.sqrt(ss), 1e-12)
    r_out[...] = r
    kh_out[...] = k * (1.0 + (a - 1.0) * ka_ref[...])
    v_out[...] = v
    kn_out[...] = kn
    b_out[...] = kn * a


RW_GROUP = 4


def _rwkv_scan_kernel(r_ref, ld_ref, kh_ref, v_ref, kn_ref, b_ref, z_ref, rk_ref, g_ref, bias_ref,
                      e_ref, out_ref, m_ref, ac_ref, gc_ref, qs_ref, y_ref, *, tt, chunk):
    @pl.when(pl.program_id(1) == 0)
    def _():
        m_ref[...] = jnp.zeros_like(m_ref)

    L = chunk
    L2 = 2 * L
    nchunk = tt // L
    npair = RW_WIDTH // LANES
    rows = lax.broadcasted_iota(jnp.int32, (L, L), 0)
    cols = lax.broadcasted_iota(jnp.int32, (L, L), 1)
    tri_b = jnp.where(rows >= cols, 1.0, 0.0).astype(BF16)
    lane = lax.broadcasted_iota(jnp.int32, (1, LANES), 1)
    m0 = jnp.where(lane < RW_HEAD_DIM, 1.0, 0.0)
    m1 = 1.0 - m0
    r2 = lax.broadcasted_iota(jnp.int32, (L2, L2), 0)
    c2 = lax.broadcasted_iota(jnp.int32, (L2, L2), 1)
    same_blk = (r2 < L) == (c2 < L)
    t2 = jnp.where(r2 < L, r2, r2 - L)
    s2 = jnp.where(c2 < L, c2, c2 - L)
    mask_strict = jnp.logical_and(same_blk, t2 > s2)
    mask_incl = jnp.logical_and(same_blk, t2 >= s2)
    eye2 = lax.broadcasted_iota(jnp.int32, (LANES, LANES), 0) == lax.broadcasted_iota(
        jnp.int32, (LANES, LANES), 1)
    n_double = max(1, (L - 1).bit_length())

    def stack(x):
        return jnp.concatenate([x * m0, x * m1], axis=0)

    def prepare(p, c):
        sl = slice(c * L, (c + 1) * L)
        cs = slice(p * LANES, (p + 1) * LANES)
        ld = ld_ref[sl, cs]
        r, kh, v, kn, b = r_ref[sl, cs], kh_ref[sl, cs], v_ref[sl, cs], kn_ref[sl, cs], b_ref[sl, cs]
        ld_hi, ld_lo = _split_bf16(ld)
        logp = (jnp.dot(tri_b, ld_hi, preferred_element_type=F32)
                + jnp.dot(tri_b, ld_lo, preferred_element_type=F32))
        cmid = logp[L // 2 - 1:L // 2, :]
        e = logp - cmid
        e_last = e[L - 1:L, :]
        p_mid = jnp.exp(cmid)
        at_s = stack(-kn * jnp.exp(e - ld))
        rt_s = stack(r * jnp.exp(e))
        inv = jnp.exp(-e)
        tail = jnp.exp(e_last - e)
        lhs = jnp.concatenate([at_s, rt_s], axis=0).astype(BF16)
        rhs = jnp.concatenate([stack(b * inv), stack(kh * inv)], axis=0).astype(BF16)
        hat_t = jnp.concatenate([stack(b * tail), stack(kh * tail)], axis=0).T.astype(BF16)
        return dict(sl=sl, cs=cs, idx=p * nchunk + c, lhs=lhs, rhs=rhs, hat_t=hat_t,
                    at_true=at_s * p_mid, rt_true=rt_s * p_mid, v_s=stack(v).astype(BF16),
                    p_last=jnp.exp(e_last + cmid))

    def coeff_stages(cg):
        insts = [prepare(p, cg * RW_GROUP + g) for p in range(npair) for g in range(RW_GROUP)]
        for s in insts:
            aa = lax.dot_general(s["lhs"], s["rhs"], _NT, preferred_element_type=F32)
            s["nmat"] = jnp.where(mask_strict, aa[0:L2, 0:L2], 0.0).astype(BF16)
            s["a_ak"] = jnp.where(mask_strict, aa[0:L2, L2:], 0.0).astype(BF16)
            s["a_r"] = jnp.concatenate([jnp.where(mask_incl, aa[L2:, 0:L2], 0.0),
                                        jnp.where(mask_incl, aa[L2:, L2:], 0.0)],
                                       axis=1).astype(BF16)
        yield
        for s in insts:
            s["x"] = jnp.concatenate(
                [s["at_true"], jnp.dot(s["a_ak"], s["v_s"], preferred_element_type=F32)],
                axis=1)
        yield
        for it in range(n_double):
            for s in insts:
                s["x"] = s["x"] + jnp.dot(s["nmat"], s["x"].astype(BF16),
                                          preferred_element_type=F32)
            if it + 1 < n_double:
                for s in insts:
                    s["nmat"] = jnp.dot(s["nmat"], s["nmat"],
                                        preferred_element_type=F32).astype(BF16)
            yield
        for s in insts:
            big_l = jnp.concatenate([s["a_r"], s["hat_t"]], axis=0)
            big_r = jnp.concatenate(
                [s["x"].astype(BF16),
                 jnp.concatenate([jnp.zeros((L2, LANES), BF16), s["v_s"]], axis=1)], axis=0)
            res = jnp.dot(big_l, big_r, preferred_element_type=F32)
            q_s = s["rt_true"] + res[0:L2, :LANES]
            y0_s = res[0:L2, LANES:]
            qs_ref[s["sl"], s["cs"]] = q_s[0:L] + q_s[L:]
            y_ref[s["sl"], s["cs"]] = y0_s[0:L] + y0_s[L:]
            ac_ref[s["idx"]] = (jnp.where(eye2, jnp.broadcast_to(s["p_last"], (LANES, LANES)), 0.0)
                                + res[L2:, :LANES])
            gc_ref[s["idx"]] = res[L2:, LANES:]

    def chain_step(c):
        sl = slice(c * L, (c + 1) * L)
        for p in range(npair):
            cs = slice(p * LANES, (p + 1) * LANES)
            idx = p * nchunk + c
            mp = m_ref[p]
            y_ref[sl, cs] = _mm(qs_ref[sl, cs], mp) + y_ref[sl, cs]
            m_ref[p] = _mm(ac_ref[idx], mp) + gc_ref[idx]

    pending = []
    for cg in range(nchunk // RW_GROUP):
        for _ in coeff_stages(cg):
            if pending:
                chain_step(pending.pop(0))
        for c in pending:
            chain_step(c)
        pending = [cg * RW_GROUP + g for g in range(RW_GROUP)]
    for c in pending:
        chain_step(c)

    e_b = e_ref[...]
    inv_n = 1.0 / RW_HEAD_DIM

    def head_sum(x):
        return jnp.dot(x.astype(BF16), e_b, preferred_element_type=F32)

    for p in range(npair):
        cs = slice(p * LANES, (p + 1) * LANES)
        y = y_ref[:, cs]
        mu = head_sum(y) * inv_n
        d = y - mu
        var = head_sum(d * d) * inv_n
        yn = d * lax.rsqrt(var + RW_GN_EPS) * g_ref[:, cs] + bias_ref[:, cs]
        v_all = v_ref[:, cs]
        bonus = head_sum(r_ref[:, cs] * kh_ref[:, cs] * rk_ref[:, cs])
        out_ref[:, cs] = (_silu(z_ref[:, cs]) * (yn + bonus * v_all)).astype(BF16)


def _rwkv_kernel(*refs, has_vres, tt, chunk):
    n_mix = 19 if has_vres else 15
    mix_in = refs[:n_mix]
    z_ref, rk_ref, g_ref, bias_ref, e_ref = refs[n_mix:n_mix + 5]
    n_out = 1 if has_vres else 2
    outs = refs[n_mix + 5:n_mix + 5 + n_out]
    (c_wa, c_r, c_k, c_v, r_s, ld_s, kh_s, v_s, kn_s, b_s,
     m_ref, ac_ref, gc_ref, qs_ref, y_ref) = refs[n_mix + 5 + n_out:]
    _rwkv_pre_kernel(*mix_in, r_s, ld_s, kh_s, v_s, kn_s, b_s, c_wa, c_r, c_k, c_v,
                     has_vres=has_vres, tm=tt)
    if not has_vres:
        outs[1][...] = v_s[...]
    _rwkv_scan_kernel(r_s, ld_s, kh_s, v_s, kn_s, b_s, z_ref, rk_ref, g_ref, bias_ref, e_ref,
                      outs[0], m_ref, ac_ref, gc_ref, qs_ref, y_ref, tt=tt, chunk=chunk)


def _rwkv(proj_a, proj_b, v_first, mu_wa, mu_r, mu_k, mu_v, w0, w2p, a0, a2p, k_k, k_a, e_head,
          v0, v1p, v2p, r_k3, ln_g3, ln_b3, e_pair, layer, batch, seq):
    m = proj_a.shape[0]
    tt = min(RW_TT, seq)
    nt = seq // tt
    chunk = min(RW_L, tt)
    has_vres = layer > 0
    w = RW_WIDTH
    npair = w // LANES

    def colblk(off, width):
        return pl.BlockSpec((tt, width), lambda b, t: (b * nt + t, off // width))

    def lay(shape):
        return pl.BlockSpec((None,) + shape, lambda b, t: (layer,) + (0,) * len(shape))

    def lay1(shape):
        return pl.BlockSpec((None,) + shape, lambda b, t: (layer - 1,) + (0,) * len(shape))

    row = pl.BlockSpec((tt, w), lambda b, t: (b * nt + t, 0))
    in_specs = [colblk(OFF_WA, LANES), colblk(OFF_RWR, w), colblk(OFF_RWK, w), colblk(OFF_RWV, w),
                lay((1, LANES)), lay((1, w)), lay((1, w)), lay((1, w)),
                lay((1, w)), lay((LANES, w)), lay((1, w)), lay((LANES, w)),
                lay((1, w)), lay((1, w)),
                pl.BlockSpec((w, w), lambda b, t: (0, 0))]
    args = [proj_a, proj_b, proj_b, proj_b, mu_wa, mu_r, mu_k, mu_v, w0, w2p, a0, a2p, k_k, k_a, e_head]
    if has_vres:
        in_specs += [row, lay1((1, w)), lay1((w, LANES)), lay1((LANES, w))]
        args += [v_first, v0, v1p, v2p]
    in_specs += [colblk(OFF_RWZ, w), lay((1, w)), lay((1, w)), lay((1, w)),
                 pl.BlockSpec((LANES, LANES), lambda b, t: (0, 0))]
    args += [proj_b, r_k3, ln_g3, ln_b3, e_pair]
    out_specs = [row]
    out_shape = [jax.ShapeDtypeStruct((m, w), BF16)]
    if not has_vres:
        out_specs.append(row)
        out_shape.append(jax.ShapeDtypeStruct((m, w), F32))
    scratch = [pltpu.VMEM((SUBLANES, LANES), F32)] + [pltpu.VMEM((SUBLANES, w), F32)] * 3
    scratch += [pltpu.VMEM((tt, w), F32)] * 6
    scratch += [
        pltpu.VMEM((npair, LANES, LANES), F32),
        pltpu.VMEM((npair * (tt // chunk), LANES, LANES), F32),
        pltpu.VMEM((npair * (tt // chunk), LANES, LANES), F32),
        pltpu.VMEM((tt, w), F32),
        pltpu.VMEM((tt, w), F32),
    ]
    return pl.pallas_call(
        functools.partial(_rwkv_kernel, has_vres=has_vres, tt=tt, chunk=chunk),
        grid=(batch, nt),
        in_specs=in_specs,
        out_specs=out_specs,
        out_shape=out_shape,
        scratch_shapes=scratch,
        compiler_params=_params("parallel", "arbitrary"),
        name="rwkv",
    )(*args)


_W_IN_PARTS = ((0, 1024), (1024, 512), (1544, 512), (2056, 512), (2568, 512), (3080, 256),
               (3336, 64), (1536, 8), (None, LANES - MLA_ROPE - 2 * ML_HEADS), (5960, 128),
               (4424, 512), (4936, 512), (5448, 512), (6088, 512), (3400, 1024))
TK_PREP = 256


def _permute_w_in_kernel(src_ref, dst_a, dst_b):
    tk = src_ref.shape[1]
    off = 0
    pending = []

    def put(off, width, value):
        if off < D_IN_A:
            dst_a[:, off:off + width] = value
        else:
            dst_b[:, off - D_IN_A:off - D_IN_A + width] = value

    for start, width in _W_IN_PARTS:
        if start is None:
            pending.append(jnp.zeros((width, tk), F32))
        elif width % LANES:
            pending.append(src_ref[start:start + width, :])
        else:
            put(off, width, src_ref[start:start + width, :].T.astype(BF16))
            off += width
            continue
        rows = sum(p.shape[0] for p in pending)
        if rows == LANES:
            put(off, LANES, jnp.concatenate(pending, axis=0).T.astype(BF16))
            off += LANES
            pending = []


def _permute_w_in(w_in):
    depth, d_model, d_in = w_in.shape
    w_t = jnp.swapaxes(w_in, 1, 2)
    return pl.pallas_call(
        _permute_w_in_kernel,
        grid=(depth, d_model // TK_PREP),
        in_specs=[pl.BlockSpec((None, d_in, TK_PREP), lambda l, i: (l, 0, i))],
        out_specs=[pl.BlockSpec((None, TK_PREP, D_IN_A), lambda l, i: (l, i, 0)),
                   pl.BlockSpec((None, TK_PREP, D_IN_B), lambda l, i: (l, i, 0))],
        out_shape=[jax.ShapeDtypeStruct((depth, d_model, D_IN_A), BF16),
                   jax.ShapeDtypeStruct((depth, d_model, D_IN_B), BF16)],
        compiler_params=_params("parallel", "parallel"),
        name="permute_w_in",
    )(w_t)


def _permute_w_uq(w_uq):
    d = w_uq.shape[0]
    w = w_uq.reshape(d, MLA_Q_RANK, MLA_HEADS, MLA_NOPE + MLA_ROPE)
    w = jnp.pad(w, ((0, 0), (0, 0), (0, 0), (0, MLA_QK_PAD - MLA_NOPE - MLA_ROPE)))
    return w.reshape(d, MLA_Q_RANK, MLA_HEADS * MLA_QK_PAD).astype(BF16)


def _split_w_ukv(w_ukv):
    d = w_ukv.shape[0]
    w = w_ukv.reshape(d, MLA_KV_RANK, MLA_HEADS, MLA_NOPE + MLA_V)
    wk = w[..., :MLA_NOPE].reshape(d, MLA_KV_RANK, MLA_HEADS * MLA_NOPE)
    wv = w[..., MLA_NOPE:].reshape(d, MLA_KV_RANK, MLA_HEADS * MLA_V)
    return wk.astype(BF16), wv.astype(BF16)


def _rope_tables(positions):
    inv_freq = jnp.power(ROPE_BASE, -jnp.arange(0, MLA_ROPE, 2, dtype=F32) / MLA_ROPE)
    ang = positions.astype(F32).reshape(-1, 1) * inv_freq
    cos, sin = jnp.cos(ang), jnp.sin(ang)
    z32 = jnp.zeros_like(cos)
    z64 = jnp.zeros((ang.shape[0], LANES - MLA_ROPE), F32)
    cosf = jnp.concatenate([cos, cos, z64], axis=-1)
    s1 = jnp.concatenate([-sin, z32, z64], axis=-1)
    s2 = jnp.concatenate([z32, sin, z64], axis=-1)
    return cosf, s1, s2


def kernel(x, positions, norm_g, w_in, ml_conv_w, ml_conv_b, ml_i_bias, ml_f_bias, ml_norm_g,
           mla_q_norm_g, mla_w_uq, mla_kv_norm_g, mla_w_ukv, rw_mu, rw_w0, rw_w2, rw_a0, rw_a2,
           rw_v0, rw_v1, rw_v2, rw_k_k, rw_k_a, rw_r_k, rw_ln_g, rw_ln_b, w_out, final_norm_g):
    batch, seq, _ = x.shape
    depth = w_in.shape[0]
    m = batch * seq
    x2 = x.reshape(m, D_MODEL)

    w_in_a, w_in_b = _permute_w_in(w_in)
    w_uq_p = _permute_w_uq(mla_w_uq)
    w_uk_p, w_uv_p = _split_w_ukv(mla_w_ukv)
    w_out_b = w_out.astype(BF16)
    cosf, s1, s2 = _rope_tables(positions)

    def row3(a):
        return a.reshape(a.shape[0], 1, a.shape[1])

    norm_g3 = row3(norm_g)
    conv_b3 = row3(ml_conv_b)
    ml_norm_g3 = row3(ml_norm_g)
    qg3, kvg3 = row3(mla_q_norm_g), row3(mla_kv_norm_g)
    w = RW_WIDTH
    mu_r, mu_k, mu_v = row3(rw_mu[:, 0:w]), row3(rw_mu[:, w:2 * w]), row3(rw_mu[:, 2 * w:3 * w])
    mu_wa = row3(rw_mu[:, 3 * w:])
    w0, a0, k_k, k_a = row3(rw_w0), row3(rw_a0), row3(rw_k_k), row3(rw_k_a)
    w2p = jnp.pad(rw_w2, ((0, 0), (0, LANES - RW_DECAY_RANK), (0, 0)))
    a2p = jnp.pad(rw_a2, ((0, 0), (LANES - RW_AAA_RANK, 0), (0, 0)))
    v0 = row3(rw_v0)
    v1p = jnp.pad(rw_v1, ((0, 0), (0, 0), (0, LANES - RW_MV_RANK)))
    v2p = jnp.pad(rw_v2, ((0, 0), (0, LANES - RW_MV_RANK), (0, 0)))
    r_k3 = rw_r_k.reshape(depth, 1, w)
    ln_g3, ln_b3 = row3(rw_ln_g), row3(rw_ln_b)
    hid = jnp.arange(w) // RW_HEAD_DIM
    e_head = (hid[:, None] == hid[None, :]).astype(F32)
    e_pair = e_head[:LANES, :LANES].astype(BF16)
    final_g3 = final_norm_g.reshape(1, 1, D_MODEL)

    v_first = None
    h = _rmsnorm(x2, norm_g3, 0)
    for layer in range(depth):
        proj_a = _inproj(h, w_in_a, layer, TN_PROJ_A)
        proj_b = _inproj(h, w_in_b, layer, TN_PROJ_B)
        gates_t = proj_a[:, OFF_KRG + MLA_ROPE:OFF_KRG + MLA_ROPE + 2 * ML_HEADS]
        gates_t = gates_t.reshape(batch, seq, 2 * ML_HEADS).transpose(0, 2, 1)
        gates_t = gates_t.reshape(batch, 2 * ML_HEADS, 1, seq)
        y_ml = _mlstm(proj_a, gates_t, ml_i_bias, ml_f_bias, ml_conv_w, conv_b3, ml_norm_g3,
                      layer, batch, seq)
        q, k, v = _mla_proj(proj_a, cosf, s1, s2, qg3, w_uq_p, kvg3, w_uk_p, w_uv_p, layer)
        y_mla = _attention(q, k, v, proj_b, batch, seq)
        rw_out = _rwkv(proj_a, proj_b, v_first, mu_wa, mu_r, mu_k, mu_v, w0, w2p, a0, a2p, k_k, k_a, e_head,
                       v0, v1p, v2p, r_k3, ln_g3, ln_b3, e_pair, layer, batch, seq)
        y_rw = rw_out[0]
        if layer == 0:
            v_first = rw_out[1]
        if layer == depth - 1:
            (x2,) = _outproj(x2, y_ml, y_mla, y_rw, w_out_b, final_g3, layer, True)
        else:
            x2, h = _outproj(x2, y_ml, y_mla, y_rw, w_out_b, norm_g3, layer, False)
    return x2.reshape(batch, seq, D_MODEL)
```

```python
import functools

import jax
import jax.numpy as jnp
from jax import lax
from jax.experimental import pallas as pl
from jax.experimental.pallas import tpu as pltpu

F32 = jnp.float32
BF16 = jnp.bfloat16

D_MODEL = 2048
NORM_EPS = 1e-6
ML_HEADS = 4
ML_HEAD_DIM = 128
ML_WIDTH = 512
ML_CONV = 4
MLA_HEADS = 8
MLA_NOPE = 128
MLA_ROPE = 64
MLA_V = 128
MLA_WIDTH = 1024
MLA_Q_RANK = 512
MLA_KV_RANK = 256
ROPE_BASE = 10000.0
RW_HEAD_DIM = 64
RW_WIDTH = 512
RW_HEADS = 8
RW_DECAY_RANK = 64
RW_AAA_RANK = 64
RW_MV_RANK = 32
RW_GN_EPS = 64e-5

LANES = 128
SUBLANES = 8
MLA_QK_PAD = 256
VMEM_LIMIT = 48 * 1024 * 1024
NEG_BIG = -1e30
LOG2E = 1.4426950408889634

OFF_QK = 0
OFF_MLV = 1024
OFF_MLO = 1536
OFF_MLZ = 2048
OFF_CQ = 2560
OFF_CKV = 3072
OFF_KRG = 3328
OFF_WA = 3456
D_IN_A = 3584
OFF_RWR = 0
OFF_RWK = 512
OFF_RWV = 1024
OFF_RWZ = 1536
OFF_MLAZ = 2048
D_IN_B = 3072

TM_PROJ = 1024
TN_PROJ_A = 1792
TN_PROJ_B = 1536
TM_OUT = 512
TQ = 512
ML_TT = 1024
ML_L = 256
RW_TT = 512
RW_L = 64


def _split_bf16(x):
    hi = x.astype(BF16)
    lo = (x - hi.astype(F32)).astype(BF16)
    return hi, lo


def _mm(a, b, passes=1):
    if passes == 1:
        return jnp.dot(a.astype(BF16), b.astype(BF16), preferred_element_type=F32)
    ah, al = _split_bf16(a)
    bh, bl = _split_bf16(b)
    return (jnp.dot(ah, bh, preferred_element_type=F32)
            + jnp.dot(ah, bl, preferred_element_type=F32)
            + jnp.dot(al, bh, preferred_element_type=F32))


_NT = (((1,), (1,)), ((), ()))


def _mm_nt(a, b, passes=1):
    if passes == 1:
        return lax.dot_general(a.astype(BF16), b.astype(BF16), _NT, preferred_element_type=F32)
    ah, al = _split_bf16(a)
    bh, bl = _split_bf16(b)
    return (lax.dot_general(ah, bh, _NT, preferred_element_type=F32)
            + lax.dot_general(ah, bl, _NT, preferred_element_type=F32)
            + lax.dot_general(al, bh, _NT, preferred_element_type=F32))


def _sigmoid(x):
    return 0.5 * jnp.tanh(0.5 * x) + 0.5


def _silu(x):
    h = 0.5 * x
    return h * jnp.tanh(h) + h


def _log_sigmoid(x):
    return jnp.minimum(x, 0.0) - jnp.log(1.0 + jnp.exp(-jnp.abs(x)))


def _params(*sem):
    return pltpu.CompilerParams(dimension_semantics=sem, vmem_limit_bytes=VMEM_LIMIT)


def _rms_scale(x, g):
    return x * lax.rsqrt(jnp.mean(x * x, axis=-1, keepdims=True) + NORM_EPS) * g


def _rmsnorm_kernel(x_ref, g_ref, h_ref):
    h_ref[...] = _rms_scale(x_ref[...], g_ref[...]).astype(BF16)


def _rmsnorm(x2, norm_g3, layer):
    m = x2.shape[0]
    return pl.pallas_call(
        _rmsnorm_kernel,
        grid=(m // TM_OUT,),
        in_specs=[pl.BlockSpec((TM_OUT, D_MODEL), lambda i: (i, 0)),
                  pl.BlockSpec((None, 1, D_MODEL), lambda i: (layer, 0, 0))],
        out_specs=pl.BlockSpec((TM_OUT, D_MODEL), lambda i: (i, 0)),
        out_shape=jax.ShapeDtypeStruct((m, D_MODEL), BF16),
        compiler_params=_params("parallel"),
        name="rmsnorm",
    )(x2, norm_g3)


def _inproj_kernel(h_ref, w_ref, o_ref):
    o_ref[...] = jnp.dot(h_ref[...], w_ref[...], preferred_element_type=F32)


def _inproj(h, w_in_p, layer, tn):
    m = h.shape[0]
    d_out = w_in_p.shape[2]
    return pl.pallas_call(
        _inproj_kernel,
        grid=(m // TM_PROJ, d_out // tn),
        in_specs=[
            pl.BlockSpec((TM_PROJ, D_MODEL), lambda i, j: (i, 0)),
            pl.BlockSpec((None, D_MODEL, tn), lambda i, j: (layer, 0, j)),
        ],
        out_specs=pl.BlockSpec((TM_PROJ, tn), lambda i, j: (i, j)),
        out_shape=jax.ShapeDtypeStruct((m, d_out), F32),
        compiler_params=_params("parallel", "arbitrary"),
        name="inproj",
    )(h, w_in_p)


def _outproj_kernel(x_ref, yml_ref, ymla_ref, yrw_ref, w_ref, g_ref, *out_refs, final):
    acc = x_ref[...]
    acc = acc + jnp.dot(yml_ref[...], w_ref[0:ML_WIDTH, :], preferred_element_type=F32)
    acc = acc + jnp.dot(ymla_ref[...], w_ref[ML_WIDTH:ML_WIDTH + MLA_WIDTH, :],
                        preferred_element_type=F32)
    acc = acc + jnp.dot(yrw_ref[...], w_ref[ML_WIDTH + MLA_WIDTH:, :], preferred_element_type=F32)
    normed = _rms_scale(acc, g_ref[...])
    if final:
        out_refs[0][...] = normed
    else:
        out_refs[0][...] = acc
        out_refs[1][...] = normed.astype(BF16)


def _outproj(x2, y_ml, y_mla, y_rw, w_out_b, gains3, layer, final):
    m = x2.shape[0]
    g_index = 0 if final else layer + 1
    row = pl.BlockSpec((TM_OUT, D_MODEL), lambda i: (i, 0))
    out_specs = [row] if final else [row, row]
    out_shape = [jax.ShapeDtypeStruct((m, D_MODEL), F32)]
    if not final:
        out_shape.append(jax.ShapeDtypeStruct((m, D_MODEL), BF16))
    return pl.pallas_call(
        functools.partial(_outproj_kernel, final=final),
        grid=(m // TM_OUT,),
        in_specs=[
            row,
            pl.BlockSpec((TM_OUT, ML_WIDTH), lambda i: (i, 0)),
            pl.BlockSpec((TM_OUT, MLA_WIDTH), lambda i: (i, 0)),
            pl.BlockSpec((TM_OUT, RW_WIDTH), lambda i: (i, 0)),
            pl.BlockSpec((None, D_MODEL, D_MODEL), lambda i: (layer, 0, 0)),
            pl.BlockSpec((None, 1, D_MODEL), lambda i: (g_index, 0, 0)),
        ],
        out_specs=out_specs,
        out_shape=out_shape,
        compiler_params=_params("parallel"),
        name="outproj",
    )(x2, y_ml, y_mla, y_rw, w_out_b, gains3)


def _mla_proj_kernel(cq_ref, ckv_ref, krg_ref, cos_ref, s1_ref, s2_ref, qg_ref, wq_ref, kvg_ref,
                     wk_ref, wv_ref, q_out, k_out, v_out):
    cq = cq_ref[...]
    cqn = (cq * lax.rsqrt(jnp.mean(cq * cq, axis=-1, keepdims=True) + NORM_EPS)
           * qg_ref[...]).astype(BF16)
    q = jnp.dot(cqn, wq_ref[...], preferred_element_type=F32)
    ckv = ckv_ref[...]
    ckvn = (ckv * lax.rsqrt(jnp.mean(ckv * ckv, axis=-1, keepdims=True) + NORM_EPS)
            * kvg_ref[...]).astype(BF16)
    kn = jnp.dot(ckvn, wk_ref[...], preferred_element_type=F32)
    v = jnp.dot(ckvn, wv_ref[...], preferred_element_type=F32)
    cosf, s1, s2 = cos_ref[...], s1_ref[...], s2_ref[...]

    def rope(xb):
        return (xb * cosf + pltpu.roll(xb, LANES - MLA_ROPE // 2, 1) * s1
                + pltpu.roll(xb, MLA_ROPE // 2, 1) * s2)

    scale = (MLA_NOPE + MLA_ROPE) ** -0.5 * LOG2E
    kr = rope(krg_ref[...]).astype(BF16)
    for h in range(MLA_HEADS):
        c0 = h * MLA_QK_PAD
        q_out[:, c0:c0 + MLA_NOPE] = (q[:, c0:c0 + MLA_NOPE] * scale).astype(BF16)
        q_out[:, c0 + MLA_NOPE:c0 + MLA_QK_PAD] = (
            rope(q[:, c0 + MLA_NOPE:c0 + MLA_QK_PAD]) * scale).astype(BF16)
        k_out[:, c0:c0 + MLA_NOPE] = kn[:, h * MLA_NOPE:(h + 1) * MLA_NOPE].astype(BF16)
        k_out[:, c0 + MLA_NOPE:c0 + MLA_QK_PAD] = kr
    v_out[...] = v.T.astype(BF16)


def _mla_proj(proj, cosf, s1, s2, qg3, wq_p, kvg3, wk_p, wv_p, layer):
    m = proj.shape[0]
    tm = TQ
    row = lambda i: (i, 0)
    return pl.pallas_call(
        _mla_proj_kernel,
        grid=(m // tm,),
        in_specs=[
            pl.BlockSpec((tm, MLA_Q_RANK), lambda i: (i, OFF_CQ // MLA_Q_RANK)),
            pl.BlockSpec((tm, MLA_KV_RANK), lambda i: (i, OFF_CKV // MLA_KV_RANK)),
            pl.BlockSpec((tm, LANES), lambda i: (i, OFF_KRG // LANES)),
            pl.BlockSpec((tm, LANES), row),
            pl.BlockSpec((tm, LANES), row),
            pl.BlockSpec((tm, LANES), row),
            pl.BlockSpec((None, 1, MLA_Q_RANK), lambda i: (layer, 0, 0)),
            pl.BlockSpec((None, MLA_Q_RANK, MLA_HEADS * MLA_QK_PAD), lambda i: (layer, 0, 0)),
            pl.BlockSpec((None, 1, MLA_KV_RANK), lambda i: (layer, 0, 0)),
            pl.BlockSpec((None, MLA_KV_RANK, MLA_HEADS * MLA_NOPE), lambda i: (layer, 0, 0)),
            pl.BlockSpec((None, MLA_KV_RANK, MLA_HEADS * MLA_V), lambda i: (layer, 0, 0)),
        ],
        out_specs=[
            pl.BlockSpec((tm, MLA_HEADS * MLA_QK_PAD), row),
            pl.BlockSpec((tm, MLA_HEADS * MLA_QK_PAD), row),
            pl.BlockSpec((None, MLA_HEADS * MLA_V, tm), lambda i: (i, 0, 0)),
        ],
        out_shape=[
            jax.ShapeDtypeStruct((m, MLA_HEADS * MLA_QK_PAD), BF16),
            jax.ShapeDtypeStruct((m, MLA_HEADS * MLA_QK_PAD), BF16),
            jax.ShapeDtypeStruct((m // tm, MLA_HEADS * MLA_V, tm), BF16),
        ],
        compiler_params=_params("parallel"),
        name="mla_proj",
    )(proj, proj, proj, cosf, s1, s2, qg3, wq_p, kvg3, wk_p, wv_p)


ATTN_HEADS_PER_STEP = 4


def _attn_kernel(q_ref, k_ref, vt_ref, z_ref, o_ref, *, tq):
    i = pl.program_id(2)
    nh = ATTN_HEADS_PER_STEP
    th = tq // 2
    chains = [(h, u) for h in range(nh) for u in range(2)]

    def head_cols(h):
        return slice(h * MLA_QK_PAD, (h + 1) * MLA_QK_PAD)

    def softmax_pv(carry_c, s, vt):
        m, l, acc = carry_c
        m_new = jnp.maximum(m, jnp.max(s, axis=0, keepdims=True))
        alpha = jnp.exp2(m - m_new)
        p = jnp.exp2(s - m_new)
        l = alpha * l + jnp.sum(p, axis=0, keepdims=True)
        acc = alpha * acc + jnp.dot(vt, p.astype(BF16), preferred_element_type=F32)
        return m_new, l, acc

    def step(j, carry):
        off = pl.multiple_of(j * tq, tq)
        ss = []
        for h in range(nh):
            s = lax.dot_general(k_ref[pl.ds(off, tq), head_cols(h)], q_ref[:, head_cols(h)], _NT,
                                preferred_element_type=F32)
            ss += [s[:, :th], s[:, th:]]
        return tuple(softmax_pv(carry[n], ss[n], vt_ref[j, chains[n][0]])
                     for n in range(len(chains)))

    def diagonal_step(carry):
        off = pl.multiple_of(i * tq, tq)
        ss, vts = [], []
        for h, u in chains:
            nk = th if u == 0 else tq
            s = lax.dot_general(k_ref[pl.ds(off, nk), head_cols(h)],
                                q_ref[u * th:(u + 1) * th, head_cols(h)], _NT,
                                preferred_element_type=F32)
            kidx = lax.broadcasted_iota(jnp.int32, (nk, th), 0)
            qidx = lax.broadcasted_iota(jnp.int32, (nk, th), 1) + u * th
            ss.append(jnp.where(kidx <= qidx, s, NEG_BIG))
            vts.append(vt_ref[i, h][:, :nk])
        return tuple(softmax_pv(carry[n], ss[n], vts[n]) for n in range(len(chains)))

    init = tuple((jnp.full((1, th), NEG_BIG, F32), jnp.zeros((1, th), F32),
                  jnp.zeros((MLA_V, th), F32)) for _ in chains)
    carry = lax.fori_loop(0, i, step, init)
    carry = diagonal_step(carry)
    for n, (h, u) in enumerate(chains):
        _, l, acc = carry[n]
        y = (acc / l).T
        rs = slice(u * th, (u + 1) * th)
        cs = slice(h * MLA_V, (h + 1) * MLA_V)
        o_ref[rs, cs] = (_silu(z_ref[rs, cs]) * y).astype(BF16)


def _attention(q, k, v_t, proj, batch, seq):
    m = q.shape[0]
    nq = seq // TQ
    nh = ATTN_HEADS_PER_STEP
    v_t5 = v_t.reshape(batch, nq, MLA_HEADS, MLA_V, TQ)
    return pl.pallas_call(
        functools.partial(_attn_kernel, tq=TQ),
        grid=(batch, MLA_HEADS // nh, nq),
        in_specs=[
            pl.BlockSpec((TQ, nh * MLA_QK_PAD), lambda b, h, i: (b * nq + i, h)),
            pl.BlockSpec((seq, nh * MLA_QK_PAD), lambda b, h, i: (b, h)),
            pl.BlockSpec((None, nq, nh, MLA_V, TQ), lambda b, h, i: (b, 0, h, 0, 0)),
            pl.BlockSpec((TQ, nh * MLA_V),
                         lambda b, h, i: (b * nq + i, OFF_MLAZ // (nh * MLA_V) + h)),
        ],
        out_specs=pl.BlockSpec((TQ, nh * MLA_V), lambda b, h, i: (b * nq + i, h)),
        out_shape=jax.ShapeDtypeStruct((m, MLA_WIDTH), BF16),
        compiler_params=_params("parallel", "parallel", "arbitrary"),
        name="mla_attn",
    )(q, k, v_t5, proj)


def _shift_rows(x, prev8, s):
    rolled = pltpu.roll(x, s, 0)
    prev_rolled = pltpu.roll(prev8, s, 0)
    rid = lax.broadcasted_iota(jnp.int32, (SUBLANES, x.shape[1]), 0)
    top = jnp.where(rid < s, prev_rolled, rolled[0:SUBLANES])
    return jnp.concatenate([top, rolled[SUBLANES:]], axis=0)


def _mlstm_kernel(ib_ref, fb_ref, q_ref, k_ref, v_ref, o_ref, z_ref, g_ref, w_ref, b_ref, ng_ref,
                  out_ref, ct_ref, n_ref, m_ref, qp_ref, kp_ref, *, layer, tt, chunk):
    @pl.when(pl.program_id(1) == 0)
    def _():
        for ref in (ct_ref, n_ref, m_ref, qp_ref, kp_ref):
            ref[...] = jnp.zeros_like(ref)

    hd = ML_HEAD_DIM
    w_all = w_ref[...]
    b_all = b_ref[...]

    def conv_silu(x_ref, prev_ref, w, b):
        x = x_ref[...]
        prev8 = prev_ref[...]
        y = b + w[ML_CONV - 1:ML_CONV] * x
        for s in range(1, ML_CONV):
            y = y + w[ML_CONV - 1 - s:ML_CONV - s] * _shift_rows(x, prev8, s)
        prev_ref[...] = x[tt - SUBLANES:tt]
        return _silu(y)

    q_all = conv_silu(q_ref, qp_ref, w_all[:, :ML_WIDTH], b_all[:, :ML_WIDTH]) * (hd ** -0.5)
    k_all = conv_silu(k_ref, kp_ref, w_all[:, ML_WIDTH:], b_all[:, ML_WIDTH:])
    li_rows = [g_ref[h] + ib_ref[layer, h] for h in range(ML_HEADS)]
    lf_rows = [_log_sigmoid(g_ref[ML_HEADS + h] + fb_ref[layer, h]) for h in range(ML_HEADS)]

    L = chunk
    rows = lax.broadcasted_iota(jnp.int32, (L, L), 0)
    cols = lax.broadcasted_iota(jnp.int32, (L, L), 1)
    tri = rows >= cols
    eye = rows == cols
    nchunk = tt // L
    upper = jnp.where(rows <= cols, 1.0, 0.0).astype(BF16)
    pad_rows = jnp.zeros((SUBLANES - ML_HEADS, L), F32)
    insts = []
    for c in range(nchunk):
        sl = slice(c * L, (c + 1) * L)
        lf8 = jnp.concatenate([lf_rows[h][:, sl] for h in range(ML_HEADS)] + [pad_rows], axis=0)
        b_rows = jnp.zeros((SUBLANES, L), F32)
        rest = lf8
        for _ in range(3):
            piece = rest.astype(BF16)
            b_rows = b_rows + jnp.dot(piece, upper, preferred_element_type=F32)
            rest = rest - piece.astype(F32)
        for h in range(ML_HEADS):
            cs = slice(h * hd, (h + 1) * hd)
            li_row = li_rows[h][:, sl]
            b_row = b_rows[h:h + 1, :]
            b_col = jnp.sum(jnp.where(eye, jnp.broadcast_to(b_row, (L, L)), 0.0),
                            axis=-1, keepdims=True)
            li_col = jnp.sum(jnp.where(eye, jnp.broadcast_to(li_row, (L, L)), 0.0),
                             axis=-1, keepdims=True)
            dmat = jnp.where(tri, b_col - b_row + li_row, NEG_BIG)
            m_loc = jnp.max(dmat, axis=-1, keepdims=True)
            b_last = b_col[L - 1:L, :]
            g_col = b_last - b_col + li_col
            g_max = jnp.max(g_col, axis=0, keepdims=True)
            kc = k_all[sl, cs]
            insts.append(dict(c=c, h=h, sl=sl, cs=cs, qc=q_all[sl, cs].astype(BF16),
                              kc=kc.astype(BF16), vc=v_ref[sl, cs].astype(BF16),
                              pm=jnp.exp(dmat - m_loc), m_loc=m_loc, b_col=b_col, b_last=b_last,
                              g_max=g_max, kwg=kc * jnp.exp(g_col - g_max)))
    for s in insts:
        s["smat"] = lax.dot_general(s["qc"], s["kc"], _NT, preferred_element_type=F32) * s["pm"]
    for s in insts:
        s["sv"] = jnp.dot(s["smat"].astype(BF16), s["vc"], preferred_element_type=F32)
        s["ssum"] = jnp.sum(s["smat"], axis=-1, keepdims=True)
        s["kv"] = jnp.dot(s["kwg"].T.astype(BF16), s["vc"], preferred_element_type=F32)
        s["ksum"] = jnp.sum(s["kwg"], axis=0, keepdims=True)

    for c in range(nchunk):
        for s in insts[c * ML_HEADS:(c + 1) * ML_HEADS]:
            h = s["h"]
            ct = ct_ref[h]
            n_row = n_ref[h]
            m_prev = m_ref[h][:, 0:1]
            m_inter = s["b_col"] + m_prev
            m_t = jnp.maximum(m_inter, s["m_loc"])
            inter = jnp.exp(m_inter - m_t)
            loc = jnp.exp(s["m_loc"] - m_t)
            num = (inter * jnp.dot(s["qc"], ct.astype(BF16), preferred_element_type=F32)
                   + loc * s["sv"])
            den = (inter * jnp.sum(s["qc"].astype(F32) * n_row, axis=-1, keepdims=True)
                   + loc * s["ssum"])
            s["hh"] = num / jnp.maximum(jnp.abs(den), jnp.exp(-m_t))
            m_new = jnp.maximum(s["b_last"] + m_prev, s["g_max"])
            decay = jnp.exp(s["b_last"] + m_prev - m_new)
            scale = jnp.exp(s["g_max"] - m_new)
            ct_ref[h] = decay * ct + scale * s["kv"]
            n_ref[h] = decay * n_row + scale * s["ksum"]
            m_ref[h] = jnp.broadcast_to(m_new, (1, hd))
    for s in insts:
        sl, cs = s["sl"], s["cs"]
        hh = s["hh"]
        mu = jnp.mean(hh, axis=-1, keepdims=True)
        dd = hh - mu
        var = jnp.mean(dd * dd, axis=-1, keepdims=True)
        yn = dd * lax.rsqrt(var + NORM_EPS) * ng_ref[:, cs]
        out_ref[sl, cs] = (_silu(z_ref[sl, cs]) * (_sigmoid(o_ref[sl, cs]) * yn)).astype(BF16)


def _mlstm(proj, gates_t, i_bias, f_bias, conv_w, conv_b3, norm_g3, layer, batch, seq):
    m = proj.shape[0]
    tt = min(ML_TT, seq)
    nt = seq // tt
    hd = ML_HEAD_DIM
    w = ML_WIDTH

    def col(off):
        return pl.BlockSpec((tt, w), lambda b, t: (b * nt + t, off // w))

    smem = pl.BlockSpec(memory_space=pltpu.SMEM)
    return pl.pallas_call(
        functools.partial(_mlstm_kernel, layer=layer, tt=tt, chunk=min(ML_L, tt)),
        grid=(batch, nt),
        in_specs=[
            smem, smem,
            col(OFF_QK), col(OFF_QK + w), col(OFF_MLV), col(OFF_MLO), col(OFF_MLZ),
            pl.BlockSpec((None, 2 * ML_HEADS, 1, tt), lambda b, t: (b, 0, 0, t)),
            pl.BlockSpec((None, ML_CONV, 2 * w), lambda b, t: (layer, 0, 0)),
            pl.BlockSpec((None, 1, 2 * w), lambda b, t: (layer, 0, 0)),
            pl.BlockSpec((None, 1, w), lambda b, t: (layer, 0, 0)),
        ],
        out_specs=pl.BlockSpec((tt, w), lambda b, t: (b * nt + t, 0)),
        out_shape=jax.ShapeDtypeStruct((m, w), BF16),
        scratch_shapes=[
            pltpu.VMEM((ML_HEADS, hd, hd), F32),
            pltpu.VMEM((ML_HEADS, 1, hd), F32),
            pltpu.VMEM((ML_HEADS, 1, hd), F32),
            pltpu.VMEM((SUBLANES, w), F32),
            pltpu.VMEM((SUBLANES, w), F32),
        ],
        compiler_params=_params("parallel", "arbitrary"),
        name="mlstm",
    )(i_bias, f_bias, proj, proj, proj, proj, proj, gates_t, conv_w, conv_b3, norm_g3)


RW_PRE_PASSES = 3


def _rwkv_pre_kernel(*refs, has_vres, tm):
    if has_vres:
        (wa_ref, r_ref, k_ref, v_ref, mu_wa, mu_r, mu_k, mu_v, w0_ref, w2_ref, a0_ref, a2_ref,
         kk_ref, ka_ref, e_ref, vf_ref, v0_ref, v1_ref, v2_ref,
         r_out, ld_out, kh_out, v_out, kn_out, b_out, c_wa, c_r, c_k, c_v) = refs
    else:
        (wa_ref, r_ref, k_ref, v_ref, mu_wa, mu_r, mu_k, mu_v, w0_ref, w2_ref, a0_ref, a2_ref,
         kk_ref, ka_ref, e_ref,
         r_out, ld_out, kh_out, v_out, kn_out, b_out, c_wa, c_r, c_k, c_v) = refs

    @pl.when(pl.program_id(1) == 0)
    def _():
        for c in (c_wa, c_r, c_k, c_v):
            c[...] = jnp.zeros_like(c)

    def mix(x_ref, c_ref, mu_ref):
        x = x_ref[...]
        rolled = pltpu.roll(x, 1, 0)
        rid = lax.broadcasted_iota(jnp.int32, (SUBLANES, x.shape[1]), 0)
        prev_last = jnp.broadcast_to(c_ref[SUBLANES - 1:SUBLANES, :], (SUBLANES, x.shape[1]))
        top = jnp.where(rid == 0, prev_last, rolled[0:SUBLANES])
        xprev = jnp.concatenate([top, rolled[SUBLANES:]], axis=0)
        c_ref[...] = x[tm - SUBLANES:tm]
        return x + mu_ref[...] * (xprev - x)

    xwa = mix(wa_ref, c_wa, mu_wa)
    r = mix(r_ref, c_r, mu_r)
    k = mix(k_ref, c_k, mu_k)
    v = mix(v_ref, c_v, mu_v)
    zw = w0_ref[...] + _mm(jnp.tanh(xwa), w2_ref[...], RW_PRE_PASSES)
    za = a0_ref[...] + _mm(xwa, a2_ref[...], RW_PRE_PASSES)
    log_w = _log_sigmoid(zw) - 0.5
    ld_out[...] = -jnp.exp(log_w)
    a = _sigmoid(za)
    if has_vres:
        gate = _sigmoid(v0_ref[...] + _mm(_mm(v, v1_ref[...], RW_PRE_PASSES), v2_ref[...],
                                          RW_PRE_PASSES))
        v = v + (vf_ref[...] - v) * gate
    kk = k * kk_ref[...]
    ss = _mm(kk * kk, e_ref[...])
    kn = kk / jnp.maximum(jnp.sqrt(ss), 1e-12)
    r_out[...] = r
    kh_out[...] = k * (1.0 + (a - 1.0) * ka_ref[...])
    v_out[...] = v
    kn_out[...] = kn
    b_out[...] = kn * a


RW_GROUP = 4


def _rwkv_scan_kernel(r_ref, ld_ref, kh_ref, v_ref, kn_ref, b_ref, z_ref, rk_ref, g_ref, bias_ref,
                      e_ref, out_ref, m_ref, ac_ref, gc_ref, qs_ref, y_ref, *, tt, chunk):
    @pl.when(pl.program_id(1) == 0)
    def _():
        m_ref[...] = jnp.zeros_like(m_ref)

    L = chunk
    L2 = 2 * L
    nchunk = tt // L
    npair = RW_WIDTH // LANES
    rows = lax.broadcasted_iota(jnp.int32, (L, L), 0)
    cols = lax.broadcasted_iota(jnp.int32, (L, L), 1)
    tri_b = jnp.where(rows >= cols, 1.0, 0.0).astype(BF16)
    lane = lax.broadcasted_iota(jnp.int32, (1, LANES), 1)
    m0 = jnp.where(lane < RW_HEAD_DIM, 1.0, 0.0)
    m1 = 1.0 - m0
    r2 = lax.broadcasted_iota(jnp.int32, (L2, L2), 0)
    c2 = lax.broadcasted_iota(jnp.int32, (L2, L2), 1)
    same_blk = (r2 < L) == (c2 < L)
    t2 = jnp.where(r2 < L, r2, r2 - L)
    s2 = jnp.where(c2 < L, c2, c2 - L)
    mask_strict = jnp.logical_and(same_blk, t2 > s2)
    mask_incl = jnp.logical_and(same_blk, t2 >= s2)
    eye2 = lax.broadcasted_iota(jnp.int32, (LANES, LANES), 0) == lax.broadcasted_iota(
        jnp.int32, (LANES, LANES), 1)
    n_double = max(1, (L - 1).bit_length())

    def stack(x):
        return jnp.concatenate([x * m0, x * m1], axis=0)

    def prepare(p, c):
        sl = slice(c * L, (c + 1) * L)
        cs = slice(p * LANES, (p + 1) * LANES)
        ld = ld_ref[sl, cs]
        r, kh, v, kn, b = r_ref[sl, cs], kh_ref[sl, cs], v_ref[sl, cs], kn_ref[sl, cs], b_ref[sl, cs]
        ld_hi, ld_lo = _split_bf16(ld)
        logp = (jnp.dot(tri_b, ld_hi, preferred_element_type=F32)
                + jnp.dot(tri_b, ld_lo, preferred_element_type=F32))
        cmid = logp[L // 2 - 1:L // 2, :]
        e = logp - cmid
        e_last = e[L - 1:L, :]
        p_mid = jnp.exp(cmid)
        at_s = stack(-kn * jnp.exp(e - ld))
        rt_s = stack(r * jnp.exp(e))
        inv = jnp.exp(-e)
        tail = jnp.exp(e_last - e)
        lhs = jnp.concatenate([at_s, rt_s], axis=0).astype(BF16)
        rhs = jnp.concatenate([stack(b * inv), stack(kh * inv)], axis=0).astype(BF16)
        hat_t = jnp.concatenate([stack(b * tail), stack(kh * tail)], axis=0).T.astype(BF16)
        return dict(sl=sl, cs=cs, idx=p * nchunk + c, lhs=lhs, rhs=rhs, hat_t=hat_t,
                    at_true=at_s * p_mid, rt_true=rt_s * p_mid, v_s=stack(v).astype(BF16),
                    p_last=jnp.exp(e_last + cmid))

    def coeff_stages(cg):
        insts = [prepare(p, cg * RW_GROUP + g) for p in range(npair) for g in range(RW_GROUP)]
        for s in insts:
            aa = lax.dot_general(s["lhs"], s["rhs"], _NT, preferred_element_type=F32)
            s["nmat"] = jnp.where(mask_strict, aa[0:L2, 0:L2], 0.0).astype(BF16)
            s["a_ak"] = jnp.where(mask_strict, aa[0:L2, L2:], 0.0).astype(BF16)
            s["a_r"] = jnp.concatenate([jnp.where(mask_incl, aa[L2:, 0:L2], 0.0),
                                        jnp.where(mask_incl, aa[L2:, L2:], 0.0)],
                                       axis=1).astype(BF16)
        yield
        for s in insts:
            s["x"] = jnp.concatenate(
                [s["at_true"], jnp.dot(s["a_ak"], s["v_s"], preferred_element_type=F32)],
                axis=1)
        yield
        for it in range(n_double):
            for s in insts:
                s["x"] = s["x"] + jnp.dot(s["nmat"], s["x"].astype(BF16),
                                          preferred_element_type=F32)
            if it + 1 < n_double:
                for s in insts:
                    s["nmat"] = jnp.dot(s["nmat"], s["nmat"],
                                        preferred_element_type=F32).astype(BF16)
            yield
        for s in insts:
            big_l = jnp.concatenate([s["a_r"], s["hat_t"]], axis=0)
            big_r = jnp.concatenate(
                [s["x"].astype(BF16),
                 jnp.concatenate([jnp.zeros((L2, LANES), BF16), s["v_s"]], axis=1)], axis=0)
            res = jnp.dot(big_l, big_r, preferred_element_type=F32)
            q_s = s["rt_true"] + res[0:L2, :LANES]
            y0_s = res[0:L2, LANES:]
            qs_ref[s["sl"], s["cs"]] = q_s[0:L] + q_s[L:]
            y_ref[s["sl"], s["cs"]] = y0_s[0:L] + y0_s[L:]
            ac_ref[s["idx"]] = (jnp.where(eye2, jnp.broadcast_to(s["p_last"], (LANES, LANES)), 0.0)
                                + res[L2:, :LANES])
            gc_ref[s["idx"]] = res[L2:, LANES:]

    def chain_step(c):
        sl = slice(c * L, (c + 1) * L)
        for p in range(npair):
            cs = slice(p * LANES, (p + 1) * LANES)
            idx = p * nchunk + c
            mp = m_ref[p]
            y_ref[sl, cs] = _mm(qs_ref[sl, cs], mp) + y_ref[sl, cs]
            m_ref[p] = _mm(ac_ref[idx], mp) + gc_ref[idx]

    pending = []
    for cg in range(nchunk // RW_GROUP):
        for _ in coeff_stages(cg):
            if pending:
                chain_step(pending.pop(0))
        for c in pending:
            chain_step(c)
        pending = [cg * RW_GROUP + g for g in range(RW_GROUP)]
    for c in pending:
        chain_step(c)

    e_b = e_ref[...]
    inv_n = 1.0 / RW_HEAD_DIM

    def head_sum(x):
        return jnp.dot(x.astype(BF16), e_b, preferred_element_type=F32)

    for p in range(npair):
        cs = slice(p * LANES, (p + 1) * LANES)
        y = y_ref[:, cs]
        mu = head_sum(y) * inv_n
        d = y - mu
        var = head_sum(d * d) * inv_n
        yn = d * lax.rsqrt(var + RW_GN_EPS) * g_ref[:, cs] + bias_ref[:, cs]
        v_all = v_ref[:, cs]
        bonus = head_sum(r_ref[:, cs] * kh_ref[:, cs] * rk_ref[:, cs])
        out_ref[:, cs] = (_silu(z_ref[:, cs]) * (yn + bonus * v_all)).astype(BF16)


def _rwkv_kernel(*refs, has_vres, tt, chunk):
    n_mix = 19 if has_vres else 15
    mix_in = refs[:n_mix]
    z_ref, rk_ref, g_ref, bias_ref, e_ref = refs[n_mix:n_mix + 5]
    n_out = 1 if has_vres else 2
    outs = refs[n_mix + 5:n_mix + 5 + n_out]
    (c_wa, c_r, c_k, c_v, r_s, ld_s, kh_s, v_s, kn_s, b_s,
     m_ref, ac_ref, gc_ref, qs_ref, y_ref) = refs[n_mix + 5 + n_out:]
    _rwkv_pre_kernel(*mix_in, r_s, ld_s, kh_s, v_s, kn_s, b_s, c_wa, c_r, c_k, c_v,
                     has_vres=has_vres, tm=tt)
    if not has_vres:
        outs[1][...] = v_s[...]
    _rwkv_scan_kernel(r_s, ld_s, kh_s, v_s, kn_s, b_s, z_ref, rk_ref, g_ref, bias_ref, e_ref,
                      outs[0], m_ref, ac_ref, gc_ref, qs_ref, y_ref, tt=tt, chunk=chunk)


def _rwkv(proj_a, proj_b, v_first, mu_wa, mu_r, mu_k, mu_v, w0, w2p, a0, a2p, k_k, k_a, e_head,
          v0, v1p, v2p, r_k3, ln_g3, ln_b3, e_pair, layer, batch, seq):
    m = proj_a.shape[0]
    tt = min(RW_TT, seq)
    nt = seq // tt
    chunk = min(RW_L, tt)
    has_vres = layer > 0
    w = RW_WIDTH
    npair = w // LANES

    def colblk(off, width):
        return pl.BlockSpec((tt, width), lambda b, t: (b * nt + t, off // width))

    def lay(shape):
        return pl.BlockSpec((None,) + shape, lambda b, t: (layer,) + (0,) * len(shape))

    def lay1(shape):
        return pl.BlockSpec((None,) + shape, lambda b, t: (layer - 1,) + (0,) * len(shape))

    row = pl.BlockSpec((tt, w), lambda b, t: (b * nt + t, 0))
    in_specs = [colblk(OFF_WA, LANES), colblk(OFF_RWR, w), colblk(OFF_RWK, w), colblk(OFF_RWV, w),
                lay((1, LANES)), lay((1, w)), lay((1, w)), lay((1, w)),
                lay((1, w)), lay((LANES, w)), lay((1, w)), lay((LANES, w)),
                lay((1, w)), lay((1, w)),
                pl.BlockSpec((w, w), lambda b, t: (0, 0))]
    args = [proj_a, proj_b, proj_b, proj_b, mu_wa, mu_r, mu_k, mu_v, w0, w2p, a0, a2p, k_k, k_a, e_head]
    if has_vres:
        in_specs += [row, lay1((1, w)), lay1((w, LANES)), lay1((LANES, w))]
        args += [v_first, v0, v1p, v2p]
    in_specs += [colblk(OFF_RWZ, w), lay((1, w)), lay((1, w)), lay((1, w)),
                 pl.BlockSpec((LANES, LANES), lambda b, t: (0, 0))]
    args += [proj_b, r_k3, ln_g3, ln_b3, e_pair]
    out_specs = [row]
    out_shape = [jax.ShapeDtypeStruct((m, w), BF16)]
    if not has_vres:
        out_specs.append(row)
        out_shape.append(jax.ShapeDtypeStruct((m, w), F32))
    scratch = [pltpu.VMEM((SUBLANES, LANES), F32)] + [pltpu.VMEM((SUBLANES, w), F32)] * 3
    scratch += [pltpu.VMEM((tt, w), F32)] * 6
    scratch += [
        pltpu.VMEM((npair, LANES, LANES), F32),
        pltpu.VMEM((npair * (tt // chunk), LANES, LANES), F32),
        pltpu.VMEM((npair * (tt // chunk), LANES, LANES), F32),
        pltpu.VMEM((tt, w), F32),
        pltpu.VMEM((tt, w), F32),
    ]
    return pl.pallas_call(
        functools.partial(_rwkv_kernel, has_vres=has_vres, tt=tt, chunk=chunk),
        grid=(batch, nt),
        in_specs=in_specs,
        out_specs=out_specs,
        out_shape=out_shape,
        scratch_shapes=scratch,
        compiler_params=_params("parallel", "arbitrary"),
        name="rwkv",
    )(*args)


_W_IN_PARTS = ((0, 1024), (1024, 512), (1544, 512), (2056, 512), (2568, 512), (3080, 256),
               (3336, 64), (1536, 8), (None, LANES - MLA_ROPE - 2 * ML_HEADS), (5960, 128),
               (4424, 512), (4936, 512), (5448, 512), (6088, 512), (3400, 1024))
TK_PREP = 256


def _permute_w_in_kernel(src_ref, dst_a, dst_b):
    tk = src_ref.shape[1]
    off = 0
    pending = []

    def put(off, width, value):
        if off < D_IN_A:
            dst_a[:, off:off + width] = value
        else:
            dst_b[:, off - D_IN_A:off - D_IN_A + width] = value

    for start, width in _W_IN_PARTS:
        if start is None:
            pending.append(jnp.zeros((width, tk), F32))
        elif width % LANES:
            pending.append(src_ref[start:start + width, :])
        else:
            put(off, width, src_ref[start:start + width, :].T.astype(BF16))
            off += width
            continue
        rows = sum(p.shape[0] for p in pending)
        if rows == LANES:
            put(off, LANES, jnp.concatenate(pending, axis=0).T.astype(BF16))
            off += LANES
            pending = []


def _permute_w_in(w_in):
    depth, d_model, d_in = w_in.shape
    w_t = jnp.swapaxes(w_in, 1, 2)
    return pl.pallas_call(
        _permute_w_in_kernel,
        grid=(depth, d_model // TK_PREP),
        in_specs=[pl.BlockSpec((None, d_in, TK_PREP), lambda l, i: (l, 0, i))],
        out_specs=[pl.BlockSpec((None, TK_PREP, D_IN_A), lambda l, i: (l, i, 0)),
                   pl.BlockSpec((None, TK_PREP, D_IN_B), lambda l, i: (l, i, 0))],
        out_shape=[jax.ShapeDtypeStruct((depth, d_model, D_IN_A), BF16),
                   jax.ShapeDtypeStruct((depth, d_model, D_IN_B), BF16)],
        compiler_params=_params("parallel", "parallel"),
        name="permute_w_in",
    )(w_t)


def _permute_w_uq(w_uq):
    d = w_uq.shape[0]
    w = w_uq.reshape(d, MLA_Q_RANK, MLA_HEADS, MLA_NOPE + MLA_ROPE)
    w = jnp.pad(w, ((0, 0), (0, 0), (0, 0), (0, MLA_QK_PAD - MLA_NOPE - MLA_ROPE)))
    return w.reshape(d, MLA_Q_RANK, MLA_HEADS * MLA_QK_PAD).astype(BF16)


def _split_w_ukv(w_ukv):
    d = w_ukv.shape[0]
    w = w_ukv.reshape(d, MLA_KV_RANK, MLA_HEADS, MLA_NOPE + MLA_V)
    wk = w[..., :MLA_NOPE].reshape(d, MLA_KV_RANK, MLA_HEADS * MLA_NOPE)
    wv = w[..., MLA_NOPE:].reshape(d, MLA_KV_RANK, MLA_HEADS * MLA_V)
    return wk.astype(BF16), wv.astype(BF16)


def _rope_tables(positions):
    inv_freq = jnp.power(ROPE_BASE, -jnp.arange(0, MLA_ROPE, 2, dtype=F32) / MLA_ROPE)
    ang = positions.astype(F32).reshape(-1, 1) * inv_freq
    cos, sin = jnp.cos(ang), jnp.sin(ang)
    z32 = jnp.zeros_like(cos)
    z64 = jnp.zeros((ang.shape[0], LANES - MLA_ROPE), F32)
    cosf = jnp.concatenate([cos, cos, z64], axis=-1)
    s1 = jnp.concatenate([-sin, z32, z64], axis=-1)
    s2 = jnp.concatenate([z32, sin, z64], axis=-1)
    return cosf, s1, s2


def kernel(x, positions, norm_g, w_in, ml_conv_w, ml_conv_b, ml_i_bias, ml_f_bias, ml_norm_g,
           mla_q_norm_g, mla_w_uq, mla_kv_norm_g, mla_w_ukv, rw_mu, rw_w0, rw_w2, rw_a0, rw_a2,
           rw_v0, rw_v1, rw_v2, rw_k_k, rw_k_a, rw_r_k, rw_ln_g, rw_ln_b, w_out, final_norm_g):
    batch, seq, _ = x.shape
    depth = w_in.shape[0]
    m = batch * seq
    x2 = x.reshape(m, D_MODEL)

    w_in_a, w_in_b = _permute_w_in(w_in)
    w_uq_p = _permute_w_uq(mla_w_uq)
    w_uk_p, w_uv_p = _split_w_ukv(mla_w_ukv)
    w_out_b = w_out.astype(BF16)
    cosf, s1, s2 = _rope_tables(positions)

    def row3(a):
        return a.reshape(a.shape[0], 1, a.shape[1])

    norm_g3 = row3(norm_g)
    conv_b3 = row3(ml_conv_b)
    ml_norm_g3 = row3(ml_norm_g)
    qg3, kvg3 = row3(mla_q_norm_g), row3(mla_kv_norm_g)
    w = RW_WIDTH
    mu_r, mu_k, mu_v = row3(rw_mu[:, 0:w]), row3(rw_mu[:, w:2 * w]), row3(rw_mu[:, 2 * w:3 * w])
    mu_wa = row3(rw_mu[:, 3 * w:])
    w0, a0, k_k, k_a = row3(rw_w0), row3(rw_a0), row3(rw_k_k), row3(rw_k_a)
    w2p = jnp.pad(rw_w2, ((0, 0), (0, LANES - RW_DECAY_RANK), (0, 0)))
    a2p = jnp.pad(rw_a2, ((0, 0), (LANES - RW_AAA_RANK, 0), (0, 0)))
    v0 = row3(rw_v0)
    v1p = jnp.pad(rw_v1, ((0, 0), (0, 0), (0, LANES - RW_MV_RANK)))
    v2p = jnp.pad(rw_v2, ((0, 0), (0, LANES - RW_MV_RANK), (0, 0)))
    r_k3 = rw_r_k.reshape(depth, 1, w)
    ln_g3, ln_b3 = row3(rw_ln_g), row3(rw_ln_b)
    hid = jnp.arange(w) // RW_HEAD_DIM
    e_head = (hid[:, None] == hid[None, :]).astype(F32)
    e_pair = e_head[:LANES, :LANES].astype(BF16)
    final_g3 = final_norm_g.reshape(1, 1, D_MODEL)

    v_first = None
    h = _rmsnorm(x2, norm_g3, 0)
    for layer in range(depth):
        proj_a = _inproj(h, w_in_a, layer, TN_PROJ_A)
        proj_b = _inproj(h, w_in_b, layer, TN_PROJ_B)
        gates_t = proj_a[:, OFF_KRG + MLA_ROPE:OFF_KRG + MLA_ROPE + 2 * ML_HEADS]
        gates_t = gates_t.reshape(batch, seq, 2 * ML_HEADS).transpose(0, 2, 1)
        gates_t = gates_t.reshape(batch, 2 * ML_HEADS, 1, seq)
        y_ml = _mlstm(proj_a, gates_t, ml_i_bias, ml_f_bias, ml_conv_w, conv_b3, ml_norm_g3,
                      layer, batch, seq)
        q, k, v = _mla_proj(proj_a, cosf, s1, s2, qg3, w_uq_p, kvg3, w_uk_p, w_uv_p, layer)
        y_mla = _attention(q, k, v, proj_b, batch, seq)
        rw_out = _rwkv(proj_a, proj_b, v_first, mu_wa, mu_r, mu_k, mu_v, w0, w2p, a0, a2p, k_k, k_a, e_head,
                       v0, v1p, v2p, r_k3, ln_g3, ln_b3, e_pair, layer, batch, seq)
        y_rw = rw_out[0]
        if layer == 0:
            v_first = rw_out[1]
        if layer == depth - 1:
            (x2,) = _outproj(x2, y_ml, y_mla, y_rw, w_out_b, final_g3, layer, True)
        else:
            x2, h = _outproj(x2, y_ml, y_mla, y_rw, w_out_b, norm_g3, layer, False)
    return x2.reshape(batch, seq, D_MODEL)
```

```python
import functools

import jax
import jax.numpy as jnp
from jax import lax
from jax.experimental import pallas as pl
from jax.experimental.pallas import tpu as pltpu

F32 = jnp.float32
BF16 = jnp.bfloat16

D_MODEL = 2048
NORM_EPS = 1e-6
ML_HEADS = 4
ML_HEAD_DIM = 128
ML_WIDTH = 512
ML_CONV = 4
MLA_HEADS = 8
MLA_NOPE = 128
MLA_ROPE = 64
MLA_V = 128
MLA_WIDTH = 1024
MLA_Q_RANK = 512
MLA_KV_RANK = 256
ROPE_BASE = 10000.0
RW_HEAD_DIM = 64
RW_WIDTH = 512
RW_HEADS = 8
RW_DECAY_RANK = 64
RW_AAA_RANK = 64
RW_MV_RANK = 32
RW_GN_EPS = 64e-5

LANES = 128
SUBLANES = 8
MLA_QK_PAD = 256
VMEM_LIMIT = 48 * 1024 * 1024
NEG_BIG = -1e30
LOG2E = 1.4426950408889634

OFF_QK = 0
OFF_MLV = 1024
OFF_MLO = 1536
OFF_MLZ = 2048
OFF_CQ = 2560
OFF_CKV = 3072
OFF_KRG = 3328
OFF_WA = 3456
D_IN_A = 3584
OFF_RWR = 0
OFF_RWK = 512
OFF_RWV = 1024
OFF_RWZ = 1536
OFF_MLAZ = 2048
D_IN_B = 3072

TM_PROJ = 1024
TN_PROJ_A = 1792
TN_PROJ_B = 1536
TM_OUT = 512
TQ = 512
ML_TT = 1024
ML_L = 256
RW_TT = 512
RW_L = 64


def _split_bf16(x):
    hi = x.astype(BF16)
    lo = (x - hi.astype(F32)).astype(BF16)
    return hi, lo


def _mm(a, b, passes=1):
    if passes == 1:
        return jnp.dot(a.astype(BF16), b.astype(BF16), preferred_element_type=F32)
    ah, al = _split_bf16(a)
    bh, bl = _split_bf16(b)
    return (jnp.dot(ah, bh, preferred_element_type=F32)
            + jnp.dot(ah, bl, preferred_element_type=F32)
            + jnp.dot(al, bh, preferred_element_type=F32))


_NT = (((1,), (1,)), ((), ()))


def _mm_nt(a, b, passes=1):
    if passes == 1:
        return lax.dot_general(a.astype(BF16), b.astype(BF16), _NT, preferred_element_type=F32)
    ah, al = _split_bf16(a)
    bh, bl = _split_bf16(b)
    return (lax.dot_general(ah, bh, _NT, preferred_element_type=F32)
            + lax.dot_general(ah, bl, _NT, preferred_element_type=F32)
            + lax.dot_general(al, bh, _NT, preferred_element_type=F32))


def _sigmoid(x):
    return 0.5 * jnp.tanh(0.5 * x) + 0.5


def _silu(x):
    h = 0.5 * x
    return h * jnp.tanh(h) + h


def _log_sigmoid(x):
    return jnp.minimum(x, 0.0) - jnp.log(1.0 + jnp.exp(-jnp.abs(x)))


def _params(*sem):
    return pltpu.CompilerParams(dimension_semantics=sem, vmem_limit_bytes=VMEM_LIMIT)


def _rms_scale(x, g):
    return x * lax.rsqrt(jnp.mean(x * x, axis=-1, keepdims=True) + NORM_EPS) * g


def _rmsnorm_kernel(x_ref, g_ref, h_ref):
    h_ref[...] = _rms_scale(x_ref[...], g_ref[...]).astype(BF16)


def _rmsnorm(x2, norm_g3, layer):
    m = x2.shape[0]
    return pl.pallas_call(
        _rmsnorm_kernel,
        grid=(m // TM_OUT,),
        in_specs=[pl.BlockSpec((TM_OUT, D_MODEL), lambda i: (i, 0)),
                  pl.BlockSpec((None, 1, D_MODEL), lambda i: (layer, 0, 0))],
        out_specs=pl.BlockSpec((TM_OUT, D_MODEL), lambda i: (i, 0)),
        out_shape=jax.ShapeDtypeStruct((m, D_MODEL), BF16),
        compiler_params=_params("parallel"),
        name="rmsnorm",
    )(x2, norm_g3)


def _inproj_kernel(h_ref, w_ref, o_ref):
    o_ref[...] = jnp.dot(h_ref[...], w_ref[...], preferred_element_type=F32)


def _inproj(h, w_in_p, layer, tn):
    m = h.shape[0]
    d_out = w_in_p.shape[2]
    return pl.pallas_call(
        _inproj_kernel,
        grid=(m // TM_PROJ, d_out // tn),
        in_specs=[
            pl.BlockSpec((TM_PROJ, D_MODEL), lambda i, j: (i, 0)),
            pl.BlockSpec((None, D_MODEL, tn), lambda i, j: (layer, 0, j)),
        ],
        out_specs=pl.BlockSpec((TM_PROJ, tn), lambda i, j: (i, j)),
        out_shape=jax.ShapeDtypeStruct((m, d_out), F32),
        compiler_params=_params("parallel", "arbitrary"),
        name="inproj",
    )(h, w_in_p)


def _outproj_kernel(x_ref, yml_ref, ymla_ref, yrw_ref, w_ref, g_ref, *out_refs, final):
    acc = x_ref[...]
    acc = acc + jnp.dot(yml_ref[...], w_ref[0:ML_WIDTH, :], preferred_element_type=F32)
    acc = acc + jnp.dot(ymla_ref[...], w_ref[ML_WIDTH:ML_WIDTH + MLA_WIDTH, :],
                        preferred_element_type=F32)
    acc = acc + jnp.dot(yrw_ref[...], w_ref[ML_WIDTH + MLA_WIDTH:, :], preferred_element_type=F32)
    normed = _rms_scale(acc, g_ref[...])
    if final:
        out_refs[0][...] = normed
    else:
        out_refs[0][...] = acc
        out_refs[1][...] = normed.astype(BF16)


def _outproj(x2, y_ml, y_mla, y_rw, w_out_b, gains3, layer, final):
    m = x2.shape[0]
    g_index = 0 if final else layer + 1
    row = pl.BlockSpec((TM_OUT, D_MODEL), lambda i: (i, 0))
    out_specs = [row] if final else [row, row]
    out_shape = [jax.ShapeDtypeStruct((m, D_MODEL), F32)]
    if not final:
        out_shape.append(jax.ShapeDtypeStruct((m, D_MODEL), BF16))
    return pl.pallas_call(
        functools.partial(_outproj_kernel, final=final),
        grid=(m // TM_OUT,),
        in_specs=[
            row,
            pl.BlockSpec((TM_OUT, ML_WIDTH), lambda i: (i, 0)),
            pl.BlockSpec((TM_OUT, MLA_WIDTH), lambda i: (i, 0)),
            pl.BlockSpec((TM_OUT, RW_WIDTH), lambda i: (i, 0)),
            pl.BlockSpec((None, D_MODEL, D_MODEL), lambda i: (layer, 0, 0)),
            pl.BlockSpec((None, 1, D_MODEL), lambda i: (g_index, 0, 0)),
        ],
        out_specs=out_specs,
        out_shape=out_shape,
        compiler_params=_params("parallel"),
        name="outproj",
    )(x2, y_ml, y_mla, y_rw, w_out_b, gains3)


def _mla_proj_kernel(cq_ref, ckv_ref, krg_ref, cos_ref, s1_ref, s2_ref, qg_ref, wq_ref, kvg_ref,
                     wk_ref, wv_ref, q_out, k_out, v_out):
    cq = cq_ref[...]
    cqn = (cq * lax.rsqrt(jnp.mean(cq * cq, axis=-1, keepdims=True) + NORM_EPS)
           * qg_ref[...]).astype(BF16)
    q = jnp.dot(cqn, wq_ref[...], preferred_element_type=F32)
    ckv = ckv_ref[...]
    ckvn = (ckv * lax.rsqrt(jnp.mean(ckv * ckv, axis=-1, keepdims=True) + NORM_EPS)
            * kvg_ref[...]).astype(BF16)
    kn = jnp.dot(ckvn, wk_ref[...], preferred_element_type=F32)
    v = jnp.dot(ckvn, wv_ref[...], preferred_element_type=F32)
    cosf, s1, s2 = cos_ref[...], s1_ref[...], s2_ref[...]

    def rope(xb):
        return (xb * cosf + pltpu.roll(xb, LANES - MLA_ROPE // 2, 1) * s1
                + pltpu.roll(xb, MLA_ROPE // 2, 1) * s2)

    scale = (MLA_NOPE + MLA_ROPE) ** -0.5 * LOG2E
    kr = rope(krg_ref[...]).astype(BF16)
    q_parts = []
    for h in range(MLA_HEADS):
        c0 = h * MLA_QK_PAD
        q_parts += [q[:, c0:c0 + MLA_NOPE], rope(q[:, c0 + MLA_NOPE:c0 + MLA_QK_PAD])]
        k_out[:, c0:c0 + MLA_NOPE] = kn[:, h * MLA_NOPE:(h + 1) * MLA_NOPE].astype(BF16)
        k_out[:, c0 + MLA_NOPE:c0 + MLA_QK_PAD] = kr
    q_out[...] = (jnp.concatenate(q_parts, axis=1) * scale).T.astype(BF16)
    v_out[...] = v.T.astype(BF16)


def _mla_proj(proj, cosf, s1, s2, qg3, wq_p, kvg3, wk_p, wv_p, layer):
    m = proj.shape[0]
    tm = TQ
    row = lambda i: (i, 0)
    return pl.pallas_call(
        _mla_proj_kernel,
        grid=(m // tm,),
        in_specs=[
            pl.BlockSpec((tm, MLA_Q_RANK), lambda i: (i, OFF_CQ // MLA_Q_RANK)),
            pl.BlockSpec((tm, MLA_KV_RANK), lambda i: (i, OFF_CKV // MLA_KV_RANK)),
            pl.BlockSpec((tm, LANES), lambda i: (i, OFF_KRG // LANES)),
            pl.BlockSpec((tm, LANES), row),
            pl.BlockSpec((tm, LANES), row),
            pl.BlockSpec((tm, LANES), row),
            pl.BlockSpec((None, 1, MLA_Q_RANK), lambda i: (layer, 0, 0)),
            pl.BlockSpec((None, MLA_Q_RANK, MLA_HEADS * MLA_QK_PAD), lambda i: (layer, 0, 0)),
            pl.BlockSpec((None, 1, MLA_KV_RANK), lambda i: (layer, 0, 0)),
            pl.BlockSpec((None, MLA_KV_RANK, MLA_HEADS * MLA_NOPE), lambda i: (layer, 0, 0)),
            pl.BlockSpec((None, MLA_KV_RANK, MLA_HEADS * MLA_V), lambda i: (layer, 0, 0)),
        ],
        out_specs=[
            pl.BlockSpec((None, MLA_HEADS * MLA_QK_PAD, tm), lambda i: (i, 0, 0)),
            pl.BlockSpec((tm, MLA_HEADS * MLA_QK_PAD), row),
            pl.BlockSpec((None, MLA_HEADS * MLA_V, tm), lambda i: (i, 0, 0)),
        ],
        out_shape=[
            jax.ShapeDtypeStruct((m // tm, MLA_HEADS * MLA_QK_PAD, tm), BF16),
            jax.ShapeDtypeStruct((m, MLA_HEADS * MLA_QK_PAD), BF16),
            jax.ShapeDtypeStruct((m // tm, MLA_HEADS * MLA_V, tm), BF16),
        ],
        compiler_params=_params("parallel"),
        name="mla_proj",
    )(proj, proj, proj, cosf, s1, s2, qg3, wq_p, kvg3, wk_p, wv_p)


ATTN_HEADS_PER_STEP = 4


def _attn_kernel(q_ref, k_ref, vt_ref, z_ref, o_ref, *, tq):
    i = pl.program_id(2)
    nh = ATTN_HEADS_PER_STEP
    th = tq // 2
    chains = [(h, u) for h in range(nh) for u in range(2)]

    def head_cols(h):
        return slice(h * MLA_QK_PAD, (h + 1) * MLA_QK_PAD)

    def softmax_pv(carry_c, s, vt):
        m, l, acc = carry_c
        m_new = jnp.maximum(m, jnp.max(s, axis=0, keepdims=True))
        alpha = jnp.exp2(m - m_new)
        p = jnp.exp2(s - m_new)
        l = alpha * l + jnp.sum(p, axis=0, keepdims=True)
        acc = alpha * acc + jnp.dot(vt, p.astype(BF16), preferred_element_type=F32)
        return m_new, l, acc

    def step(j, carry):
        off = pl.multiple_of(j * tq, tq)
        ss = []
        for h in range(nh):
            s = jnp.dot(k_ref[pl.ds(off, tq), head_cols(h)], q_ref[head_cols(h), :],
                        preferred_element_type=F32)
            ss += [s[:, :th], s[:, th:]]
        return tuple(softmax_pv(carry[n], ss[n], vt_ref[j, chains[n][0]])
                     for n in range(len(chains)))

    def diagonal_step(carry):
        off = pl.multiple_of(i * tq, tq)
        ss, vts = [], []
        for h, u in chains:
            nk = th if u == 0 else tq
            s = jnp.dot(k_ref[pl.ds(off, nk), head_cols(h)],
                        q_ref[head_cols(h), u * th:(u + 1) * th],
                        preferred_element_type=F32)
            kidx = lax.broadcasted_iota(jnp.int32, (nk, th), 0)
            qidx = lax.broadcasted_iota(jnp.int32, (nk, th), 1) + u * th
            ss.append(jnp.where(kidx <= qidx, s, NEG_BIG))
            vts.append(vt_ref[i, h][:, :nk])
        return tuple(softmax_pv(carry[n], ss[n], vts[n]) for n in range(len(chains)))

    init = tuple((jnp.full((1, th), NEG_BIG, F32), jnp.zeros((1, th), F32),
                  jnp.zeros((MLA_V, th), F32)) for _ in chains)
    carry = lax.fori_loop(0, i, step, init)
    carry = diagonal_step(carry)
    for n, (h, u) in enumerate(chains):
        _, l, acc = carry[n]
        y = (acc / l).T
        rs = slice(u * th, (u + 1) * th)
        cs = slice(h * MLA_V, (h + 1) * MLA_V)
        o_ref[rs, cs] = (_silu(z_ref[rs, cs]) * y).astype(BF16)


def _attention(q_t, k, v_t, proj, batch, seq):
    m = k.shape[0]
    nq = seq // TQ
    nh = ATTN_HEADS_PER_STEP
    v_t5 = v_t.reshape(batch, nq, MLA_HEADS, MLA_V, TQ)
    return pl.pallas_call(
        functools.partial(_attn_kernel, tq=TQ),
        grid=(batch, MLA_HEADS // nh, nq),
        in_specs=[
            pl.BlockSpec((None, nh * MLA_QK_PAD, TQ), lambda b, h, i: (b * nq + i, h, 0)),
            pl.BlockSpec((seq, nh * MLA_QK_PAD), lambda b, h, i: (b, h)),
            pl.BlockSpec((None, nq, nh, MLA_V, TQ), lambda b, h, i: (b, 0, h, 0, 0)),
            pl.BlockSpec((TQ, nh * MLA_V),
                         lambda b, h, i: (b * nq + i, OFF_MLAZ // (nh * MLA_V) + h)),
        ],
        out_specs=pl.BlockSpec((TQ, nh * MLA_V), lambda b, h, i: (b * nq + i, h)),
        out_shape=jax.ShapeDtypeStruct((m, MLA_WIDTH), BF16),
        compiler_params=_params("parallel", "parallel", "arbitrary"),
        name="mla_attn",
    )(q_t, k, v_t5, proj)


def _shift_rows(x, prev8, s):
    rolled = pltpu.roll(x, s, 0)
    prev_rolled = pltpu.roll(prev8, s, 0)
    rid = lax.broadcasted_iota(jnp.int32, (SUBLANES, x.shape[1]), 0)
    top = jnp.where(rid < s, prev_rolled, rolled[0:SUBLANES])
    return jnp.concatenate([top, rolled[SUBLANES:]], axis=0)


def _mlstm_kernel(ib_ref, fb_ref, q_ref, k_ref, v_ref, o_ref, z_ref, g_ref, w_ref, b_ref, ng_ref,
                  out_ref, ct_ref, n_ref, m_ref, qp_ref, kp_ref, *, layer, tt, chunk):
    @pl.when(pl.program_id(1) == 0)
    def _():
        for ref in (ct_ref, n_ref, m_ref, qp_ref, kp_ref):
            ref[...] = jnp.zeros_like(ref)

    hd = ML_HEAD_DIM
    w_all = w_ref[...]
    b_all = b_ref[...]

    def conv_silu(x_ref, prev_ref, w, b):
        x = x_ref[...]
        prev8 = prev_ref[...]
        y = b + w[ML_CONV - 1:ML_CONV] * x
        for s in range(1, ML_CONV):
            y = y + w[ML_CONV - 1 - s:ML_CONV - s] * _shift_rows(x, prev8, s)
        prev_ref[...] = x[tt - SUBLANES:tt]
        return _silu(y)

    q_all = conv_silu(q_ref, qp_ref, w_all[:, :ML_WIDTH], b_all[:, :ML_WIDTH]) * (hd ** -0.5)
    k_all = conv_silu(k_ref, kp_ref, w_all[:, ML_WIDTH:], b_all[:, ML_WIDTH:])
    li_rows = [g_ref[h] + ib_ref[layer, h] for h in range(ML_HEADS)]
    lf_rows = [_log_sigmoid(g_ref[ML_HEADS + h] + fb_ref[layer, h]) for h in range(ML_HEADS)]

    L = chunk
    rows = lax.broadcasted_iota(jnp.int32, (L, L), 0)
    cols = lax.broadcasted_iota(jnp.int32, (L, L), 1)
    tri = rows >= cols
    eye = rows == cols
    nchunk = tt // L
    upper = jnp.where(rows <= cols, 1.0, 0.0).astype(BF16)
    pad_rows = jnp.zeros((SUBLANES - ML_HEADS, L), F32)
    insts = []
    for c in range(nchunk):
        sl = slice(c * L, (c + 1) * L)
        lf8 = jnp.concatenate([lf_rows[h][:, sl] for h in range(ML_HEADS)] + [pad_rows], axis=0)
        b_rows = jnp.zeros((SUBLANES, L), F32)
        rest = lf8
        for _ in range(3):
            piece = rest.astype(BF16)
            b_rows = b_rows + jnp.dot(piece, upper, preferred_element_type=F32)
            rest = rest - piece.astype(F32)
        for h in range(ML_HEADS):
            cs = slice(h * hd, (h + 1) * hd)
            li_row = li_rows[h][:, sl]
            b_row = b_rows[h:h + 1, :]
            b_col = jnp.sum(jnp.where(eye, jnp.broadcast_to(b_row, (L, L)), 0.0),
                            axis=-1, keepdims=True)
            li_col = jnp.sum(jnp.where(eye, jnp.broadcast_to(li_row, (L, L)), 0.0),
                             axis=-1, keepdims=True)
            dmat = jnp.where(tri, b_col - b_row + li_row, NEG_BIG)
            m_loc = jnp.max(dmat, axis=-1, keepdims=True)
            b_last = b_col[L - 1:L, :]
            g_col = b_last - b_col + li_col
            g_max = jnp.max(g_col, axis=0, keepdims=True)
            kc = k_all[sl, cs]
            insts.append(dict(c=c, h=h, sl=sl, cs=cs, qc=q_all[sl, cs].astype(BF16),
                              kc=kc.astype(BF16), vc=v_ref[sl, cs].astype(BF16),
                              pm=jnp.exp(dmat - m_loc), m_loc=m_loc, b_col=b_col, b_last=b_last,
                              g_max=g_max, kwg=kc * jnp.exp(g_col - g_max)))
    for s in insts:
        s["smat"] = lax.dot_general(s["qc"], s["kc"], _NT, preferred_element_type=F32) * s["pm"]
    for s in insts:
        s["sv"] = jnp.dot(s["smat"].astype(BF16), s["vc"], preferred_element_type=F32)
        s["ssum"] = jnp.sum(s["smat"], axis=-1, keepdims=True)
        s["kv"] = jnp.dot(s["kwg"].T.astype(BF16), s["vc"], preferred_element_type=F32)
        s["ksum"] = jnp.sum(s["kwg"], axis=0, keepdims=True)

    for c in range(nchunk):
        for s in insts[c * ML_HEADS:(c + 1) * ML_HEADS]:
            h = s["h"]
            ct = ct_ref[h]
            n_row = n_ref[h]
            m_prev = m_ref[h][:, 0:1]
            m_inter = s["b_col"] + m_prev
            m_t = jnp.maximum(m_inter, s["m_loc"])
            inter = jnp.exp(m_inter - m_t)
            loc = jnp.exp(s["m_loc"] - m_t)
            num = (inter * jnp.dot(s["qc"], ct.astype(BF16), preferred_element_type=F32)
                   + loc * s["sv"])
            den = (inter * jnp.sum(s["qc"].astype(F32) * n_row, axis=-1, keepdims=True)
                   + loc * s["ssum"])
            s["hh"] = num / jnp.maximum(jnp.abs(den), jnp.exp(-m_t))
            m_new = jnp.maximum(s["b_last"] + m_prev, s["g_max"])
            decay = jnp.exp(s["b_last"] + m_prev - m_new)
            scale = jnp.exp(s["g_max"] - m_new)
            ct_ref[h] = decay * ct + scale * s["kv"]
            n_ref[h] = decay * n_row + scale * s["ksum"]
            m_ref[h] = jnp.broadcast_to(m_new, (1, hd))
    for s in insts:
        sl, cs = s["sl"], s["cs"]
        hh = s["hh"]
        mu = jnp.mean(hh, axis=-1, keepdims=True)
        dd = hh - mu
        var = jnp.mean(dd * dd, axis=-1, keepdims=True)
        yn = dd * lax.rsqrt(var + NORM_EPS) * ng_ref[:, cs]
        out_ref[sl, cs] = (_silu(z_ref[sl, cs]) * (_sigmoid(o_ref[sl, cs]) * yn)).astype(BF16)


def _mlstm(proj, gates_t, i_bias, f_bias, conv_w, conv_b3, norm_g3, layer, batch, seq):
    m = proj.shape[0]
    tt = min(ML_TT, seq)
    nt = seq // tt
    hd = ML_HEAD_DIM
    w = ML_WIDTH

    def col(off):
        return pl.BlockSpec((tt, w), lambda b, t: (b * nt + t, off // w))

    smem = pl.BlockSpec(memory_space=pltpu.SMEM)
    return pl.pallas_call(
        functools.partial(_mlstm_kernel, layer=layer, tt=tt, chunk=min(ML_L, tt)),
        grid=(batch, nt),
        in_specs=[
            smem, smem,
            col(OFF_QK), col(OFF_QK + w), col(OFF_MLV), col(OFF_MLO), col(OFF_MLZ),
            pl.BlockSpec((None, 2 * ML_HEADS, 1, tt), lambda b, t: (b, 0, 0, t)),
            pl.BlockSpec((None, ML_CONV, 2 * w), lambda b, t: (layer, 0, 0)),
            pl.BlockSpec((None, 1, 2 * w), lambda b, t: (layer, 0, 0)),
            pl.BlockSpec((None, 1, w), lambda b, t: (layer, 0, 0)),
        ],
        out_specs=pl.BlockSpec((tt, w), lambda b, t: (b * nt + t, 0)),
        out_shape=jax.ShapeDtypeStruct((m, w), BF16),
        scratch_shapes=[
            pltpu.VMEM((ML_HEADS, hd, hd), F32),
            pltpu.VMEM((ML_HEADS, 1, hd), F32),
            pltpu.VMEM((ML_HEADS, 1, hd), F32),
            pltpu.VMEM((SUBLANES, w), F32),
            pltpu.VMEM((SUBLANES, w), F32),
        ],
        compiler_params=_params("parallel", "arbitrary"),
        name="mlstm",
    )(i_bias, f_bias, proj, proj, proj, proj, proj, gates_t, conv_w, conv_b3, norm_g3)


RW_PRE_PASSES = 3


def _rwkv_pre_kernel(*refs, has_vres, tm):
    if has_vres:
        (wa_ref, r_ref, k_ref, v_ref, mu_wa, mu_r, mu_k, mu_v, w0_ref, w2_ref, a0_ref, a2_ref,
         kk_ref, ka_ref, e_ref, vf_ref, v0_ref, v1_ref, v2_ref,
         r_out, ld_out, kh_out, v_out, kn_out, b_out, c_wa, c_r, c_k, c_v) = refs
    else:
        (wa_ref, r_ref, k_ref, v_ref, mu_wa, mu_r, mu_k, mu_v, w0_ref, w2_ref, a0_ref, a2_ref,
         kk_ref, ka_ref, e_ref,
         r_out, ld_out, kh_out, v_out, kn_out, b_out, c_wa, c_r, c_k, c_v) = refs

    @pl.when(pl.program_id(1) == 0)
    def _():
        for c in (c_wa, c_r, c_k, c_v):
            c[...] = jnp.zeros_like(c)

    def mix(x_ref, c_ref, mu_ref):
        x = x_ref[...]
        rolled = pltpu.roll(x, 1, 0)
        rid = lax.broadcasted_iota(jnp.int32, (SUBLANES, x.shape[1]), 0)
        prev_last = jnp.broadcast_to(c_ref[SUBLANES - 1:SUBLANES, :], (SUBLANES, x.shape[1]))
        top = jnp.where(rid == 0, prev_last, rolled[0:SUBLANES])
        xprev = jnp.concatenate([top, rolled[SUBLANES:]], axis=0)
        c_ref[...] = x[tm - SUBLANES:tm]
        return x + mu_ref[...] * (xprev - x)

    xwa = mix(wa_ref, c_wa, mu_wa)
    r = mix(r_ref, c_r, mu_r)
    k = mix(k_ref, c_k, mu_k)
    v = mix(v_ref, c_v, mu_v)
    zw = w0_ref[...] + _mm(jnp.tanh(xwa), w2_ref[...], RW_PRE_PASSES)
    za = a0_ref[...] + _mm(xwa, a2_ref[...], RW_PRE_PASSES)
    log_w = _log_sigmoid(zw) - 0.5
    ld_out[...] = -jnp.exp(log_w)
    a = _sigmoid(za)
    if has_vres:
        gate = _sigmoid(v0_ref[...] + _mm(_mm(v, v1_ref[...], RW_PRE_PASSES), v2_ref[...],
                                          RW_PRE_PASSES))
        v = v + (vf_ref[...] - v) * gate
    kk = k * kk_ref[...]
    ss = _mm(kk * kk, e_ref[...])
    kn = kk / jnp.maximum(jnp.sqrt(ss), 1e-12)
    r_out[...] = r
    kh_out[...] = k * (1.0 + (a - 1.0) * ka_ref[...])
    v_out[...] = v
    kn_out[...] = kn
    b_out[...] = kn * a


RW_GROUP = 4


def _rwkv_scan_kernel(r_ref, ld_ref, kh_ref, v_ref, kn_ref, b_ref, z_ref, rk_ref, g_ref, bias_ref,
                      e_ref, out_ref, m_ref, ac_ref, gc_ref, qs_ref, y_ref, *, tt, chunk):
    @pl.when(pl.program_id(1) == 0)
    def _():
        m_ref[...] = jnp.zeros_like(m_ref)

    L = chunk
    L2 = 2 * L
    nchunk = tt // L
    npair = RW_WIDTH // LANES
    rows = lax.broadcasted_iota(jnp.int32, (L, L), 0)
    cols = lax.broadcasted_iota(jnp.int32, (L, L), 1)
    tri_b = jnp.where(rows >= cols, 1.0, 0.0).astype(BF16)
    lane = lax.broadcasted_iota(jnp.int32, (1, LANES), 1)
    m0 = jnp.where(lane < RW_HEAD_DIM, 1.0, 0.0)
    m1 = 1.0 - m0
    r2 = lax.broadcasted_iota(jnp.int32, (L2, L2), 0)
    c2 = lax.broadcasted_iota(jnp.int32, (L2, L2), 1)
    same_blk = (r2 < L) == (c2 < L)
    t2 = jnp.where(r2 < L, r2, r2 - L)
    s2 = jnp.where(c2 < L, c2, c2 - L)
    mask_strict = jnp.logical_and(same_blk, t2 > s2)
    mask_incl = jnp.logical_and(same_blk, t2 >= s2)
    eye2 = lax.broadcasted_iota(jnp.int32, (LANES, LANES), 0) == lax.broadcasted_iota(
        jnp.int32, (LANES, LANES), 1)
    n_double = max(1, (L - 1).bit_length())

    def stack(x):
        return jnp.concatenate([x * m0, x * m1], axis=0)

    def prepare(p, c):
        sl = slice(c * L, (c + 1) * L)
        cs = slice(p * LANES, (p + 1) * LANES)
        ld = ld_ref[sl, cs]
        r, kh, v, kn, b = r_ref[sl, cs], kh_ref[sl, cs], v_ref[sl, cs], kn_ref[sl, cs], b_ref[sl, cs]
        ld_hi, ld_lo = _split_bf16(ld)
        logp = (jnp.dot(tri_b, ld_hi, preferred_element_type=F32)
                + jnp.dot(tri_b, ld_lo, preferred_element_type=F32))
        cmid = logp[L // 2 - 1:L // 2, :]
        e = logp - cmid
        e_last = e[L - 1:L, :]
        p_mid = jnp.exp(cmid)
        at_s = stack(-kn * jnp.exp(e - ld))
        rt_s = stack(r * jnp.exp(e))
        inv = jnp.exp(-e)
        tail = jnp.exp(e_last - e)
        lhs = jnp.concatenate([at_s, rt_s], axis=0).astype(BF16)
        rhs = jnp.concatenate([stack(b * inv), stack(kh * inv)], axis=0).astype(BF16)
        hat_t = jnp.concatenate([stack(b * tail), stack(kh * tail)], axis=0).T.astype(BF16)
        return dict(sl=sl, cs=cs, idx=p * nchunk + c, lhs=lhs, rhs=rhs, hat_t=hat_t,
                    at_true=at_s * p_mid, rt_true=rt_s * p_mid, v_s=stack(v).astype(BF16),
                    p_last=jnp.exp(e_last + cmid))

    def coeff_stages(cg):
        insts = [prepare(p, cg * RW_GROUP + g) for p in range(npair) for g in range(RW_GROUP)]
        for s in insts:
            aa = lax.dot_general(s["lhs"], s["rhs"], _NT, preferred_element_type=F32)
            s["nmat"] = jnp.where(mask_strict, aa[0:L2, 0:L2], 0.0).astype(BF16)
            s["a_ak"] = jnp.where(mask_strict, aa[0:L2, L2:], 0.0).astype(BF16)
            s["a_r"] = jnp.concatenate([jnp.where(mask_incl, aa[L2:, 0:L2], 0.0),
                                        jnp.where(mask_incl, aa[L2:, L2:], 0.0)],
                                       axis=1).astype(BF16)
        yield
        for s in insts:
            s["x"] = jnp.concatenate(
                [s["at_true"], jnp.dot(s["a_ak"], s["v_s"], preferred_element_type=F32)],
                axis=1)
        yield
        for it in range(n_double):
            for s in insts:
                s["x"] = s["x"] + jnp.dot(s["nmat"], s["x"].astype(BF16),
                                          preferred_element_type=F32)
            if it + 1 < n_double:
                for s in insts:
                    s["nmat"] = jnp.dot(s["nmat"], s["nmat"],
                                        preferred_element_type=F32).astype(BF16)
            yield
        for s in insts:
            big_l = jnp.concatenate([s["a_r"], s["hat_t"]], axis=0)
            big_r = jnp.concatenate(
                [s["x"].astype(BF16),
                 jnp.concatenate([jnp.zeros((L2, LANES), BF16), s["v_s"]], axis=1)], axis=0)
            res = jnp.dot(big_l, big_r, preferred_element_type=F32)
            q_s = s["rt_true"] + res[0:L2, :LANES]
            y0_s = res[0:L2, LANES:]
            qs_ref[s["sl"], s["cs"]] = q_s[0:L] + q_s[L:]
            y_ref[s["sl"], s["cs"]] = y0_s[0:L] + y0_s[L:]
            ac_ref[s["idx"]] = (jnp.where(eye2, jnp.broadcast_to(s["p_last"], (LANES, LANES)), 0.0)
                                + res[L2:, :LANES])
            gc_ref[s["idx"]] = res[L2:, LANES:]

    def chain_step(c):
        sl = slice(c * L, (c + 1) * L)
        for p in range(npair):
            cs = slice(p * LANES, (p + 1) * LANES)
            idx = p * nchunk + c
            mp = m_ref[p]
            y_ref[sl, cs] = _mm(qs_ref[sl, cs], mp) + y_ref[sl, cs]
            m_ref[p] = _mm(ac_ref[idx], mp) + gc_ref[idx]

    pending = []
    for cg in range(nchunk // RW_GROUP):
        for _ in coeff_stages(cg):
            if pending:
                chain_step(pending.pop(0))
        for c in pending:
            chain_step(c)
        pending = [cg * RW_GROUP + g for g in range(RW_GROUP)]
    for c in pending:
        chain_step(c)

    e_b = e_ref[...]
    inv_n = 1.0 / RW_HEAD_DIM

    def head_sum(x):
        return jnp.dot(x.astype(BF16), e_b, preferred_element_type=F32)

    for p in range(npair):
        cs = slice(p * LANES, (p + 1) * LANES)
        y = y_ref[:, cs]
        mu = head_sum(y) * inv_n
        d = y - mu
        var = head_sum(d * d) * inv_n
        yn = d * lax.rsqrt(var + RW_GN_EPS) * g_ref[:, cs] + bias_ref[:, cs]
        v_all = v_ref[:, cs]
        bonus = head_sum(r_ref[:, cs] * kh_ref[:, cs] * rk_ref[:, cs])
        out_ref[:, cs] = (_silu(z_ref[:, cs]) * (yn + bonus * v_all)).astype(BF16)


def _rwkv_kernel(*refs, has_vres, tt, chunk):
    n_mix = 19 if has_vres else 15
    mix_in = refs[:n_mix]
    z_ref, rk_ref, g_ref, bias_ref, e_ref = refs[n_mix:n_mix + 5]
    n_out = 1 if has_vres else 2
    outs = refs[n_mix + 5:n_mix + 5 + n_out]
    (c_wa, c_r, c_k, c_v, r_s, ld_s, kh_s, v_s, kn_s, b_s,
     m_ref, ac_ref, gc_ref, qs_ref, y_ref) = refs[n_mix + 5 + n_out:]
    _rwkv_pre_kernel(*mix_in, r_s, ld_s, kh_s, v_s, kn_s, b_s, c_wa, c_r, c_k, c_v,
                     has_vres=has_vres, tm=tt)
    if not has_vres:
        outs[1][...] = v_s[...]
    _rwkv_scan_kernel(r_s, ld_s, kh_s, v_s, kn_s, b_s, z_ref, rk_ref, g_ref, bias_ref, e_ref,
                      outs[0], m_ref, ac_ref, gc_ref, qs_ref, y_ref, tt=tt, chunk=chunk)


def _rwkv(proj_a, proj_b, v_first, mu_wa, mu_r, mu_k, mu_v, w0, w2p, a0, a2p, k_k, k_a, e_head,
          v0, v1p, v2p, r_k3, ln_g3, ln_b3, e_pair, layer, batch, seq):
    m = proj_a.shape[0]
    tt = min(RW_TT, seq)
    nt = seq // tt
    chunk = min(RW_L, tt)
    has_vres = layer > 0
    w = RW_WIDTH
    npair = w // LANES

    def colblk(off, width):
        return pl.BlockSpec((tt, width), lambda b, t: (b * nt + t, off // width))

    def lay(shape):
        return pl.BlockSpec((None,) + shape, lambda b, t: (layer,) + (0,) * len(shape))

    def lay1(shape):
        return pl.BlockSpec((None,) + shape, lambda b, t: (layer - 1,) + (0,) * len(shape))

    row = pl.BlockSpec((tt, w), lambda b, t: (b * nt + t, 0))
    in_specs = [colblk(OFF_WA, LANES), colblk(OFF_RWR, w), colblk(OFF_RWK, w), colblk(OFF_RWV, w),
                lay((1, LANES)), lay((1, w)), lay((1, w)), lay((1, w)),
                lay((1, w)), lay((LANES, w)), lay((1, w)), lay((LANES, w)),
                lay((1, w)), lay((1, w)),
                pl.BlockSpec((w, w), lambda b, t: (0, 0))]
    args = [proj_a, proj_b, proj_b, proj_b, mu_wa, mu_r, mu_k, mu_v, w0, w2p, a0, a2p, k_k, k_a, e_head]
    if has_vres:
        in_specs += [row, lay1((1, w)), lay1((w, LANES)), lay1((LANES, w))]
        args += [v_first, v0, v1p, v2p]
    in_specs += [colblk(OFF_RWZ, w), lay((1, w)), lay((1, w)), lay((1, w)),
                 pl.BlockSpec((LANES, LANES), lambda b, t: (0, 0))]
    args += [proj_b, r_k3, ln_g3, ln_b3, e_pair]
    out_specs = [row]
    out_shape = [jax.ShapeDtypeStruct((m, w), BF16)]
    if not has_vres:
        out_specs.append(row)
        out_shape.append(jax.ShapeDtypeStruct((m, w), F32))
    scratch = [pltpu.VMEM((SUBLANES, LANES), F32)] + [pltpu.VMEM((SUBLANES, w), F32)] * 3
    scratch += [pltpu.VMEM((tt, w), F32)] * 6
    scratch += [
        pltpu.VMEM((npair, LANES, LANES), F32),
        pltpu.VMEM((npair * (tt // chunk), LANES, LANES), F32),
        pltpu.VMEM((npair * (tt // chunk), LANES, LANES), F32),
        pltpu.VMEM((tt, w), F32),
        pltpu.VMEM((tt, w), F32),
    ]
    return pl.pallas_call(
        functools.partial(_rwkv_kernel, has_vres=has_vres, tt=tt, chunk=chunk),
        grid=(batch, nt),
        in_specs=in_specs,
        out_specs=out_specs,
        out_shape=out_shape,
        scratch_shapes=scratch,
        compiler_params=_params("parallel", "arbitrary"),
        name="rwkv",
    )(*args)


_W_IN_PARTS = ((0, 1024), (1024, 512), (1544, 512), (2056, 512), (2568, 512), (3080, 256),
               (3336, 64), (1536, 8), (None, LANES - MLA_ROPE - 2 * ML_HEADS), (5960, 128),
               (4424, 512), (4936, 512), (5448, 512), (6088, 512), (3400, 1024))
TK_PREP = 256


def _permute_w_in_kernel(src_ref, dst_a, dst_b):
    tk = src_ref.shape[1]
    off = 0
    pending = []

    def put(off, width, value):
        if off < D_IN_A:
            dst_a[:, off:off + width] = value
        else:
            dst_b[:, off - D_IN_A:off - D_IN_A + width] = value

    for start, width in _W_IN_PARTS:
        if start is None:
            pending.append(jnp.zeros((width, tk), F32))
        elif width % LANES:
            pending.append(src_ref[start:start + width, :])
        else:
            put(off, width, src_ref[start:start + width, :].T.astype(BF16))
            off += width
            continue
        rows = sum(p.shape[0] for p in pending)
        if rows == LANES:
            put(off, LANES, jnp.concatenate(pending, axis=0).T.astype(BF16))
            off += LANES
            pending = []


def _permute_w_in(w_in):
    depth, d_model, d_in = w_in.shape
    w_t = jnp.swapaxes(w_in, 1, 2)
    return pl.pallas_call(
        _permute_w_in_kernel,
        grid=(depth, d_model // TK_PREP),
        in_specs=[pl.BlockSpec((None, d_in, TK_PREP), lambda l, i: (l, 0, i))],
        out_specs=[pl.BlockSpec((None, TK_PREP, D_IN_A), lambda l, i: (l, i, 0)),
                   pl.BlockSpec((None, TK_PREP, D_IN_B), lambda l, i: (l, i, 0))],
        out_shape=[jax.ShapeDtypeStruct((depth, d_model, D_IN_A), BF16),
                   jax.ShapeDtypeStruct((depth, d_model, D_IN_B), BF16)],
        compiler_params=_params("parallel", "parallel"),
        name="permute_w_in",
    )(w_t)


def _permute_w_uq(w_uq):
    d = w_uq.shape[0]
    w = w_uq.reshape(d, MLA_Q_RANK, MLA_HEADS, MLA_NOPE + MLA_ROPE)
    w = jnp.pad(w, ((0, 0), (0, 0), (0, 0), (0, MLA_QK_PAD - MLA_NOPE - MLA_ROPE)))
    return w.reshape(d, MLA_Q_RANK, MLA_HEADS * MLA_QK_PAD).astype(BF16)


def _split_w_ukv(w_ukv):
    d = w_ukv.shape[0]
    w = w_ukv.reshape(d, MLA_KV_RANK, MLA_HEADS, MLA_NOPE + MLA_V)
    wk = w[..., :MLA_NOPE].reshape(d, MLA_KV_RANK, MLA_HEADS * MLA_NOPE)
    wv = w[..., MLA_NOPE:].reshape(d, MLA_KV_RANK, MLA_HEADS * MLA_V)
    return wk.astype(BF16), wv.astype(BF16)


def _rope_tables(positions):
    inv_freq = jnp.power(ROPE_BASE, -jnp.arange(0, MLA_ROPE, 2, dtype=F32) / MLA_ROPE)
    ang = positions.astype(F32).reshape(-1, 1) * inv_freq
    cos, sin = jnp.cos(ang), jnp.sin(ang)
    z32 = jnp.zeros_like(cos)
    z64 = jnp.zeros((ang.shape[0], LANES - MLA_ROPE), F32)
    cosf = jnp.concatenate([cos, cos, z64], axis=-1)
    s1 = jnp.concatenate([-sin, z32, z64], axis=-1)
    s2 = jnp.concatenate([z32, sin, z64], axis=-1)
    return cosf, s1, s2


def kernel(x, positions, norm_g, w_in, ml_conv_w, ml_conv_b, ml_i_bias, ml_f_bias, ml_norm_g,
           mla_q_norm_g, mla_w_uq, mla_kv_norm_g, mla_w_ukv, rw_mu, rw_w0, rw_w2, rw_a0, rw_a2,
           rw_v0, rw_v1, rw_v2, rw_k_k, rw_k_a, rw_r_k, rw_ln_g, rw_ln_b, w_out, final_norm_g):
    batch, seq, _ = x.shape
    depth = w_in.shape[0]
    m = batch * seq
    x2 = x.reshape(m, D_MODEL)

    w_in_a, w_in_b = _permute_w_in(w_in)
    w_uq_p = _permute_w_uq(mla_w_uq)
    w_uk_p, w_uv_p = _split_w_ukv(mla_w_ukv)
    w_out_b = w_out.astype(BF16)
    cosf, s1, s2 = _rope_tables(positions)

    def row3(a):
        return a.reshape(a.shape[0], 1, a.shape[1])

    norm_g3 = row3(norm_g)
    conv_b3 = row3(ml_conv_b)
    ml_norm_g3 = row3(ml_norm_g)
    qg3, kvg3 = row3(mla_q_norm_g), row3(mla_kv_norm_g)
    w = RW_WIDTH
    mu_r, mu_k, mu_v = row3(rw_mu[:, 0:w]), row3(rw_mu[:, w:2 * w]), row3(rw_mu[:, 2 * w:3 * w])
    mu_wa = row3(rw_mu[:, 3 * w:])
    w0, a0, k_k, k_a = row3(rw_w0), row3(rw_a0), row3(rw_k_k), row3(rw_k_a)
    w2p = jnp.pad(rw_w2, ((0, 0), (0, LANES - RW_DECAY_RANK), (0, 0)))
    a2p = jnp.pad(rw_a2, ((0, 0), (LANES - RW_AAA_RANK, 0), (0, 0)))
    v0 = row3(rw_v0)
    v1p = jnp.pad(rw_v1, ((0, 0), (0, 0), (0, LANES - RW_MV_RANK)))
    v2p = jnp.pad(rw_v2, ((0, 0), (0, LANES - RW_MV_RANK), (0, 0)))
    r_k3 = rw_r_k.reshape(depth, 1, w)
    ln_g3, ln_b3 = row3(rw_ln_g), row3(rw_ln_b)
    hid = jnp.arange(w) // RW_HEAD_DIM
    e_head = (hid[:, None] == hid[None, :]).astype(F32)
    e_pair = e_head[:LANES, :LANES].astype(BF16)
    final_g3 = final_norm_g.reshape(1, 1, D_MODEL)

    v_first = None
    h = _rmsnorm(x2, norm_g3, 0)
    for layer in range(depth):
        proj_a = _inproj(h, w_in_a, layer, TN_PROJ_A)
        proj_b = _inproj(h, w_in_b, layer, TN_PROJ_B)
        gates_t = proj_a[:, OFF_KRG + MLA_ROPE:OFF_KRG + MLA_ROPE + 2 * ML_HEADS]
        gates_t = gates_t.reshape(batch, seq, 2 * ML_HEADS).transpose(0, 2, 1)
        gates_t = gates_t.reshape(batch, 2 * ML_HEADS, 1, seq)
        y_ml = _mlstm(proj_a, gates_t, ml_i_bias, ml_f_bias, ml_conv_w, conv_b3, ml_norm_g3,
                      layer, batch, seq)
        q, k, v = _mla_proj(proj_a, cosf, s1, s2, qg3, w_uq_p, kvg3, w_uk_p, w_uv_p, layer)
        y_mla = _attention(q, k, v, proj_b, batch, seq)
        rw_out = _rwkv(proj_a, proj_b, v_first, mu_wa, mu_r, mu_k, mu_v, w0, w2p, a0, a2p, k_k, k_a, e_head,
                       v0, v1p, v2p, r_k3, ln_g3, ln_b3, e_pair, layer, batch, seq)
        y_rw = rw_out[0]
        if layer == 0:
            v_first = rw_out[1]
        if layer == depth - 1:
            (x2,) = _outproj(x2, y_ml, y_mla, y_rw, w_out_b, final_g3, layer, True)
        else:
            x2, h = _outproj(x2, y_ml, y_mla, y_rw, w_out_b, norm_g3, layer, False)
    return x2.reshape(batch, seq, D_MODEL)
```

```python
import functools

import jax
import jax.numpy as jnp
from jax import lax
from jax.experimental import pallas as pl
from jax.experimental.pallas import tpu as pltpu

F32 = jnp.float32
BF16 = jnp.bfloat16

D_MODEL = 2048
NORM_EPS = 1e-6
ML_HEADS = 4
ML_HEAD_DIM = 128
ML_WIDTH = 512
ML_CONV = 4
MLA_HEADS = 8
MLA_NOPE = 128
MLA_ROPE = 64
MLA_V = 128
MLA_WIDTH = 1024
MLA_Q_RANK = 512
MLA_KV_RANK = 256
ROPE_BASE = 10000.0
RW_HEAD_DIM = 64
RW_WIDTH = 512
RW_HEADS = 8
RW_DECAY_RANK = 64
RW_AAA_RANK = 64
RW_MV_RANK = 32
RW_GN_EPS = 64e-5

LANES = 128
SUBLANES = 8
MLA_QK_PAD = 256
VMEM_LIMIT = 48 * 1024 * 1024
NEG_BIG = -1e30
LOG2E = 1.4426950408889634

OFF_QK = 0
OFF_MLV = 1024
OFF_MLO = 1536
OFF_MLZ = 2048
OFF_CQ = 2560
OFF_CKV = 3072
OFF_KRG = 3328
OFF_WA = 3456
D_IN_A = 3584
OFF_RWR = 0
OFF_RWK = 512
OFF_RWV = 1024
OFF_RWZ = 1536
OFF_MLAZ = 2048
D_IN_B = 3072

TM_PROJ = 1024
TN_PROJ_A = 1792
TN_PROJ_B = 1536
TM_OUT = 512
TQ = 512
ML_TT = 1024
ML_L = 256
RW_TT = 512
RW_L = 64


def _split_bf16(x):
    hi = x.astype(BF16)
    lo = (x - hi.astype(F32)).astype(BF16)
    return hi, lo


def _mm(a, b, passes=1):
    if passes == 1:
        return jnp.dot(a.astype(BF16), b.astype(BF16), preferred_element_type=F32)
    ah, al = _split_bf16(a)
    bh, bl = _split_bf16(b)
    return (jnp.dot(ah, bh, preferred_element_type=F32)
            + jnp.dot(ah, bl, preferred_element_type=F32)
            + jnp.dot(al, bh, preferred_element_type=F32))


_NT = (((1,), (1,)), ((), ()))


def _sigmoid(x):
    return 0.5 * jnp.tanh(0.5 * x) + 0.5


def _silu(x):
    h = 0.5 * x
    return h * jnp.tanh(h) + h


def _log_sigmoid(x):
    return jnp.minimum(x, 0.0) - jnp.log(1.0 + jnp.exp(-jnp.abs(x)))


def _params(*sem):
    return pltpu.CompilerParams(dimension_semantics=sem, vmem_limit_bytes=VMEM_LIMIT)


def _rms_scale(x, g):
    return x * lax.rsqrt(jnp.mean(x * x, axis=-1, keepdims=True) + NORM_EPS) * g


def _rmsnorm_kernel(x_ref, g_ref, h_ref):
    h_ref[...] = _rms_scale(x_ref[...], g_ref[...]).astype(BF16)


def _rmsnorm(x2, norm_g3, layer):
    m = x2.shape[0]
    return pl.pallas_call(
        _rmsnorm_kernel,
        grid=(m // TM_OUT,),
        in_specs=[pl.BlockSpec((TM_OUT, D_MODEL), lambda i: (i, 0)),
                  pl.BlockSpec((None, 1, D_MODEL), lambda i: (layer, 0, 0))],
        out_specs=pl.BlockSpec((TM_OUT, D_MODEL), lambda i: (i, 0)),
        out_shape=jax.ShapeDtypeStruct((m, D_MODEL), BF16),
        compiler_params=_params("parallel"),
        name="rmsnorm",
    )(x2, norm_g3)


def _inproj_kernel(h_ref, w_ref, o_ref):
    o_ref[...] = jnp.dot(h_ref[...], w_ref[...], preferred_element_type=F32)


def _inproj(h, w_in_p, layer, tn):
    m = h.shape[0]
    d_out = w_in_p.shape[2]
    return pl.pallas_call(
        _inproj_kernel,
        grid=(m // TM_PROJ, d_out // tn),
        in_specs=[
            pl.BlockSpec((TM_PROJ, D_MODEL), lambda i, j: (i, 0)),
            pl.BlockSpec((None, D_MODEL, tn), lambda i, j: (layer, 0, j)),
        ],
        out_specs=pl.BlockSpec((TM_PROJ, tn), lambda i, j: (i, j)),
        out_shape=jax.ShapeDtypeStruct((m, d_out), F32),
        compiler_params=_params("parallel", "arbitrary"),
        name="inproj",
    )(h, w_in_p)


def _outproj_kernel(x_ref, yml_ref, ymla_ref, yrw_ref, w_ref, g_ref, *out_refs, final):
    acc = x_ref[...]
    acc = acc + jnp.dot(yml_ref[...], w_ref[0:ML_WIDTH, :], preferred_element_type=F32)
    acc = acc + jnp.dot(ymla_ref[...], w_ref[ML_WIDTH:ML_WIDTH + MLA_WIDTH, :],
                        preferred_element_type=F32)
    acc = acc + jnp.dot(yrw_ref[...], w_ref[ML_WIDTH + MLA_WIDTH:, :], preferred_element_type=F32)
    normed = _rms_scale(acc, g_ref[...])
    if final:
        out_refs[0][...] = normed
    else:
        out_refs[0][...] = acc
        out_refs[1][...] = normed.astype(BF16)


def _outproj(x2, y_ml, y_mla, y_rw, w_out_b, gains3, layer, final):
    m = x2.shape[0]
    g_index = 0 if final else layer + 1
    row = pl.BlockSpec((TM_OUT, D_MODEL), lambda i: (i, 0))
    out_specs = [row] if final else [row, row]
    out_shape = [jax.ShapeDtypeStruct((m, D_MODEL), F32)]
    if not final:
        out_shape.append(jax.ShapeDtypeStruct((m, D_MODEL), BF16))
    return pl.pallas_call(
        functools.partial(_outproj_kernel, final=final),
        grid=(m // TM_OUT,),
        in_specs=[
            row,
            pl.BlockSpec((TM_OUT, ML_WIDTH), lambda i: (i, 0)),
            pl.BlockSpec((TM_OUT, MLA_WIDTH), lambda i: (i, 0)),
            pl.BlockSpec((TM_OUT, RW_WIDTH), lambda i: (i, 0)),
            pl.BlockSpec((None, D_MODEL, D_MODEL), lambda i: (layer, 0, 0)),
            pl.BlockSpec((None, 1, D_MODEL), lambda i: (g_index, 0, 0)),
        ],
        out_specs=out_specs,
        out_shape=out_shape,
        compiler_params=_params("parallel"),
        name="outproj",
    )(x2, y_ml, y_mla, y_rw, w_out_b, gains3)


def _mla_proj_kernel(cq_ref, ckv_ref, krg_ref, cos_ref, s1_ref, s2_ref, qg_ref, wq_ref, kvg_ref,
                     wk_ref, wv_ref, q_out, k_out, v_out):
    cq = cq_ref[...]
    cqn = (cq * lax.rsqrt(jnp.mean(cq * cq, axis=-1, keepdims=True) + NORM_EPS)
           * qg_ref[...]).astype(BF16)
    q = jnp.dot(cqn, wq_ref[...], preferred_element_type=F32)
    ckv = ckv_ref[...]
    ckvn = (ckv * lax.rsqrt(jnp.mean(ckv * ckv, axis=-1, keepdims=True) + NORM_EPS)
            * kvg_ref[...]).astype(BF16)
    kn = jnp.dot(ckvn, wk_ref[...], preferred_element_type=F32)
    v = jnp.dot(ckvn, wv_ref[...], preferred_element_type=F32)
    cosf, s1, s2 = cos_ref[...], s1_ref[...], s2_ref[...]

    def rope(xb):
        return (xb * cosf + pltpu.roll(xb, LANES - MLA_ROPE // 2, 1) * s1
                + pltpu.roll(xb, MLA_ROPE // 2, 1) * s2)

    scale = (MLA_NOPE + MLA_ROPE) ** -0.5 * LOG2E
    kr = rope(krg_ref[...]).astype(BF16)
    q_parts = []
    for h in range(MLA_HEADS):
        c0 = h * MLA_QK_PAD
        q_parts += [q[:, c0:c0 + MLA_NOPE], rope(q[:, c0 + MLA_NOPE:c0 + MLA_QK_PAD])]
        k_out[:, c0:c0 + MLA_NOPE] = kn[:, h * MLA_NOPE:(h + 1) * MLA_NOPE].astype(BF16)
        k_out[:, c0 + MLA_NOPE:c0 + MLA_QK_PAD] = kr
    q_out[...] = (jnp.concatenate(q_parts, axis=1) * scale).astype(BF16).T
    v_out[...] = v.astype(BF16).T


def _mla_proj(proj, cosf, s1, s2, qg3, wq_p, kvg3, wk_p, wv_p, layer):
    m = proj.shape[0]
    tm = TQ
    row = lambda i: (i, 0)
    return pl.pallas_call(
        _mla_proj_kernel,
        grid=(m // tm,),
        in_specs=[
            pl.BlockSpec((tm, MLA_Q_RANK), lambda i: (i, OFF_CQ // MLA_Q_RANK)),
            pl.BlockSpec((tm, MLA_KV_RANK), lambda i: (i, OFF_CKV // MLA_KV_RANK)),
            pl.BlockSpec((tm, LANES), lambda i: (i, OFF_KRG // LANES)),
            pl.BlockSpec((tm, LANES), row),
            pl.BlockSpec((tm, LANES), row),
            pl.BlockSpec((tm, LANES), row),
            pl.BlockSpec((None, 1, MLA_Q_RANK), lambda i: (layer, 0, 0)),
            pl.BlockSpec((None, MLA_Q_RANK, MLA_HEADS * MLA_QK_PAD), lambda i: (layer, 0, 0)),
            pl.BlockSpec((None, 1, MLA_KV_RANK), lambda i: (layer, 0, 0)),
            pl.BlockSpec((None, MLA_KV_RANK, MLA_HEADS * MLA_NOPE), lambda i: (layer, 0, 0)),
            pl.BlockSpec((None, MLA_KV_RANK, MLA_HEADS * MLA_V), lambda i: (layer, 0, 0)),
        ],
        out_specs=[
            pl.BlockSpec((None, MLA_HEADS * MLA_QK_PAD, tm), lambda i: (i, 0, 0)),
            pl.BlockSpec((tm, MLA_HEADS * MLA_QK_PAD), row),
            pl.BlockSpec((None, MLA_HEADS * MLA_V, tm), lambda i: (i, 0, 0)),
        ],
        out_shape=[
            jax.ShapeDtypeStruct((m // tm, MLA_HEADS * MLA_QK_PAD, tm), BF16),
            jax.ShapeDtypeStruct((m, MLA_HEADS * MLA_QK_PAD), BF16),
            jax.ShapeDtypeStruct((m // tm, MLA_HEADS * MLA_V, tm), BF16),
        ],
        compiler_params=_params("parallel"),
        name="mla_proj",
    )(proj, proj, proj, cosf, s1, s2, qg3, wq_p, kvg3, wk_p, wv_p)


ATTN_HEADS_PER_STEP = 4


def _attn_kernel(q_ref, k_ref, vt_ref, z_ref, o_ref, *, tq):
    i = pl.program_id(2)
    nh = ATTN_HEADS_PER_STEP
    th = tq // 2
    chains = [(h, u) for h in range(nh) for u in range(2)]

    def head_cols(h):
        return slice(h * MLA_QK_PAD, (h + 1) * MLA_QK_PAD)

    def softmax_pv(carry_c, s, vt):
        m, l, acc = carry_c
        m_new = jnp.maximum(m, jnp.max(s, axis=0, keepdims=True))
        alpha = jnp.exp2(m - m_new)
        p = jnp.exp2(s - m_new)
        l = alpha * l + jnp.sum(p, axis=0, keepdims=True)
        acc = alpha * acc + jnp.dot(vt, p.astype(BF16), preferred_element_type=F32)
        return m_new, l, acc

    def step(j, carry):
        off = pl.multiple_of(j * tq, tq)
        ss = []
        for h in range(nh):
            s = jnp.dot(k_ref[pl.ds(off, tq), head_cols(h)], q_ref[head_cols(h), :],
                        preferred_element_type=F32)
            ss += [s[:, :th], s[:, th:]]
        return tuple(softmax_pv(carry[n], ss[n], vt_ref[j, chains[n][0]])
                     for n in range(len(chains)))

    def diagonal_step(carry):
        off = pl.multiple_of(i * tq, tq)
        ss, vts = [], []
        for h, u in chains:
            nk = th if u == 0 else tq
            s = jnp.dot(k_ref[pl.ds(off, nk), head_cols(h)],
                        q_ref[head_cols(h), u * th:(u + 1) * th],
                        preferred_element_type=F32)
            kidx = lax.broadcasted_iota(jnp.int32, (nk, th), 0)
            qidx = lax.broadcasted_iota(jnp.int32, (nk, th), 1) + u * th
            ss.append(jnp.where(kidx <= qidx, s, NEG_BIG))
            vts.append(vt_ref[i, h][:, :nk])
        return tuple(softmax_pv(carry[n], ss[n], vts[n]) for n in range(len(chains)))

    init = tuple((jnp.full((1, th), NEG_BIG, F32), jnp.zeros((1, th), F32),
                  jnp.zeros((MLA_V, th), F32)) for _ in chains)
    carry = lax.fori_loop(0, i, step, init)
    carry = diagonal_step(carry)
    for n, (h, u) in enumerate(chains):
        _, l, acc = carry[n]
        y = (acc / l).T
        rs = slice(u * th, (u + 1) * th)
        cs = slice(h * MLA_V, (h + 1) * MLA_V)
        o_ref[rs, cs] = (_silu(z_ref[rs, cs]) * y).astype(BF16)


def _attention(q_t, k, v_t, proj, batch, seq):
    m = k.shape[0]
    nq = seq // TQ
    nh = ATTN_HEADS_PER_STEP
    v_t5 = v_t.reshape(batch, nq, MLA_HEADS, MLA_V, TQ)
    return pl.pallas_call(
        functools.partial(_attn_kernel, tq=TQ),
        grid=(batch, MLA_HEADS // nh, nq),
        in_specs=[
            pl.BlockSpec((None, nh * MLA_QK_PAD, TQ), lambda b, h, i: (b * nq + i, h, 0)),
            pl.BlockSpec((seq, nh * MLA_QK_PAD), lambda b, h, i: (b, h)),
            pl.BlockSpec((None, nq, nh, MLA_V, TQ), lambda b, h, i: (b, 0, h, 0, 0)),
            pl.BlockSpec((TQ, nh * MLA_V),
                         lambda b, h, i: (b * nq + i, OFF_MLAZ // (nh * MLA_V) + h)),
        ],
        out_specs=pl.BlockSpec((TQ, nh * MLA_V), lambda b, h, i: (b * nq + i, h)),
        out_shape=jax.ShapeDtypeStruct((m, MLA_WIDTH), BF16),
        compiler_params=_params("parallel", "parallel", "arbitrary"),
        name="mla_attn",
    )(q_t, k, v_t5, proj)


def _shift_rows(x, prev8, s):
    rolled = pltpu.roll(x, s, 0)
    prev_rolled = pltpu.roll(prev8, s, 0)
    rid = lax.broadcasted_iota(jnp.int32, (SUBLANES, x.shape[1]), 0)
    top = jnp.where(rid < s, prev_rolled, rolled[0:SUBLANES])
    return jnp.concatenate([top, rolled[SUBLANES:]], axis=0)


def _mlstm_kernel(ib_ref, fb_ref, q_ref, k_ref, v_ref, o_ref, z_ref, g_ref, w_ref, b_ref, ng_ref,
                  out_ref, ct_ref, n_ref, m_ref, qp_ref, kp_ref, *, layer, tt, chunk):
    @pl.when(pl.program_id(1) == 0)
    def _():
        for ref in (ct_ref, n_ref, m_ref, qp_ref, kp_ref):
            ref[...] = jnp.zeros_like(ref)

    hd = ML_HEAD_DIM
    w_all = w_ref[...]
    b_all = b_ref[...]

    def conv_silu(x_ref, prev_ref, w, b):
        x = x_ref[...]
        prev8 = prev_ref[...]
        y = b + w[ML_CONV - 1:ML_CONV] * x
        for s in range(1, ML_CONV):
            y = y + w[ML_CONV - 1 - s:ML_CONV - s] * _shift_rows(x, prev8, s)
        prev_ref[...] = x[tt - SUBLANES:tt]
        return _silu(y)

    q_all = conv_silu(q_ref, qp_ref, w_all[:, :ML_WIDTH], b_all[:, :ML_WIDTH]) * (hd ** -0.5)
    k_all = conv_silu(k_ref, kp_ref, w_all[:, ML_WIDTH:], b_all[:, ML_WIDTH:])
    li_rows = [g_ref[h] + ib_ref[layer, h] for h in range(ML_HEADS)]
    lf_rows = [_log_sigmoid(g_ref[ML_HEADS + h] + fb_ref[layer, h]) for h in range(ML_HEADS)]

    L = chunk
    rows = lax.broadcasted_iota(jnp.int32, (L, L), 0)
    cols = lax.broadcasted_iota(jnp.int32, (L, L), 1)
    tri = rows >= cols
    eye = rows == cols
    nchunk = tt // L
    upper = jnp.where(rows <= cols, 1.0, 0.0).astype(BF16)
    pad_rows = jnp.zeros((SUBLANES - ML_HEADS, L), F32)
    insts = []
    for c in range(nchunk):
        sl = slice(c * L, (c + 1) * L)
        lf8 = jnp.concatenate([lf_rows[h][:, sl] for h in range(ML_HEADS)] + [pad_rows], axis=0)
        b_rows = jnp.zeros((SUBLANES, L), F32)
        rest = lf8
        for _ in range(3):
            piece = rest.astype(BF16)
            b_rows = b_rows + jnp.dot(piece, upper, preferred_element_type=F32)
            rest = rest - piece.astype(F32)
        for h in range(ML_HEADS):
            cs = slice(h * hd, (h + 1) * hd)
            li_row = li_rows[h][:, sl]
            b_row = b_rows[h:h + 1, :]
            b_col = jnp.sum(jnp.where(eye, jnp.broadcast_to(b_row, (L, L)), 0.0),
                            axis=-1, keepdims=True)
            li_col = jnp.sum(jnp.where(eye, jnp.broadcast_to(li_row, (L, L)), 0.0),
                             axis=-1, keepdims=True)
            dmat = jnp.where(tri, b_col - b_row + li_row, NEG_BIG)
            m_loc = jnp.max(dmat, axis=-1, keepdims=True)
            b_last = b_col[L - 1:L, :]
            g_col = b_last - b_col + li_col
            g_max = jnp.max(g_col, axis=0, keepdims=True)
            kc = k_all[sl, cs]
            insts.append(dict(c=c, h=h, sl=sl, cs=cs, qc=q_all[sl, cs].astype(BF16),
                              kc=kc.astype(BF16), vc=v_ref[sl, cs].astype(BF16),
                              pm=jnp.exp(dmat - m_loc), m_loc=m_loc, b_col=b_col, b_last=b_last,
                              g_max=g_max, kwg=kc * jnp.exp(g_col - g_max)))
    for s in insts:
        s["smat"] = lax.dot_general(s["qc"], s["kc"], _NT, preferred_element_type=F32) * s["pm"]
    for s in insts:
        s["sv"] = jnp.dot(s["smat"].astype(BF16), s["vc"], preferred_element_type=F32)
        s["ssum"] = jnp.sum(s["smat"], axis=-1, keepdims=True)
        s["kv"] = jnp.dot(s["kwg"].T.astype(BF16), s["vc"], preferred_element_type=F32)
        s["ksum"] = jnp.sum(s["kwg"], axis=0, keepdims=True)

    for c in range(nchunk):
        for s in insts[c * ML_HEADS:(c + 1) * ML_HEADS]:
            h = s["h"]
            ct = ct_ref[h]
            n_row = n_ref[h]
            m_prev = m_ref[h][:, 0:1]
            m_inter = s["b_col"] + m_prev
            m_t = jnp.maximum(m_inter, s["m_loc"])
            inter = jnp.exp(m_inter - m_t)
            loc = jnp.exp(s["m_loc"] - m_t)
            num = (inter * jnp.dot(s["qc"], ct.astype(BF16), preferred_element_type=F32)
                   + loc * s["sv"])
            den = (inter * jnp.sum(s["qc"].astype(F32) * n_row, axis=-1, keepdims=True)
                   + loc * s["ssum"])
            s["hh"] = num / jnp.maximum(jnp.abs(den), jnp.exp(-m_t))
            m_new = jnp.maximum(s["b_last"] + m_prev, s["g_max"])
            decay = jnp.exp(s["b_last"] + m_prev - m_new)
            scale = jnp.exp(s["g_max"] - m_new)
            ct_ref[h] = decay * ct + scale * s["kv"]
            n_ref[h] = decay * n_row + scale * s["ksum"]
            m_ref[h] = jnp.broadcast_to(m_new, (1, hd))
    for s in insts:
        sl, cs = s["sl"], s["cs"]
        hh = s["hh"]
        mu = jnp.mean(hh, axis=-1, keepdims=True)
        dd = hh - mu
        var = jnp.mean(dd * dd, axis=-1, keepdims=True)
        yn = dd * lax.rsqrt(var + NORM_EPS) * ng_ref[:, cs]
        out_ref[sl, cs] = (_silu(z_ref[sl, cs]) * (_sigmoid(o_ref[sl, cs]) * yn)).astype(BF16)


def _mlstm(proj, gates_t, i_bias, f_bias, conv_w, conv_b3, norm_g3, layer, batch, seq):
    m = proj.shape[0]
    tt = min(ML_TT, seq)
    nt = seq // tt
    hd = ML_HEAD_DIM
    w = ML_WIDTH

    def col(off):
        return pl.BlockSpec((tt, w), lambda b, t: (b * nt + t, off // w))

    smem = pl.BlockSpec(memory_space=pltpu.SMEM)
    return pl.pallas_call(
        functools.partial(_mlstm_kernel, layer=layer, tt=tt, chunk=min(ML_L, tt)),
        grid=(batch, nt),
        in_specs=[
            smem, smem,
            col(OFF_QK), col(OFF_QK + w), col(OFF_MLV), col(OFF_MLO), col(OFF_MLZ),
            pl.BlockSpec((None, 2 * ML_HEADS, 1, tt), lambda b, t: (b, 0, 0, t)),
            pl.BlockSpec((None, ML_CONV, 2 * w), lambda b, t: (layer, 0, 0)),
            pl.BlockSpec((None, 1, 2 * w), lambda b, t: (layer, 0, 0)),
            pl.BlockSpec((None, 1, w), lambda b, t: (layer, 0, 0)),
        ],
        out_specs=pl.BlockSpec((tt, w), lambda b, t: (b * nt + t, 0)),
        out_shape=jax.ShapeDtypeStruct((m, w), BF16),
        scratch_shapes=[
            pltpu.VMEM((ML_HEADS, hd, hd), F32),
            pltpu.VMEM((ML_HEADS, 1, hd), F32),
            pltpu.VMEM((ML_HEADS, 1, hd), F32),
            pltpu.VMEM((SUBLANES, w), F32),
            pltpu.VMEM((SUBLANES, w), F32),
        ],
        compiler_params=_params("parallel", "arbitrary"),
        name="mlstm",
    )(i_bias, f_bias, proj, proj, proj, proj, proj, gates_t, conv_w, conv_b3, norm_g3)


RW_PRE_PASSES = 3


def _rwkv_pre_kernel(*refs, has_vres, tm):
    if has_vres:
        (wa_ref, r_ref, k_ref, v_ref, mu_wa, mu_r, mu_k, mu_v, w0_ref, w2_ref, a0_ref, a2_ref,
         kk_ref, ka_ref, e_ref, vf_ref, v0_ref, v1_ref, v2_ref,
         r_out, ld_out, kh_out, v_out, kn_out, b_out, c_wa, c_r, c_k, c_v) = refs
    else:
        (wa_ref, r_ref, k_ref, v_ref, mu_wa, mu_r, mu_k, mu_v, w0_ref, w2_ref, a0_ref, a2_ref,
         kk_ref, ka_ref, e_ref,
         r_out, ld_out, kh_out, v_out, kn_out, b_out, c_wa, c_r, c_k, c_v) = refs

    @pl.when(pl.program_id(1) == 0)
    def _():
        for c in (c_wa, c_r, c_k, c_v):
            c[...] = jnp.zeros_like(c)

    def mix(x_ref, c_ref, mu_ref):
        x = x_ref[...]
        rolled = pltpu.roll(x, 1, 0)
        rid = lax.broadcasted_iota(jnp.int32, (SUBLANES, x.shape[1]), 0)
        prev_last = jnp.broadcast_to(c_ref[SUBLANES - 1:SUBLANES, :], (SUBLANES, x.shape[1]))
        top = jnp.where(rid == 0, prev_last, rolled[0:SUBLANES])
        xprev = jnp.concatenate([top, rolled[SUBLANES:]], axis=0)
        c_ref[...] = x[tm - SUBLANES:tm]
        return x + mu_ref[...] * (xprev - x)

    xwa = mix(wa_ref, c_wa, mu_wa)
    r = mix(r_ref, c_r, mu_r)
    k = mix(k_ref, c_k, mu_k)
    v = mix(v_ref, c_v, mu_v)
    zw = w0_ref[...] + _mm(jnp.tanh(xwa), w2_ref[...], RW_PRE_PASSES)
    za = a0_ref[...] + _mm(xwa, a2_ref[...], RW_PRE_PASSES)
    log_w = _log_sigmoid(zw) - 0.5
    ld_out[...] = -jnp.exp(log_w)
    a = _sigmoid(za)
    if has_vres:
        gate = _sigmoid(v0_ref[...] + _mm(_mm(v, v1_ref[...], RW_PRE_PASSES), v2_ref[...],
                                          RW_PRE_PASSES))
        v = v + (vf_ref[...] - v) * gate
    kk = k * kk_ref[...]
    ss = _mm(kk * kk, e_ref[...])
    kn = kk / jnp.maximum(jnp.sqrt(ss), 1e-12)
    r_out[...] = r
    kh_out[...] = k * (1.0 + (a - 1.0) * ka_ref[...])
    v_out[...] = v
    kn_out[...] = kn
    b_out[...] = kn * a


RW_GROUP = 4


def _rwkv_scan_kernel(r_ref, ld_ref, kh_ref, v_ref, kn_ref, b_ref, z_ref, rk_ref, g_ref, bias_ref,
                      e_ref, out_ref, m_ref, ac_ref, gc_ref, qs_ref, y_ref, *, tt, chunk):
    @pl.when(pl.program_id(1) == 0)
    def _():
        m_ref[...] = jnp.zeros_like(m_ref)

    L = chunk
    L2 = 2 * L
    nchunk = tt // L
    npair = RW_WIDTH // LANES
    rows = lax.broadcasted_iota(jnp.int32, (L, L), 0)
    cols = lax.broadcasted_iota(jnp.int32, (L, L), 1)
    tri_b = jnp.where(rows >= cols, 1.0, 0.0).astype(BF16)
    lane = lax.broadcasted_iota(jnp.int32, (1, LANES), 1)
    m0 = jnp.where(lane < RW_HEAD_DIM, 1.0, 0.0)
    m1 = 1.0 - m0
    r2 = lax.broadcasted_iota(jnp.int32, (L2, L2), 0)
    c2 = lax.broadcasted_iota(jnp.int32, (L2, L2), 1)
    same_blk = (r2 < L) == (c2 < L)
    t2 = jnp.where(r2 < L, r2, r2 - L)
    s2 = jnp.where(c2 < L, c2, c2 - L)
    mask_strict = jnp.logical_and(same_blk, t2 > s2)
    mask_incl = jnp.logical_and(same_blk, t2 >= s2)
    eye2 = lax.broadcasted_iota(jnp.int32, (LANES, LANES), 0) == lax.broadcasted_iota(
        jnp.int32, (LANES, LANES), 1)
    n_double = max(1, (L - 1).bit_length())

    def stack(x):
        return jnp.concatenate([x * m0, x * m1], axis=0)

    def prepare(p, c):
        sl = slice(c * L, (c + 1) * L)
        cs = slice(p * LANES, (p + 1) * LANES)
        ld = ld_ref[sl, cs]
        r, kh, v, kn, b = r_ref[sl, cs], kh_ref[sl, cs], v_ref[sl, cs], kn_ref[sl, cs], b_ref[sl, cs]
        ld_hi, ld_lo = _split_bf16(ld)
        logp = (jnp.dot(tri_b, ld_hi, preferred_element_type=F32)
                + jnp.dot(tri_b, ld_lo, preferred_element_type=F32))
        cmid = logp[L // 2 - 1:L // 2, :]
        e = logp - cmid
        e_last = e[L - 1:L, :]
        p_mid = jnp.exp(cmid)
        at_s = stack(-kn * jnp.exp(e - ld))
        rt_s = stack(r * jnp.exp(e))
        inv = jnp.exp(-e)
        tail = jnp.exp(e_last - e)
        lhs = jnp.concatenate([at_s, rt_s], axis=0).astype(BF16)
        rhs = jnp.concatenate([stack(b * inv), stack(kh * inv)], axis=0).astype(BF16)
        hat_t = jnp.concatenate([stack(b * tail), stack(kh * tail)], axis=0).T.astype(BF16)
        return dict(sl=sl, cs=cs, idx=p * nchunk + c, lhs=lhs, rhs=rhs, hat_t=hat_t,
                    at_true=at_s * p_mid, rt_true=rt_s * p_mid, v_s=stack(v).astype(BF16),
                    p_last=jnp.exp(e_last + cmid))

    def coeff_stages(cg):
        insts = [prepare(p, cg * RW_GROUP + g) for p in range(npair) for g in range(RW_GROUP)]
        for s in insts:
            aa = lax.dot_general(s["lhs"], s["rhs"], _NT, preferred_element_type=F32)
            s["nmat"] = jnp.where(mask_strict, aa[0:L2, 0:L2], 0.0).astype(BF16)
            s["a_ak"] = jnp.where(mask_strict, aa[0:L2, L2:], 0.0).astype(BF16)
            s["a_r"] = jnp.concatenate([jnp.where(mask_incl, aa[L2:, 0:L2], 0.0),
                                        jnp.where(mask_incl, aa[L2:, L2:], 0.0)],
                                       axis=1).astype(BF16)
        yield
        for s in insts:
            s["x"] = jnp.concatenate(
                [s["at_true"], jnp.dot(s["a_ak"], s["v_s"], preferred_element_type=F32)],
                axis=1)
        yield
        for it in range(n_double):
            for s in insts:
                s["x"] = s["x"] + jnp.dot(s["nmat"], s["x"].astype(BF16),
                                          preferred_element_type=F32)
            if it + 1 < n_double:
                for s in insts:
                    s["nmat"] = jnp.dot(s["nmat"], s["nmat"],
                                        preferred_element_type=F32).astype(BF16)
            yield
        for s in insts:
            big_l = jnp.concatenate([s["a_r"], s["hat_t"]], axis=0)
            big_r = jnp.concatenate(
                [s["x"].astype(BF16),
                 jnp.concatenate([jnp.zeros((L2, LANES), BF16), s["v_s"]], axis=1)], axis=0)
            res = jnp.dot(big_l, big_r, preferred_element_type=F32)
            q_s = s["rt_true"] + res[0:L2, :LANES]
            y0_s = res[0:L2, LANES:]
            qs_ref[s["sl"], s["cs"]] = q_s[0:L] + q_s[L:]
            y_ref[s["sl"], s["cs"]] = y0_s[0:L] + y0_s[L:]
            ac_ref[s["idx"]] = (jnp.where(eye2, jnp.broadcast_to(s["p_last"], (LANES, LANES)), 0.0)
                                + res[L2:, :LANES])
            gc_ref[s["idx"]] = res[L2:, LANES:]

    def chain_step(c):
        sl = slice(c * L, (c + 1) * L)
        for p in range(npair):
            cs = slice(p * LANES, (p + 1) * LANES)
            idx = p * nchunk + c
            mp = m_ref[p]
            y_ref[sl, cs] = _mm(qs_ref[sl, cs], mp) + y_ref[sl, cs]
            m_ref[p] = _mm(ac_ref[idx], mp) + gc_ref[idx]

    pending = []
    for cg in range(nchunk // RW_GROUP):
        for _ in coeff_stages(cg):
            if pending:
                chain_step(pending.pop(0))
        for c in pending:
            chain_step(c)
        pending = [cg * RW_GROUP + g for g in range(RW_GROUP)]
    for c in pending:
        chain_step(c)

    e_b = e_ref[...]
    inv_n = 1.0 / RW_HEAD_DIM

    def head_sum(x):
        return jnp.dot(x.astype(BF16), e_b, preferred_element_type=F32)

    for p in range(npair):
        cs = slice(p * LANES, (p + 1) * LANES)
        y = y_ref[:, cs]
        mu = head_sum(y) * inv_n
        d = y - mu
        var = head_sum(d * d) * inv_n
        yn = d * lax.rsqrt(var + RW_GN_EPS) * g_ref[:, cs] + bias_ref[:, cs]
        v_all = v_ref[:, cs]
        bonus = head_sum(r_ref[:, cs] * kh_ref[:, cs] * rk_ref[:, cs])
        out_ref[:, cs] = (_silu(z_ref[:, cs]) * (yn + bonus * v_all)).astype(BF16)


def _rwkv_kernel(*refs, has_vres, tt, chunk):
    n_mix = 19 if has_vres else 15
    mix_in = refs[:n_mix]
    z_ref, rk_ref, g_ref, bias_ref, e_ref = refs[n_mix:n_mix + 5]
    n_out = 1 if has_vres else 2
    outs = refs[n_mix + 5:n_mix + 5 + n_out]
    (c_wa, c_r, c_k, c_v, r_s, ld_s, kh_s, v_s, kn_s, b_s,
     m_ref, ac_ref, gc_ref, qs_ref, y_ref) = refs[n_mix + 5 + n_out:]
    _rwkv_pre_kernel(*mix_in, r_s, ld_s, kh_s, v_s, kn_s, b_s, c_wa, c_r, c_k, c_v,
                     has_vres=has_vres, tm=tt)
    if not has_vres:
        outs[1][...] = v_s[...]
    _rwkv_scan_kernel(r_s, ld_s, kh_s, v_s, kn_s, b_s, z_ref, rk_ref, g_ref, bias_ref, e_ref,
                      outs[0], m_ref, ac_ref, gc_ref, qs_ref, y_ref, tt=tt, chunk=chunk)


def _rwkv(proj_a, proj_b, v_first, mu_wa, mu_r, mu_k, mu_v, w0, w2p, a0, a2p, k_k, k_a, e_head,
          v0, v1p, v2p, r_k3, ln_g3, ln_b3, e_pair, layer, batch, seq):
    m = proj_a.shape[0]
    tt = min(RW_TT, seq)
    nt = seq // tt
    chunk = min(RW_L, tt)
    has_vres = layer > 0
    w = RW_WIDTH
    npair = w // LANES

    def colblk(off, width):
        return pl.BlockSpec((tt, width), lambda b, t: (b * nt + t, off // width))

    def lay(shape):
        return pl.BlockSpec((None,) + shape, lambda b, t: (layer,) + (0,) * len(shape))

    def lay1(shape):
        return pl.BlockSpec((None,) + shape, lambda b, t: (layer - 1,) + (0,) * len(shape))

    row = pl.BlockSpec((tt, w), lambda b, t: (b * nt + t, 0))
    in_specs = [colblk(OFF_WA, LANES), colblk(OFF_RWR, w), colblk(OFF_RWK, w), colblk(OFF_RWV, w),
                lay((1, LANES)), lay((1, w)), lay((1, w)), lay((1, w)),
                lay((1, w)), lay((LANES, w)), lay((1, w)), lay((LANES, w)),
                lay((1, w)), lay((1, w)),
                pl.BlockSpec((w, w), lambda b, t: (0, 0))]
    args = [proj_a, proj_b, proj_b, proj_b, mu_wa, mu_r, mu_k, mu_v, w0, w2p, a0, a2p, k_k, k_a, e_head]
    if has_vres:
        in_specs += [row, lay1((1, w)), lay1((w, LANES)), lay1((LANES, w))]
        args += [v_first, v0, v1p, v2p]
    in_specs += [colblk(OFF_RWZ, w), lay((1, w)), lay((1, w)), lay((1, w)),
                 pl.BlockSpec((LANES, LANES), lambda b, t: (0, 0))]
    args += [proj_b, r_k3, ln_g3, ln_b3, e_pair]
    out_specs = [row]
    out_shape = [jax.ShapeDtypeStruct((m, w), BF16)]
    if not has_vres:
        out_specs.append(row)
        out_shape.append(jax.ShapeDtypeStruct((m, w), F32))
    scratch = [pltpu.VMEM((SUBLANES, LANES), F32)] + [pltpu.VMEM((SUBLANES, w), F32)] * 3
    scratch += [pltpu.VMEM((tt, w), F32)] * 6
    scratch += [
        pltpu.VMEM((npair, LANES, LANES), F32),
        pltpu.VMEM((npair * (tt // chunk), LANES, LANES), F32),
        pltpu.VMEM((npair * (tt // chunk), LANES, LANES), F32),
        pltpu.VMEM((tt, w), F32),
        pltpu.VMEM((tt, w), F32),
    ]
    return pl.pallas_call(
        functools.partial(_rwkv_kernel, has_vres=has_vres, tt=tt, chunk=chunk),
        grid=(batch, nt),
        in_specs=in_specs,
        out_specs=out_specs,
        out_shape=out_shape,
        scratch_shapes=scratch,
        compiler_params=_params("parallel", "arbitrary"),
        name="rwkv",
    )(*args)


_W_IN_PARTS = ((0, 1024), (1024, 512), (1544, 512), (2056, 512), (2568, 512), (3080, 256),
               (3336, 64), (1536, 8), (None, LANES - MLA_ROPE - 2 * ML_HEADS), (5960, 128),
               (4424, 512), (4936, 512), (5448, 512), (6088, 512), (3400, 1024))
TK_PREP = 256


def _permute_w_in_kernel(src_ref, dst_a, dst_b):
    tk = src_ref.shape[1]
    off = 0
    pending = []

    def put(off, width, value):
        if off < D_IN_A:
            dst_a[:, off:off + width] = value
        else:
            dst_b[:, off - D_IN_A:off - D_IN_A + width] = value

    for start, width in _W_IN_PARTS:
        if start is None:
            pending.append(jnp.zeros((width, tk), F32))
        elif width % LANES:
            pending.append(src_ref[start:start + width, :])
        else:
            put(off, width, src_ref[start:start + width, :].T.astype(BF16))
            off += width
            continue
        rows = sum(p.shape[0] for p in pending)
        if rows == LANES:
            put(off, LANES, jnp.concatenate(pending, axis=0).T.astype(BF16))
            off += LANES
            pending = []


def _permute_w_in(w_in):
    depth, d_model, d_in = w_in.shape
    w_t = jnp.swapaxes(w_in, 1, 2)
    return pl.pallas_call(
        _permute_w_in_kernel,
        grid=(depth, d_model // TK_PREP),
        in_specs=[pl.BlockSpec((None, d_in, TK_PREP), lambda l, i: (l, 0, i))],
        out_specs=[pl.BlockSpec((None, TK_PREP, D_IN_A), lambda l, i: (l, i, 0)),
                   pl.BlockSpec((None, TK_PREP, D_IN_B), lambda l, i: (l, i, 0))],
        out_shape=[jax.ShapeDtypeStruct((depth, d_model, D_IN_A), BF16),
                   jax.ShapeDtypeStruct((depth, d_model, D_IN_B), BF16)],
        compiler_params=_params("parallel", "parallel"),
        name="permute_w_in",
    )(w_t)


def _permute_w_uq(w_uq):
    d = w_uq.shape[0]
    w = w_uq.reshape(d, MLA_Q_RANK, MLA_HEADS, MLA_NOPE + MLA_ROPE)
    w = jnp.pad(w, ((0, 0), (0, 0), (0, 0), (0, MLA_QK_PAD - MLA_NOPE - MLA_ROPE)))
    return w.reshape(d, MLA_Q_RANK, MLA_HEADS * MLA_QK_PAD).astype(BF16)


def _split_w_ukv(w_ukv):
    d = w_ukv.shape[0]
    w = w_ukv.reshape(d, MLA_KV_RANK, MLA_HEADS, MLA_NOPE + MLA_V)
    wk = w[..., :MLA_NOPE].reshape(d, MLA_KV_RANK, MLA_HEADS * MLA_NOPE)
    wv = w[..., MLA_NOPE:].reshape(d, MLA_KV_RANK, MLA_HEADS * MLA_V)
    return wk.astype(BF16), wv.astype(BF16)


def _rope_tables(positions):
    inv_freq = jnp.power(ROPE_BASE, -jnp.arange(0, MLA_ROPE, 2, dtype=F32) / MLA_ROPE)
    ang = positions.astype(F32).reshape(-1, 1) * inv_freq
    cos, sin = jnp.cos(ang), jnp.sin(ang)
    z32 = jnp.zeros_like(cos)
    z64 = jnp.zeros((ang.shape[0], LANES - MLA_ROPE), F32)
    cosf = jnp.concatenate([cos, cos, z64], axis=-1)
    s1 = jnp.concatenate([-sin, z32, z64], axis=-1)
    s2 = jnp.concatenate([z32, sin, z64], axis=-1)
    return cosf, s1, s2


def kernel(x, positions, norm_g, w_in, ml_conv_w, ml_conv_b, ml_i_bias, ml_f_bias, ml_norm_g,
           mla_q_norm_g, mla_w_uq, mla_kv_norm_g, mla_w_ukv, rw_mu, rw_w0, rw_w2, rw_a0, rw_a2,
           rw_v0, rw_v1, rw_v2, rw_k_k, rw_k_a, rw_r_k, rw_ln_g, rw_ln_b, w_out, final_norm_g):
    batch, seq, _ = x.shape
    depth = w_in.shape[0]
    m = batch * seq
    x2 = x.reshape(m, D_MODEL)

    w_in_a, w_in_b = _permute_w_in(w_in)
    w_uq_p = _permute_w_uq(mla_w_uq)
    w_uk_p, w_uv_p = _split_w_ukv(mla_w_ukv)
    w_out_b = w_out.astype(BF16)
    cosf, s1, s2 = _rope_tables(positions)

    def row3(a):
        return a.reshape(a.shape[0], 1, a.shape[1])

    norm_g3 = row3(norm_g)
    conv_b3 = row3(ml_conv_b)
    ml_norm_g3 = row3(ml_norm_g)
    qg3, kvg3 = row3(mla_q_norm_g), row3(mla_kv_norm_g)
    w = RW_WIDTH
    mu_r, mu_k, mu_v = row3(rw_mu[:, 0:w]), row3(rw_mu[:, w:2 * w]), row3(rw_mu[:, 2 * w:3 * w])
    mu_wa = row3(rw_mu[:, 3 * w:])
    w0, a0, k_k, k_a = row3(rw_w0), row3(rw_a0), row3(rw_k_k), row3(rw_k_a)
    w2p = jnp.pad(rw_w2, ((0, 0), (0, LANES - RW_DECAY_RANK), (0, 0)))
    a2p = jnp.pad(rw_a2, ((0, 0), (LANES - RW_AAA_RANK, 0), (0, 0)))
    v0 = row3(rw_v0)
    v1p = jnp.pad(rw_v1, ((0, 0), (0, 0), (0, LANES - RW_MV_RANK)))
    v2p = jnp.pad(rw_v2, ((0, 0), (0, LANES - RW_MV_RANK), (0, 0)))
    r_k3 = rw_r_k.reshape(depth, 1, w)
    ln_g3, ln_b3 = row3(rw_ln_g), row3(rw_ln_b)
    hid = jnp.arange(w) // RW_HEAD_DIM
    e_head = (hid[:, None] == hid[None, :]).astype(F32)
    e_pair = e_head[:LANES, :LANES].astype(BF16)
    final_g3 = final_norm_g.reshape(1, 1, D_MODEL)

    v_first = None
    h = _rmsnorm(x2, norm_g3, 0)
    for layer in range(depth):
        proj_a = _inproj(h, w_in_a, layer, TN_PROJ_A)
        proj_b = _inproj(h, w_in_b, layer, TN_PROJ_B)
        gates_t = proj_a[:, OFF_KRG + MLA_ROPE:OFF_KRG + MLA_ROPE + 2 * ML_HEADS]
        gates_t = gates_t.reshape(batch, seq, 2 * ML_HEADS).transpose(0, 2, 1)
        gates_t = gates_t.reshape(batch, 2 * ML_HEADS, 1, seq)
        y_ml = _mlstm(proj_a, gates_t, ml_i_bias, ml_f_bias, ml_conv_w, conv_b3, ml_norm_g3,
                      layer, batch, seq)
        q, k, v = _mla_proj(proj_a, cosf, s1, s2, qg3, w_uq_p, kvg3, w_uk_p, w_uv_p, layer)
        y_mla = _attention(q, k, v, proj_b, batch, seq)
        rw_out = _rwkv(proj_a, proj_b, v_first, mu_wa, mu_r, mu_k, mu_v, w0, w2p, a0, a2p, k_k, k_a, e_head,
                       v0, v1p, v2p, r_k3, ln_g3, ln_b3, e_pair, layer, batch, seq)
        y_rw = rw_out[0]
        if layer == 0:
            v_first = rw_out[1]
        if layer == depth - 1:
            (x2,) = _outproj(x2, y_ml, y_mla, y_rw, w_out_b, final_g3, layer, True)
        else:
            x2, h = _outproj(x2, y_ml, y_mla, y_rw, w_out_b, norm_g3, layer, False)
    return x2.reshape(batch, seq, D_MODEL)
```

```python
import functools

import jax
import jax.numpy as jnp
from jax import lax
from jax.experimental import pallas as pl
from jax.experimental.pallas import tpu as pltpu

F32 = jnp.float32
BF16 = jnp.bfloat16

D_MODEL = 2048
NORM_EPS = 1e-6
ML_HEADS = 4
ML_HEAD_DIM = 128
ML_WIDTH = 512
ML_CONV = 4
MLA_HEADS = 8
MLA_NOPE = 128
MLA_ROPE = 64
MLA_V = 128
MLA_WIDTH = 1024
MLA_Q_RANK = 512
MLA_KV_RANK = 256
ROPE_BASE = 10000.0
RW_HEAD_DIM = 64
RW_WIDTH = 512
RW_HEADS = 8
RW_DECAY_RANK = 64
RW_AAA_RANK = 64
RW_MV_RANK = 32
RW_GN_EPS = 64e-5

LANES = 128
SUBLANES = 8
MLA_QK_PAD = 256
VMEM_LIMIT = 48 * 1024 * 1024
NEG_BIG = -1e30
LOG2E = 1.4426950408889634

OFF_QK = 0
OFF_MLV = 1024
OFF_MLO = 1536
OFF_MLZ = 2048
OFF_CQ = 2560
OFF_CKV = 3072
OFF_KRG = 3328
OFF_WA = 3456
D_IN_A = 3584
OFF_RWR = 0
OFF_RWK = 512
OFF_RWV = 1024
OFF_RWZ = 1536
OFF_MLAZ = 2048
D_IN_B = 3072

TM_PROJ = 1024
TN_PROJ_A = 1792
TN_PROJ_B = 1536
TM_OUT = 512
TQ = 512
ML_TT = 1024
ML_L = 256
RW_TT = 512
RW_L = 64


def _split_bf16(x):
    hi = x.astype(BF16)
    lo = (x - hi.astype(F32)).astype(BF16)
    return hi, lo


def _mm(a, b, passes=1):
    if passes == 1:
        return jnp.dot(a.astype(BF16), b.astype(BF16), preferred_element_type=F32)
    ah, al = _split_bf16(a)
    bh, bl = _split_bf16(b)
    return (jnp.dot(ah, bh, preferred_element_type=F32)
            + jnp.dot(ah, bl, preferred_element_type=F32)
            + jnp.dot(al, bh, preferred_element_type=F32))


_NT = (((1,), (1,)), ((), ()))


def _sigmoid(x):
    return 0.5 * jnp.tanh(0.5 * x) + 0.5


def _silu(x):
    h = 0.5 * x
    return h * jnp.tanh(h) + h


def _log_sigmoid(x):
    return jnp.minimum(x, 0.0) - jnp.log(1.0 + jnp.exp(-jnp.abs(x)))


def _params(*sem):
    return pltpu.CompilerParams(dimension_semantics=sem, vmem_limit_bytes=VMEM_LIMIT)


def _rms_scale(x, g):
    return x * lax.rsqrt(jnp.mean(x * x, axis=-1, keepdims=True) + NORM_EPS) * g


def _rmsnorm_kernel(x_ref, g_ref, h_ref):
    h_ref[...] = _rms_scale(x_ref[...], g_ref[...]).astype(BF16)


def _rmsnorm(x2, norm_g3, layer):
    m = x2.shape[0]
    return pl.pallas_call(
        _rmsnorm_kernel,
        grid=(m // TM_OUT,),
        in_specs=[pl.BlockSpec((TM_OUT, D_MODEL), lambda i: (i, 0)),
                  pl.BlockSpec((None, 1, D_MODEL), lambda i: (layer, 0, 0))],
        out_specs=pl.BlockSpec((TM_OUT, D_MODEL), lambda i: (i, 0)),
        out_shape=jax.ShapeDtypeStruct((m, D_MODEL), BF16),
        compiler_params=_params("parallel"),
        name="rmsnorm",
    )(x2, norm_g3)


def _inproj_kernel(h_ref, w_ref, o_ref):
    o_ref[...] = jnp.dot(h_ref[...], w_ref[...], preferred_element_type=F32)


def _inproj(h, w_in_p, layer, tn):
    m = h.shape[0]
    d_out = w_in_p.shape[2]
    return pl.pallas_call(
        _inproj_kernel,
        grid=(m // TM_PROJ, d_out // tn),
        in_specs=[
            pl.BlockSpec((TM_PROJ, D_MODEL), lambda i, j: (i, 0)),
            pl.BlockSpec((None, D_MODEL, tn), lambda i, j: (layer, 0, j)),
        ],
        out_specs=pl.BlockSpec((TM_PROJ, tn), lambda i, j: (i, j)),
        out_shape=jax.ShapeDtypeStruct((m, d_out), F32),
        compiler_params=_params("parallel", "arbitrary"),
        name="inproj",
    )(h, w_in_p)


def _outproj_kernel(x_ref, yml_ref, ymla_ref, yrw_ref, w_ref, g_ref, *out_refs, final):
    acc = x_ref[...]
    acc = acc + jnp.dot(yml_ref[...], w_ref[0:ML_WIDTH, :], preferred_element_type=F32)
    acc = acc + jnp.dot(ymla_ref[...], w_ref[ML_WIDTH:ML_WIDTH + MLA_WIDTH, :],
                        preferred_element_type=F32)
    acc = acc + jnp.dot(yrw_ref[...], w_ref[ML_WIDTH + MLA_WIDTH:, :], preferred_element_type=F32)
    normed = _rms_scale(acc, g_ref[...])
    if final:
        out_refs[0][...] = normed
    else:
        out_refs[0][...] = acc
        out_refs[1][...] = normed.astype(BF16)


def _outproj(x2, y_ml, y_mla, y_rw, w_out_b, gains3, layer, final):
    m = x2.shape[0]
    g_index = 0 if final else layer + 1
    row = pl.BlockSpec((TM_OUT, D_MODEL), lambda i: (i, 0))
    out_specs = [row] if final else [row, row]
    out_shape = [jax.ShapeDtypeStruct((m, D_MODEL), F32)]
    if not final:
        out_shape.append(jax.ShapeDtypeStruct((m, D_MODEL), BF16))
    return pl.pallas_call(
        functools.partial(_outproj_kernel, final=final),
        grid=(m // TM_OUT,),
        in_specs=[
            row,
            pl.BlockSpec((TM_OUT, ML_WIDTH), lambda i: (i, 0)),
            pl.BlockSpec((TM_OUT, MLA_WIDTH), lambda i: (i, 0)),
            pl.BlockSpec((TM_OUT, RW_WIDTH), lambda i: (i, 0)),
            pl.BlockSpec((None, D_MODEL, D_MODEL), lambda i: (layer, 0, 0)),
            pl.BlockSpec((None, 1, D_MODEL), lambda i: (g_index, 0, 0)),
        ],
        out_specs=out_specs,
        out_shape=out_shape,
        compiler_params=_params("parallel"),
        name="outproj",
    )(x2, y_ml, y_mla, y_rw, w_out_b, gains3)


def _mla_proj_kernel(cq_ref, ckv_ref, krg_ref, cos_ref, s1_ref, s2_ref, qg_ref, wq_ref, kvg_ref,
                     wk_ref, wv_ref, q_out, k_out, v_out):
    cq = cq_ref[...]
    cqn = (cq * lax.rsqrt(jnp.mean(cq * cq, axis=-1, keepdims=True) + NORM_EPS)
           * qg_ref[...]).astype(BF16)
    q = jnp.dot(cqn, wq_ref[...], preferred_element_type=F32)
    ckv = ckv_ref[...]
    ckvn = (ckv * lax.rsqrt(jnp.mean(ckv * ckv, axis=-1, keepdims=True) + NORM_EPS)
            * kvg_ref[...]).astype(BF16)
    kn = jnp.dot(ckvn, wk_ref[...], preferred_element_type=F32)
    v = jnp.dot(ckvn, wv_ref[...], preferred_element_type=F32)
    cosf, s1, s2 = cos_ref[...], s1_ref[...], s2_ref[...]

    def rope(xb):
        return (xb * cosf + pltpu.roll(xb, LANES - MLA_ROPE // 2, 1) * s1
                + pltpu.roll(xb, MLA_ROPE // 2, 1) * s2)

    scale = (MLA_NOPE + MLA_ROPE) ** -0.5 * LOG2E
    kr = rope(krg_ref[...]).astype(BF16)
    q_parts = []
    for h in range(MLA_HEADS):
        c0 = h * MLA_QK_PAD
        q_parts += [q[:, c0:c0 + MLA_NOPE], rope(q[:, c0 + MLA_NOPE:c0 + MLA_QK_PAD])]
        k_out[:, c0:c0 + MLA_NOPE] = kn[:, h * MLA_NOPE:(h + 1) * MLA_NOPE].astype(BF16)
        k_out[:, c0 + MLA_NOPE:c0 + MLA_QK_PAD] = kr
    q_out[...] = (jnp.concatenate(q_parts, axis=1) * scale).astype(BF16).T
    v_out[...] = v.astype(BF16).T


def _mla_proj(proj, cosf, s1, s2, qg3, wq_p, kvg3, wk_p, wv_p, layer):
    m = proj.shape[0]
    tm = TQ
    row = lambda i: (i, 0)
    return pl.pallas_call(
        _mla_proj_kernel,
        grid=(m // tm,),
        in_specs=[
            pl.BlockSpec((tm, MLA_Q_RANK), lambda i: (i, OFF_CQ // MLA_Q_RANK)),
            pl.BlockSpec((tm, MLA_KV_RANK), lambda i: (i, OFF_CKV // MLA_KV_RANK)),
            pl.BlockSpec((tm, LANES), lambda i: (i, OFF_KRG // LANES)),
            pl.BlockSpec((tm, LANES), row),
            pl.BlockSpec((tm, LANES), row),
            pl.BlockSpec((tm, LANES), row),
            pl.BlockSpec((None, 1, MLA_Q_RANK), lambda i: (layer, 0, 0)),
            pl.BlockSpec((None, MLA_Q_RANK, MLA_HEADS * MLA_QK_PAD), lambda i: (layer, 0, 0)),
            pl.BlockSpec((None, 1, MLA_KV_RANK), lambda i: (layer, 0, 0)),
            pl.BlockSpec((None, MLA_KV_RANK, MLA_HEADS * MLA_NOPE), lambda i: (layer, 0, 0)),
            pl.BlockSpec((None, MLA_KV_RANK, MLA_HEADS * MLA_V), lambda i: (layer, 0, 0)),
        ],
        out_specs=[
            pl.BlockSpec((None, MLA_HEADS * MLA_QK_PAD, tm), lambda i: (i, 0, 0)),
            pl.BlockSpec((tm, MLA_HEADS * MLA_QK_PAD), row),
            pl.BlockSpec((None, MLA_HEADS * MLA_V, tm), lambda i: (i, 0, 0)),
        ],
        out_shape=[
            jax.ShapeDtypeStruct((m // tm, MLA_HEADS * MLA_QK_PAD, tm), BF16),
            jax.ShapeDtypeStruct((m, MLA_HEADS * MLA_QK_PAD), BF16),
            jax.ShapeDtypeStruct((m // tm, MLA_HEADS * MLA_V, tm), BF16),
        ],
        compiler_params=_params("parallel"),
        name="mla_proj",
    )(proj, proj, proj, cosf, s1, s2, qg3, wq_p, kvg3, wk_p, wv_p)


ATTN_HEADS_PER_STEP = 4


def _attn_kernel(q_ref, k_ref, vt_ref, z_ref, o_ref, *, tq):
    i = pl.program_id(2)
    nh = ATTN_HEADS_PER_STEP
    th = tq // 2
    chains = [(h, u) for h in range(nh) for u in range(2)]

    def head_cols(h):
        return slice(h * MLA_QK_PAD, (h + 1) * MLA_QK_PAD)

    def softmax_pv(carry_c, s, vt):
        m, l, acc = carry_c
        m_new = jnp.maximum(m, jnp.max(s, axis=0, keepdims=True))
        alpha = jnp.exp2(m - m_new)
        p = jnp.exp2(s - m_new)
        l = alpha * l + jnp.sum(p, axis=0, keepdims=True)
        acc = alpha * acc + jnp.dot(vt, p.astype(BF16), preferred_element_type=F32)
        return m_new, l, acc

    def step(j, carry):
        off = pl.multiple_of(j * tq, tq)
        ss = []
        for h in range(nh):
            s = jnp.dot(k_ref[pl.ds(off, tq), head_cols(h)], q_ref[head_cols(h), :],
                        preferred_element_type=F32)
            ss += [s[:, :th], s[:, th:]]
        return tuple(softmax_pv(carry[n], ss[n], vt_ref[j, chains[n][0]])
                     for n in range(len(chains)))

    def diagonal_step(carry):
        off = pl.multiple_of(i * tq, tq)
        ss, vts = [], []
        for h, u in chains:
            nk = th if u == 0 else tq
            s = jnp.dot(k_ref[pl.ds(off, nk), head_cols(h)],
                        q_ref[head_cols(h), u * th:(u + 1) * th],
                        preferred_element_type=F32)
            kidx = lax.broadcasted_iota(jnp.int32, (nk, th), 0)
            qidx = lax.broadcasted_iota(jnp.int32, (nk, th), 1) + u * th
            ss.append(jnp.where(kidx <= qidx, s, NEG_BIG))
            vts.append(vt_ref[i, h][:, :nk])
        return tuple(softmax_pv(carry[n], ss[n], vts[n]) for n in range(len(chains)))

    init = tuple((jnp.full((1, th), NEG_BIG, F32), jnp.zeros((1, th), F32),
                  jnp.zeros((MLA_V, th), F32)) for _ in chains)
    carry = lax.fori_loop(0, i, step, init)
    carry = diagonal_step(carry)
    for n, (h, u) in enumerate(chains):
        _, l, acc = carry[n]
        y = (acc / l).T
        rs = slice(u * th, (u + 1) * th)
        cs = slice(h * MLA_V, (h + 1) * MLA_V)
        o_ref[rs, cs] = (_silu(z_ref[rs, cs]) * y).astype(BF16)


def _attention(q_t, k, v_t, proj, batch, seq):
    m = k.shape[0]
    nq = seq // TQ
    nh = ATTN_HEADS_PER_STEP
    v_t5 = v_t.reshape(batch, nq, MLA_HEADS, MLA_V, TQ)
    return pl.pallas_call(
        functools.partial(_attn_kernel, tq=TQ),
        grid=(batch, MLA_HEADS // nh, nq),
        in_specs=[
            pl.BlockSpec((None, nh * MLA_QK_PAD, TQ), lambda b, h, i: (b * nq + i, h, 0)),
            pl.BlockSpec((seq, nh * MLA_QK_PAD), lambda b, h, i: (b, h)),
            pl.BlockSpec((None, nq, nh, MLA_V, TQ), lambda b, h, i: (b, 0, h, 0, 0)),
            pl.BlockSpec((TQ, nh * MLA_V),
                         lambda b, h, i: (b * nq + i, OFF_MLAZ // (nh * MLA_V) + h)),
        ],
        out_specs=pl.BlockSpec((TQ, nh * MLA_V), lambda b, h, i: (b * nq + i, h)),
        out_shape=jax.ShapeDtypeStruct((m, MLA_WIDTH), BF16),
        compiler_params=_params("parallel", "parallel", "arbitrary"),
        name="mla_attn",
    )(q_t, k, v_t5, proj)


def _shift_rows(x, prev8, s):
    rolled = pltpu.roll(x, s, 0)
    prev_rolled = pltpu.roll(prev8, s, 0)
    rid = lax.broadcasted_iota(jnp.int32, (SUBLANES, x.shape[1]), 0)
    top = jnp.where(rid < s, prev_rolled, rolled[0:SUBLANES])
    return jnp.concatenate([top, rolled[SUBLANES:]], axis=0)


def _mlstm_kernel(ib_ref, fb_ref, q_ref, k_ref, v_ref, o_ref, z_ref, g_ref, w_ref, b_ref, ng_ref,
                  out_ref, ct_ref, n_ref, m_ref, qp_ref, kp_ref, *, layer, tt, chunk):
    @pl.when(pl.program_id(1) == 0)
    def _():
        for ref in (ct_ref, n_ref, m_ref, qp_ref, kp_ref):
            ref[...] = jnp.zeros_like(ref)

    hd = ML_HEAD_DIM
    w_all = w_ref[...]
    b_all = b_ref[...]

    def conv_silu(x_ref, prev_ref, w, b):
        x = x_ref[...]
        prev8 = prev_ref[...]
        y = b + w[ML_CONV - 1:ML_CONV] * x
        for s in range(1, ML_CONV):
            y = y + w[ML_CONV - 1 - s:ML_CONV - s] * _shift_rows(x, prev8, s)
        prev_ref[...] = x[tt - SUBLANES:tt]
        return _silu(y)

    q_all = conv_silu(q_ref, qp_ref, w_all[:, :ML_WIDTH], b_all[:, :ML_WIDTH]) * (hd ** -0.5)
    k_all = conv_silu(k_ref, kp_ref, w_all[:, ML_WIDTH:], b_all[:, ML_WIDTH:])
    li_rows = [g_ref[h] + ib_ref[layer, h] for h in range(ML_HEADS)]
    lf_rows = [_log_sigmoid(g_ref[ML_HEADS + h] + fb_ref[layer, h]) for h in range(ML_HEADS)]

    L = chunk
    rows = lax.broadcasted_iota(jnp.int32, (L, L), 0)
    cols = lax.broadcasted_iota(jnp.int32, (L, L), 1)
    tri = rows >= cols
    eye = rows == cols
    nchunk = tt // L
    upper = jnp.where(rows <= cols, 1.0, 0.0).astype(BF16)
    pad_rows = jnp.zeros((SUBLANES - ML_HEADS, L), F32)
    insts = []
    for c in range(nchunk):
        sl = slice(c * L, (c + 1) * L)
        lf8 = jnp.concatenate([lf_rows[h][:, sl] for h in range(ML_HEADS)] + [pad_rows], axis=0)
        b_rows = jnp.zeros((SUBLANES, L), F32)
        rest = lf8
        for _ in range(3):
            piece = rest.astype(BF16)
            b_rows = b_rows + jnp.dot(piece, upper, preferred_element_type=F32)
            rest = rest - piece.astype(F32)
        for h in range(ML_HEADS):
            cs = slice(h * hd, (h + 1) * hd)
            li_row = li_rows[h][:, sl]
            b_row = b_rows[h:h + 1, :]
            b_col = jnp.sum(jnp.where(eye, jnp.broadcast_to(b_row, (L, L)), 0.0),
                            axis=-1, keepdims=True)
            li_col = jnp.sum(jnp.where(eye, jnp.broadcast_to(li_row, (L, L)), 0.0),
                             axis=-1, keepdims=True)
            dmat = jnp.where(tri, b_col - b_row + li_row, NEG_BIG)
            m_loc = jnp.max(dmat, axis=-1, keepdims=True)
            b_last = b_col[L - 1:L, :]
            g_col = b_last - b_col + li_col
            g_max = jnp.max(g_col, axis=0, keepdims=True)
            kc = k_all[sl, cs]
            insts.append(dict(c=c, h=h, sl=sl, cs=cs, qc=q_all[sl, cs].astype(BF16),
                              kc=kc.astype(BF16), vc=v_ref[sl, cs].astype(BF16),
                              pm=jnp.exp(dmat - m_loc), m_loc=m_loc, b_col=b_col, b_last=b_last,
                              g_max=g_max, kwg=kc * jnp.exp(g_col - g_max)))
    for s in insts:
        s["smat"] = lax.dot_general(s["qc"], s["kc"], _NT, preferred_element_type=F32) * s["pm"]
    for s in insts:
        s["sv"] = jnp.dot(s["smat"].astype(BF16), s["vc"], preferred_element_type=F32)
        s["ssum"] = jnp.sum(s["smat"], axis=-1, keepdims=True)
        s["kv"] = jnp.dot(s["kwg"].T.astype(BF16), s["vc"], preferred_element_type=F32)
        s["ksum"] = jnp.sum(s["kwg"], axis=0, keepdims=True)

    for c in range(nchunk):
        for s in insts[c * ML_HEADS:(c + 1) * ML_HEADS]:
            h = s["h"]
            ct = ct_ref[h]
            n_row = n_ref[h]
            m_prev = m_ref[h][:, 0:1]
            m_inter = s["b_col"] + m_prev
            m_t = jnp.maximum(m_inter, s["m_loc"])
            inter = jnp.exp(m_inter - m_t)
            loc = jnp.exp(s["m_loc"] - m_t)
            num = (inter * jnp.dot(s["qc"], ct.astype(BF16), preferred_element_type=F32)
                   + loc * s["sv"])
            den = (inter * jnp.sum(s["qc"].astype(F32) * n_row, axis=-1, keepdims=True)
                   + loc * s["ssum"])
            s["hh"] = num / jnp.maximum(jnp.abs(den), jnp.exp(-m_t))
            m_new = jnp.maximum(s["b_last"] + m_prev, s["g_max"])
            decay = jnp.exp(s["b_last"] + m_prev - m_new)
            scale = jnp.exp(s["g_max"] - m_new)
            ct_ref[h] = decay * ct + scale * s["kv"]
            n_ref[h] = decay * n_row + scale * s["ksum"]
            m_ref[h] = jnp.broadcast_to(m_new, (1, hd))
    for s in insts:
        sl, cs = s["sl"], s["cs"]
        hh = s["hh"]
        mu = jnp.mean(hh, axis=-1, keepdims=True)
        dd = hh - mu
        var = jnp.mean(dd * dd, axis=-1, keepdims=True)
        yn = dd * lax.rsqrt(var + NORM_EPS) * ng_ref[:, cs]
        out_ref[sl, cs] = (_silu(z_ref[sl, cs]) * (_sigmoid(o_ref[sl, cs]) * yn)).astype(BF16)


def _mlstm(proj, gates_t, i_bias, f_bias, conv_w, conv_b3, norm_g3, layer, batch, seq):
    m = proj.shape[0]
    tt = min(ML_TT, seq)
    nt = seq // tt
    hd = ML_HEAD_DIM
    w = ML_WIDTH

    def col(off):
        return pl.BlockSpec((tt, w), lambda b, t: (b * nt + t, off // w))

    smem = pl.BlockSpec(memory_space=pltpu.SMEM)
    return pl.pallas_call(
        functools.partial(_mlstm_kernel, layer=layer, tt=tt, chunk=min(ML_L, tt)),
        grid=(batch, nt),
        in_specs=[
            smem, smem,
            col(OFF_QK), col(OFF_QK + w), col(OFF_MLV), col(OFF_MLO), col(OFF_MLZ),
            pl.BlockSpec((None, 2 * ML_HEADS, 1, tt), lambda b, t: (b, 0, 0, t)),
            pl.BlockSpec((None, ML_CONV, 2 * w), lambda b, t: (layer, 0, 0)),
            pl.BlockSpec((None, 1, 2 * w), lambda b, t: (layer, 0, 0)),
            pl.BlockSpec((None, 1, w), lambda b, t: (layer, 0, 0)),
        ],
        out_specs=pl.BlockSpec((tt, w), lambda b, t: (b * nt + t, 0)),
        out_shape=jax.ShapeDtypeStruct((m, w), BF16),
        scratch_shapes=[
            pltpu.VMEM((ML_HEADS, hd, hd), F32),
            pltpu.VMEM((ML_HEADS, 1, hd), F32),
            pltpu.VMEM((ML_HEADS, 1, hd), F32),
            pltpu.VMEM((SUBLANES, w), F32),
            pltpu.VMEM((SUBLANES, w), F32),
        ],
        compiler_params=_params("parallel", "arbitrary"),
        name="mlstm",
    )(i_bias, f_bias, proj, proj, proj, proj, proj, gates_t, conv_w, conv_b3, norm_g3)


RW_PRE_PASSES = 1


def _rwkv_pre_kernel(*refs, has_vres, tm):
    if has_vres:
        (wa_ref, r_ref, k_ref, v_ref, mu_wa, mu_r, mu_k, mu_v, w0_ref, w2_ref, a0_ref, a2_ref,
         kk_ref, ka_ref, e_ref, vf_ref, v0_ref, v1_ref, v2_ref,
         r_out, ld_out, kh_out, v_out, kn_out, b_out, c_wa, c_r, c_k, c_v) = refs
    else:
        (wa_ref, r_ref, k_ref, v_ref, mu_wa, mu_r, mu_k, mu_v, w0_ref, w2_ref, a0_ref, a2_ref,
         kk_ref, ka_ref, e_ref,
         r_out, ld_out, kh_out, v_out, kn_out, b_out, c_wa, c_r, c_k, c_v) = refs

    @pl.when(pl.program_id(1) == 0)
    def _():
        for c in (c_wa, c_r, c_k, c_v):
            c[...] = jnp.zeros_like(c)

    def mix(x_ref, c_ref, mu_ref):
        x = x_ref[...]
        rolled = pltpu.roll(x, 1, 0)
        rid = lax.broadcasted_iota(jnp.int32, (SUBLANES, x.shape[1]), 0)
        prev_last = jnp.broadcast_to(c_ref[SUBLANES - 1:SUBLANES, :], (SUBLANES, x.shape[1]))
        top = jnp.where(rid == 0, prev_last, rolled[0:SUBLANES])
        xprev = jnp.concatenate([top, rolled[SUBLANES:]], axis=0)
        c_ref[...] = x[tm - SUBLANES:tm]
        return x + mu_ref[...] * (xprev - x)

    xwa = mix(wa_ref, c_wa, mu_wa)
    r = mix(r_ref, c_r, mu_r)
    k = mix(k_ref, c_k, mu_k)
    v = mix(v_ref, c_v, mu_v)
    zw = w0_ref[...] + _mm(jnp.tanh(xwa), w2_ref[...], RW_PRE_PASSES)
    za = a0_ref[...] + _mm(xwa, a2_ref[...], RW_PRE_PASSES)
    log_w = _log_sigmoid(zw) - 0.5
    ld_out[...] = -jnp.exp(log_w)
    a = _sigmoid(za)
    if has_vres:
        gate = _sigmoid(v0_ref[...] + _mm(_mm(v, v1_ref[...], RW_PRE_PASSES), v2_ref[...],
                                          RW_PRE_PASSES))
        v = v + (vf_ref[...] - v) * gate
    kk = k * kk_ref[...]
    ss = _mm(kk * kk, e_ref[...])
    kn = kk / jnp.maximum(jnp.sqrt(ss), 1e-12)
    r_out[...] = r
    kh_out[...] = k * (1.0 + (a - 1.0) * ka_ref[...])
    v_out[...] = v
    kn_out[...] = kn
    b_out[...] = kn * a


RW_GROUP = 4


def _rwkv_scan_kernel(r_ref, ld_ref, kh_ref, v_ref, kn_ref, b_ref, z_ref, rk_ref, g_ref, bias_ref,
                      e_ref, out_ref, m_ref, ac_ref, gc_ref, qs_ref, y_ref, *, tt, chunk):
    @pl.when(pl.program_id(1) == 0)
    def _():
        m_ref[...] = jnp.zeros_like(m_ref)

    L = chunk
    L2 = 2 * L
    nchunk = tt // L
    npair = RW_WIDTH // LANES
    rows = lax.broadcasted_iota(jnp.int32, (L, L), 0)
    cols = lax.broadcasted_iota(jnp.int32, (L, L), 1)
    tri_b = jnp.where(rows >= cols, 1.0, 0.0).astype(BF16)
    lane = lax.broadcasted_iota(jnp.int32, (1, LANES), 1)
    m0 = jnp.where(lane < RW_HEAD_DIM, 1.0, 0.0)
    m1 = 1.0 - m0
    r2 = lax.broadcasted_iota(jnp.int32, (L2, L2), 0)
    c2 = lax.broadcasted_iota(jnp.int32, (L2, L2), 1)
    same_blk = (r2 < L) == (c2 < L)
    t2 = jnp.where(r2 < L, r2, r2 - L)
    s2 = jnp.where(c2 < L, c2, c2 - L)
    mask_strict = jnp.logical_and(same_blk, t2 > s2)
    mask_incl = jnp.logical_and(same_blk, t2 >= s2)
    eye2 = lax.broadcasted_iota(jnp.int32, (LANES, LANES), 0) == lax.broadcasted_iota(
        jnp.int32, (LANES, LANES), 1)
    n_double = max(1, (L - 1).bit_length())

    def stack(x):
        return jnp.concatenate([x * m0, x * m1], axis=0)

    def prepare(p, c):
        sl = slice(c * L, (c + 1) * L)
        cs = slice(p * LANES, (p + 1) * LANES)
        ld = ld_ref[sl, cs]
        r, kh, v, kn, b = r_ref[sl, cs], kh_ref[sl, cs], v_ref[sl, cs], kn_ref[sl, cs], b_ref[sl, cs]
        ld_hi, ld_lo = _split_bf16(ld)
        logp = (jnp.dot(tri_b, ld_hi, preferred_element_type=F32)
                + jnp.dot(tri_b, ld_lo, preferred_element_type=F32))
        cmid = logp[L // 2 - 1:L // 2, :]
        e = logp - cmid
        e_last = e[L - 1:L, :]
        p_mid = jnp.exp(cmid)
        at_s = stack(-kn * jnp.exp(e - ld))
        rt_s = stack(r * jnp.exp(e))
        inv = jnp.exp(-e)
        tail = jnp.exp(e_last - e)
        lhs = jnp.concatenate([at_s, rt_s], axis=0).astype(BF16)
        rhs = jnp.concatenate([stack(b * inv), stack(kh * inv)], axis=0).astype(BF16)
        hat_t = jnp.concatenate([stack(b * tail), stack(kh * tail)], axis=0).T.astype(BF16)
        return dict(sl=sl, cs=cs, idx=p * nchunk + c, lhs=lhs, rhs=rhs, hat_t=hat_t,
                    at_true=at_s * p_mid, rt_true=rt_s * p_mid, v_s=stack(v).astype(BF16),
                    p_last=jnp.exp(e_last + cmid))

    def coeff_stages(cg):
        insts = [prepare(p, cg * RW_GROUP + g) for p in range(npair) for g in range(RW_GROUP)]
        for s in insts:
            aa = lax.dot_general(s["lhs"], s["rhs"], _NT, preferred_element_type=F32)
            s["nmat"] = jnp.where(mask_strict, aa[0:L2, 0:L2], 0.0).astype(BF16)
            s["a_ak"] = jnp.where(mask_strict, aa[0:L2, L2:], 0.0).astype(BF16)
            s["a_r"] = jnp.concatenate([jnp.where(mask_incl, aa[L2:, 0:L2], 0.0),
                                        jnp.where(mask_incl, aa[L2:, L2:], 0.0)],
                                       axis=1).astype(BF16)
        yield
        for s in insts:
            s["x"] = jnp.concatenate(
                [s["at_true"], jnp.dot(s["a_ak"], s["v_s"], preferred_element_type=F32)],
                axis=1)
        yield
        for it in range(n_double):
            for s in insts:
                s["x"] = s["x"] + jnp.dot(s["nmat"], s["x"].astype(BF16),
                                          preferred_element_type=F32)
            if it + 1 < n_double:
                for s in insts:
                    s["nmat"] = jnp.dot(s["nmat"], s["nmat"],
                                        preferred_element_type=F32).astype(BF16)
            yield
        for s in insts:
            big_l = jnp.concatenate([s["a_r"], s["hat_t"]], axis=0)
            big_r = jnp.concatenate(
                [s["x"].astype(BF16),
                 jnp.concatenate([jnp.zeros((L2, LANES), BF16), s["v_s"]], axis=1)], axis=0)
            res = jnp.dot(big_l, big_r, preferred_element_type=F32)
            q_s = s["rt_true"] + res[0:L2, :LANES]
            y0_s = res[0:L2, LANES:]
            qs_ref[s["sl"], s["cs"]] = q_s[0:L] + q_s[L:]
            y_ref[s["sl"], s["cs"]] = y0_s[0:L] + y0_s[L:]
            ac_ref[s["idx"]] = (jnp.where(eye2, jnp.broadcast_to(s["p_last"], (LANES, LANES)), 0.0)
                                + res[L2:, :LANES])
            gc_ref[s["idx"]] = res[L2:, LANES:]

    def chain_step(c):
        sl = slice(c * L, (c + 1) * L)
        for p in range(npair):
            cs = slice(p * LANES, (p + 1) * LANES)
            idx = p * nchunk + c
            mp = m_ref[p]
            y_ref[sl, cs] = _mm(qs_ref[sl, cs], mp) + y_ref[sl, cs]
            m_ref[p] = _mm(ac_ref[idx], mp) + gc_ref[idx]

    pending = []
    for cg in range(nchunk // RW_GROUP):
        for _ in coeff_stages(cg):
            if pending:
                chain_step(pending.pop(0))
        for c in pending:
            chain_step(c)
        pending = [cg * RW_GROUP + g for g in range(RW_GROUP)]
    for c in pending:
        chain_step(c)

    e_b = e_ref[...]
    inv_n = 1.0 / RW_HEAD_DIM

    def head_sum(x):
        return jnp.dot(x.astype(BF16), e_b, preferred_element_type=F32)

    for p in range(npair):
        cs = slice(p * LANES, (p + 1) * LANES)
        y = y_ref[:, cs]
        mu = head_sum(y) * inv_n
        d = y - mu
        var = head_sum(d * d) * inv_n
        yn = d * lax.rsqrt(var + RW_GN_EPS) * g_ref[:, cs] + bias_ref[:, cs]
        v_all = v_ref[:, cs]
        bonus = head_sum(r_ref[:, cs] * kh_ref[:, cs] * rk_ref[:, cs])
        out_ref[:, cs] = (_silu(z_ref[:, cs]) * (yn + bonus * v_all)).astype(BF16)


def _rwkv_kernel(*refs, has_vres, tt, chunk):
    n_mix = 19 if has_vres else 15
    mix_in = refs[:n_mix]
    z_ref, rk_ref, g_ref, bias_ref, e_ref = refs[n_mix:n_mix + 5]
    n_out = 1 if has_vres else 2
    outs = refs[n_mix + 5:n_mix + 5 + n_out]
    (c_wa, c_r, c_k, c_v, r_s, ld_s, kh_s, v_s, kn_s, b_s,
     m_ref, ac_ref, gc_ref, qs_ref, y_ref) = refs[n_mix + 5 + n_out:]
    _rwkv_pre_kernel(*mix_in, r_s, ld_s, kh_s, v_s, kn_s, b_s, c_wa, c_r, c_k, c_v,
                     has_vres=has_vres, tm=tt)
    if not has_vres:
        outs[1][...] = v_s[...]
    _rwkv_scan_kernel(r_s, ld_s, kh_s, v_s, kn_s, b_s, z_ref, rk_ref, g_ref, bias_ref, e_ref,
                      outs[0], m_ref, ac_ref, gc_ref, qs_ref, y_ref, tt=tt, chunk=chunk)


def _rwkv(proj_a, proj_b, v_first, mu_wa, mu_r, mu_k, mu_v, w0, w2p, a0, a2p, k_k, k_a, e_head,
          v0, v1p, v2p, r_k3, ln_g3, ln_b3, e_pair, layer, batch, seq):
    m = proj_a.shape[0]
    tt = min(RW_TT, seq)
    nt = seq // tt
    chunk = min(RW_L, tt)
    has_vres = layer > 0
    w = RW_WIDTH
    npair = w // LANES

    def colblk(off, width):
        return pl.BlockSpec((tt, width), lambda b, t: (b * nt + t, off // width))

    def lay(shape):
        return pl.BlockSpec((None,) + shape, lambda b, t: (layer,) + (0,) * len(shape))

    def lay1(shape):
        return pl.BlockSpec((None,) + shape, lambda b, t: (layer - 1,) + (0,) * len(shape))

    row = pl.BlockSpec((tt, w), lambda b, t: (b * nt + t, 0))
    in_specs = [colblk(OFF_WA, LANES), colblk(OFF_RWR, w), colblk(OFF_RWK, w), colblk(OFF_RWV, w),
                lay((1, LANES)), lay((1, w)), lay((1, w)), lay((1, w)),
                lay((1, w)), lay((LANES, w)), lay((1, w)), lay((LANES, w)),
                lay((1, w)), lay((1, w)),
                pl.BlockSpec((w, w), lambda b, t: (0, 0))]
    args = [proj_a, proj_b, proj_b, proj_b, mu_wa, mu_r, mu_k, mu_v, w0, w2p, a0, a2p, k_k, k_a, e_head]
    if has_vres:
        in_specs += [row, lay1((1, w)), lay1((w, LANES)), lay1((LANES, w))]
        args += [v_first, v0, v1p, v2p]
    in_specs += [colblk(OFF_RWZ, w), lay((1, w)), lay((1, w)), lay((1, w)),
                 pl.BlockSpec((LANES, LANES), lambda b, t: (0, 0))]
    args += [proj_b, r_k3, ln_g3, ln_b3, e_pair]
    out_specs = [row]
    out_shape = [jax.ShapeDtypeStruct((m, w), BF16)]
    if not has_vres:
        out_specs.append(row)
        out_shape.append(jax.ShapeDtypeStruct((m, w), F32))
    scratch = [pltpu.VMEM((SUBLANES, LANES), F32)] + [pltpu.VMEM((SUBLANES, w), F32)] * 3
    scratch += [pltpu.VMEM((tt, w), F32)] * 6
    scratch += [
        pltpu.VMEM((npair, LANES, LANES), F32),
        pltpu.VMEM((npair * (tt // chunk), LANES, LANES), F32),
        pltpu.VMEM((npair * (tt // chunk), LANES, LANES), F32),
        pltpu.VMEM((tt, w), F32),
        pltpu.VMEM((tt, w), F32),
    ]
    return pl.pallas_call(
        functools.partial(_rwkv_kernel, has_vres=has_vres, tt=tt, chunk=chunk),
        grid=(batch, nt),
        in_specs=in_specs,
        out_specs=out_specs,
        out_shape=out_shape,
        scratch_shapes=scratch,
        compiler_params=_params("parallel", "arbitrary"),
        name="rwkv",
    )(*args)


_W_IN_PARTS = ((0, 1024), (1024, 512), (1544, 512), (2056, 512), (2568, 512), (3080, 256),
               (3336, 64), (1536, 8), (None, LANES - MLA_ROPE - 2 * ML_HEADS), (5960, 128),
               (4424, 512), (4936, 512), (5448, 512), (6088, 512), (3400, 1024))
TK_PREP = 256


def _permute_w_in_kernel(src_ref, dst_a, dst_b):
    tk = src_ref.shape[1]
    off = 0
    pending = []

    def put(off, width, value):
        if off < D_IN_A:
            dst_a[:, off:off + width] = value
        else:
            dst_b[:, off - D_IN_A:off - D_IN_A + width] = value

    for start, width in _W_IN_PARTS:
        if start is None:
            pending.append(jnp.zeros((width, tk), F32))
        elif width % LANES:
            pending.append(src_ref[start:start + width, :])
        else:
            put(off, width, src_ref[start:start + width, :].T.astype(BF16))
            off += width
            continue
        rows = sum(p.shape[0] for p in pending)
        if rows == LANES:
            put(off, LANES, jnp.concatenate(pending, axis=0).T.astype(BF16))
            off += LANES
            pending = []


def _permute_w_in(w_in):
    depth, d_model, d_in = w_in.shape
    w_t = jnp.swapaxes(w_in, 1, 2)
    return pl.pallas_call(
        _permute_w_in_kernel,
        grid=(depth, d_model // TK_PREP),
        in_specs=[pl.BlockSpec((None, d_in, TK_PREP), lambda l, i: (l, 0, i))],
        out_specs=[pl.BlockSpec((None, TK_PREP, D_IN_A), lambda l, i: (l, i, 0)),
                   pl.BlockSpec((None, TK_PREP, D_IN_B), lambda l, i: (l, i, 0))],
        out_shape=[jax.ShapeDtypeStruct((depth, d_model, D_IN_A), BF16),
                   jax.ShapeDtypeStruct((depth, d_model, D_IN_B), BF16)],
        compiler_params=_params("parallel", "parallel"),
        name="permute_w_in",
    )(w_t)


def _permute_w_uq(w_uq):
    d = w_uq.shape[0]
    w = w_uq.reshape(d, MLA_Q_RANK, MLA_HEADS, MLA_NOPE + MLA_ROPE)
    w = jnp.pad(w, ((0, 0), (0, 0), (0, 0), (0, MLA_QK_PAD - MLA_NOPE - MLA_ROPE)))
    return w.reshape(d, MLA_Q_RANK, MLA_HEADS * MLA_QK_PAD).astype(BF16)


def _split_w_ukv(w_ukv):
    d = w_ukv.shape[0]
    w = w_ukv.reshape(d, MLA_KV_RANK, MLA_HEADS, MLA_NOPE + MLA_V)
    wk = w[..., :MLA_NOPE].reshape(d, MLA_KV_RANK, MLA_HEADS * MLA_NOPE)
    wv = w[..., MLA_NOPE:].reshape(d, MLA_KV_RANK, MLA_HEADS * MLA_V)
    return wk.astype(BF16), wv.astype(BF16)


def _rope_tables(positions):
    inv_freq = jnp.power(ROPE_BASE, -jnp.arange(0, MLA_ROPE, 2, dtype=F32) / MLA_ROPE)
    ang = positions.astype(F32).reshape(-1, 1) * inv_freq
    cos, sin = jnp.cos(ang), jnp.sin(ang)
    z32 = jnp.zeros_like(cos)
    z64 = jnp.zeros((ang.shape[0], LANES - MLA_ROPE), F32)
    cosf = jnp.concatenate([cos, cos, z64], axis=-1)
    s1 = jnp.concatenate([-sin, z32, z64], axis=-1)
    s2 = jnp.concatenate([z32, sin, z64], axis=-1)
    return cosf, s1, s2


def kernel(x, positions, norm_g, w_in, ml_conv_w, ml_conv_b, ml_i_bias, ml_f_bias, ml_norm_g,
           mla_q_norm_g, mla_w_uq, mla_kv_norm_g, mla_w_ukv, rw_mu, rw_w0, rw_w2, rw_a0, rw_a2,
           rw_v0, rw_v1, rw_v2, rw_k_k, rw_k_a, rw_r_k, rw_ln_g, rw_ln_b, w_out, final_norm_g):
    batch, seq, _ = x.shape
    depth = w_in.shape[0]
    m = batch * seq
    x2 = x.reshape(m, D_MODEL)

    w_in_a, w_in_b = _permute_w_in(w_in)
    w_uq_p = _permute_w_uq(mla_w_uq)
    w_uk_p, w_uv_p = _split_w_ukv(mla_w_ukv)
    w_out_b = w_out.astype(BF16)
    cosf, s1, s2 = _rope_tables(positions)

    def row3(a):
        return a.reshape(a.shape[0], 1, a.shape[1])

    norm_g3 = row3(norm_g)
    conv_b3 = row3(ml_conv_b)
    ml_norm_g3 = row3(ml_norm_g)
    qg3, kvg3 = row3(mla_q_norm_g), row3(mla_kv_norm_g)
    w = RW_WIDTH
    mu_r, mu_k, mu_v = row3(rw_mu[:, 0:w]), row3(rw_mu[:, w:2 * w]), row3(rw_mu[:, 2 * w:3 * w])
    mu_wa = row3(rw_mu[:, 3 * w:])
    w0, a0, k_k, k_a = row3(rw_w0), row3(rw_a0), row3(rw_k_k), row3(rw_k_a)
    w2p = jnp.pad(rw_w2, ((0, 0), (0, LANES - RW_DECAY_RANK), (0, 0)))
    a2p = jnp.pad(rw_a2, ((0, 0), (LANES - RW_AAA_RANK, 0), (0, 0)))
    v0 = row3(rw_v0)
    v1p = jnp.pad(rw_v1, ((0, 0), (0, 0), (0, LANES - RW_MV_RANK)))
    v2p = jnp.pad(rw_v2, ((0, 0), (0, LANES - RW_MV_RANK), (0, 0)))
    r_k3 = rw_r_k.reshape(depth, 1, w)
    ln_g3, ln_b3 = row3(rw_ln_g), row3(rw_ln_b)
    hid = jnp.arange(w) // RW_HEAD_DIM
    e_head = (hid[:, None] == hid[None, :]).astype(F32)
    e_pair = e_head[:LANES, :LANES].astype(BF16)
    final_g3 = final_norm_g.reshape(1, 1, D_MODEL)

    v_first = None
    h = _rmsnorm(x2, norm_g3, 0)
    for layer in range(depth):
        proj_a = _inproj(h, w_in_a, layer, TN_PROJ_A)
        proj_b = _inproj(h, w_in_b, layer, TN_PROJ_B)
        gates_t = proj_a[:, OFF_KRG + MLA_ROPE:OFF_KRG + MLA_ROPE + 2 * ML_HEADS]
        gates_t = gates_t.reshape(batch, seq, 2 * ML_HEADS).transpose(0, 2, 1)
        gates_t = gates_t.reshape(batch, 2 * ML_HEADS, 1, seq)
        y_ml = _mlstm(proj_a, gates_t, ml_i_bias, ml_f_bias, ml_conv_w, conv_b3, ml_norm_g3,
                      layer, batch, seq)
        q, k, v = _mla_proj(proj_a, cosf, s1, s2, qg3, w_uq_p, kvg3, w_uk_p, w_uv_p, layer)
        y_mla = _attention(q, k, v, proj_b, batch, seq)
        rw_out = _rwkv(proj_a, proj_b, v_first, mu_wa, mu_r, mu_k, mu_v, w0, w2p, a0, a2p, k_k, k_a, e_head,
                       v0, v1p, v2p, r_k3, ln_g3, ln_b3, e_pair, layer, batch, seq)
        y_rw = rw_out[0]
        if layer == 0:
            v_first = rw_out[1]
        if layer == depth - 1:
            (x2,) = _outproj(x2, y_ml, y_mla, y_rw, w_out_b, final_g3, layer, True)
        else:
            x2, h = _outproj(x2, y_ml, y_mla, y_rw, w_out_b, norm_g3, layer, False)
    return x2.reshape(batch, seq, D_MODEL)
```
